```python
import jax, jax.numpy as jnp
from jax import lax
import numpy as np

D_MODEL = 1024
BATCH = 8
SEQ = 2048
DEPTH = 1
DEC_BATCH = 4
DEC_SEQ = 8192
PAST_LEN = 128

D_MIX = D_MODEL
D_A = D_MIX // 2
N_HEADS_A = 4
HEAD_A = D_A // N_HEADS_A
CHUNK_A = 128
D_B = D_MIX - D_A
N_HEADS_B = 4
HEAD_V_B = D_B // N_HEADS_B
D_K_B = D_B // 2
HEAD_K_B = D_K_B // N_HEADS_B
GATE_RANK = 16
GATE_NORMALIZER = 16.0
CHUNK_B = 64
EPS = 1e-6
SPLITS = [D_A, D_A, D_A, D_K_B, D_K_B, D_B, D_B, GATE_RANK, GATE_RANK]
D_IN = sum(SPLITS)

kernel_name = "hybrid_gmlp_gla_bidir_encoder"


def rmsnorm(x, g):
    xf = x.astype(jnp.float32)
    xf = xf * lax.rsqrt(jnp.mean(xf * xf, axis=-1, keepdims=True) + EPS)
    return (xf * g.astype(jnp.float32)).astype(x.dtype)


def layernorm(x, g):
    xf = x.astype(jnp.float32)
    xf = xf - jnp.mean(xf, axis=-1, keepdims=True)
    xf = xf * lax.rsqrt(jnp.mean(xf * xf, axis=-1, keepdims=True) + EPS)
    return (xf * g.astype(jnp.float32)).astype(x.dtype)


def spatial_gating(u, v, w_sp, b_sp, g_v):
    bsz, s, _ = u.shape
    v = layernorm(v, g_v)
    vc = v.reshape(bsz, s // CHUNK_A, CHUNK_A, N_HEADS_A, HEAD_A)
    sv = jnp.einsum('hij,bcjhd->bcihd', w_sp, vc) + b_sp.T[None, None, :, :, None]
    return u * sv.reshape(bsz, s, D_A).astype(u.dtype)


def gla_direction(q, k, v, g, strict):
    bsz, s, h, dk = q.shape
    dv = v.shape[-1]
    n = s // CHUNK_B
    f32 = jnp.float32
    q = q.astype(f32).reshape(bsz, n, CHUNK_B, h, dk)
    k = k.astype(f32).reshape(bsz, n, CHUNK_B, h, dk)
    v = v.astype(f32).reshape(bsz, n, CHUNK_B, h, dv)
    g = g.astype(f32).reshape(bsz, n, CHUNK_B, h, dk)
    bcum = jnp.cumsum(g, axis=2)
    b_last = bcum[:, :, -1:]
    q_in = q * jnp.exp(bcum)
    k_in = k * jnp.exp(-bcum)
    scores = jnp.einsum('bnchd,bnjhd->bnhcj', q_in, k_in)
    idx = jnp.arange(CHUNK_B)
    mask = (idx[:, None] > idx[None, :]) if strict else (idx[:, None] >= idx[None, :])
    scores = jnp.where(mask, scores, 0.0)
    o_intra = jnp.einsum('bnhcj,bnjhe->bnche', scores, v)
    k_dec = k * jnp.exp(b_last - bcum)
    chunk_kv = jnp.einsum('bnjhd,bnjhe->bnhde', k_dec, v)
    decay = jnp.exp(b_last[:, :, 0])

    def step(state, inp):
        kv_c, dec_c = inp
        return dec_c[..., None] * state + kv_c, state

    init = jnp.zeros((bsz, h, dk, dv), f32)
    _, states = lax.scan(step, init, (jnp.moveaxis(chunk_kv, 1, 0), jnp.moveaxis(decay, 1, 0)))
    states = jnp.moveaxis(states, 0, 1)
    o_inter = jnp.einsum('bnchd,bnhde->bnche', q_in, states)
    return (o_intra + o_inter).reshape(bsz, s, h, dv)


def hybrid_layer(x, norm_pre, w_in, w_sp, b_sp, g_v_a, w_gk_fwd, b_gk_fwd,
                 w_gk_bwd, b_gk_bwd, g_norm_b, w_out, norm_post):
    bsz, s, _ = x.shape
    h = rmsnorm(x, norm_pre)
    p = h @ w_in
    offs = np.cumsum(SPLITS)[:-1].tolist()
    u_a, v_a, z_a, q_b, k_b, v_b, z_b, lr_f, lr_b = jnp.split(p, offs, axis=-1)

    out_a = spatial_gating(jax.nn.gelu(u_a), jax.nn.gelu(v_a), w_sp, b_sp, g_v_a)
    out_a = out_a * jax.nn.silu(z_a)

    q = q_b.reshape(bsz, s, N_HEADS_B, HEAD_K_B) * (HEAD_K_B ** -0.5)
    k = k_b.reshape(bsz, s, N_HEADS_B, HEAD_K_B)
    v = v_b.reshape(bsz, s, N_HEADS_B, HEAD_V_B)
    g_f = jax.nn.log_sigmoid((lr_f @ w_gk_fwd + b_gk_fwd).astype(jnp.float32)) / GATE_NORMALIZER
    g_b = jax.nn.log_sigmoid((lr_b @ w_gk_bwd + b_gk_bwd).astype(jnp.float32)) / GATE_NORMALIZER
    g_f = g_f.reshape(bsz, s, N_HEADS_B, HEAD_K_B)
    g_b = g_b.reshape(bsz, s, N_HEADS_B, HEAD_K_B)
    o_fwd = gla_direction(q, k, v, g_f, strict=False)
    flip = lambda t: jnp.flip(t, axis=1)
    o_bwd = flip(gla_direction(flip(q), flip(k), flip(v), flip(g_b), strict=True))
    o_b = rmsnorm((o_fwd + o_bwd).astype(x.dtype), g_norm_b)
    out_b = o_b.reshape(bsz, s, D_B) * jax.nn.silu(z_b)

    mixed = jnp.concatenate([out_a, out_b], axis=-1) @ w_out
    return x + rmsnorm(mixed, norm_post)


def setup_inputs(seed: int = 0) -> dict:
    key = jax.random.key(seed)
    ks = jax.random.split(key, 16)
    f32 = jnp.float32
    nrm = lambda k, shape, scale: jax.random.normal(k, shape, f32) * scale
    return {
        "x_prompt": nrm(ks[0], (BATCH, SEQ, D_MODEL), 1.0),
        "x_sample": nrm(ks[1], (DEC_BATCH, DEC_SEQ, D_MODEL), 1.0),
        "norm_pre": 1.0 + nrm(ks[2], (DEPTH, D_MODEL), 0.02),
        "w_in": nrm(ks[3], (DEPTH, D_MODEL, D_IN), D_MODEL ** -0.5),
        "w_sp": nrm(ks[4], (DEPTH, N_HEADS_A, CHUNK_A, CHUNK_A), 0.5 * CHUNK_A ** -0.5),
        "b_sp": 1.0 + nrm(ks[5], (DEPTH, N_HEADS_A, CHUNK_A), 0.02),
        "g_v_a": 1.0 + nrm(ks[6], (DEPTH, D_A), 0.02),
        "w_gk_fwd": nrm(ks[7], (DEPTH, GATE_RANK, D_K_B), GATE_RANK ** -0.5),
        "b_gk_fwd": nrm(ks[8], (DEPTH, D_K_B), 0.01),
        "w_gk_bwd": nrm(ks[9], (DEPTH, GATE_RANK, D_K_B), GATE_RANK ** -0.5),
        "b_gk_bwd": nrm(ks[10], (DEPTH, D_K_B), 0.01),
        "g_norm_b": 1.0 + nrm(ks[11], (DEPTH, HEAD_V_B), 0.02),
        "w_out": nrm(ks[12], (DEPTH, D_MIX, D_MODEL), D_MIX ** -0.5),
        "norm_post": 1.0 + nrm(ks[13], (DEPTH, D_MODEL), 0.02),
    }


def reference(x_prompt, x_sample, norm_pre, w_in, w_sp, b_sp, g_v_a, w_gk_fwd, b_gk_fwd,
              w_gk_bwd, b_gk_bwd, g_norm_b, w_out, norm_post):
    y_prompt = x_prompt
    y_sample = x_sample
    for l in range(DEPTH):
        params = (norm_pre[l], w_in[l], w_sp[l], b_sp[l], g_v_a[l], w_gk_fwd[l], b_gk_fwd[l],
                  w_gk_bwd[l], b_gk_bwd[l], g_norm_b[l], w_out[l], norm_post[l])
        y_prompt = hybrid_layer(y_prompt, *params)
        y_sample = hybrid_layer(y_sample, *params)
    return (y_prompt, y_sample)
```

```python
import functools

import jax
import jax.numpy as jnp
from jax import lax
from jax.experimental import pallas as pl
from jax.experimental.pallas import tpu as pltpu

F32 = jnp.float32
BF16 = jnp.bfloat16

D_MODEL = 1024
D_A = 512
N_HEADS_A = 4
HEAD_A = 128
CHUNK_A = 128
D_B = 512
N_HEADS_B = 4
HEAD_V = 128
D_K = 256
HEAD_K = 64
GATE_RANK = 16
GATE_NORMALIZER = 16.0
CHUNK_B = 64
EPS = 1e-6

OFF_U, OFF_V, OFF_ZA, OFF_Q, OFF_K, OFF_VB, OFF_ZB, OFF_LRF, OFF_LRB, D_IN = (
    0, 512, 1024, 1536, 1792, 2048, 2560, 3072, 3088, 3104)

LANES = 128
TILE = 256
N_CHUNK = TILE // CHUNK_B
N_PAIR = N_HEADS_B // 2
VMEM_LIMIT = 56 * 1024 * 1024


def _dot(a, b):
    return jnp.dot(a, b, preferred_element_type=F32)


def _dot_nt(a, b):
    return lax.dot_general(a, b, (((1,), (1,)), ((), ())), preferred_element_type=F32)


def _gelu(x):
    return 0.5 * x * (1.0 + jnp.tanh(0.7978845608028654 * (x + 0.044715 * (x * x * x))))


def _silu(x):
    return x * (0.5 * jnp.tanh(0.5 * x) + 0.5)


def _log_sigmoid(x):
    return jnp.minimum(x, 0.0) - jnp.log1p(jnp.exp(-jnp.abs(x)))


def _rms_scale(x):
    return lax.rsqrt(jnp.mean(x * x, axis=-1, keepdims=True) + EPS)


def _chunk_cumsum(tri_ref, g):
    hi = g.astype(BF16)
    lo = (g - hi.astype(F32)).astype(BF16)
    tri = tri_ref[...]
    return _dot(tri, hi) + _dot(tri, lo)


def _decayed_keys(k, bcum, backward):
    kd, dec = [], []
    for c in range(N_CHUNK):
        r0 = c * CHUNK_B
        last = r0 if backward else r0 + CHUNK_B - 1
        b_last = bcum[last:last + 1, :]
        kd.append(k[r0:r0 + CHUNK_B, :] * jnp.exp(b_last - bcum[r0:r0 + CHUNK_B, :]))
        dec.append(jnp.exp(b_last))
    return jnp.concatenate(kd, axis=0), dec


def _state_update(s_ref, kt, v_par, dec_row, c):
    cp, par = divmod(c, 2)
    parts = []
    for p in range(N_PAIR):
        full = _dot(kt[p * LANES:(p + 1) * LANES, cp * LANES:(cp + 1) * LANES],
                    v_par[cp][par][:, p * 2 * HEAD_V:(p + 1) * 2 * HEAD_V])
        parts.append(full[0:HEAD_K, 0:HEAD_V])
        parts.append(full[HEAD_K:2 * HEAD_K, HEAD_V:2 * HEAD_V])
    kv = jnp.concatenate(parts, axis=0)
    dec_col = jnp.transpose(jnp.broadcast_to(dec_row, (LANES, D_K)))
    s_ref[...] = dec_col * s_ref[...] + kv


def _parity_masked_values(vb):
    out = []
    row = lax.broadcasted_iota(jnp.int32, (2 * CHUNK_B, D_B), 0)
    for cp in range(N_CHUNK // 2):
        vp = vb[cp * 2 * CHUNK_B:(cp + 1) * 2 * CHUNK_B, :]
        zero = jnp.zeros_like(vp)
        out.append((jnp.where(row < CHUNK_B, vp, zero), jnp.where(row >= CHUNK_B, vp, zero)))
    return out


def _bwd_state_kernel(x_ref, npre_ref, wkv_ref, wlr_ref, wgk_ref, bgk_ref, utri_ref,
                      hb_ref, s_ref):
    j = pl.program_id(1)

    @pl.when(j == 0)
    def _():
        s_ref[...] = jnp.zeros_like(s_ref)

    hb_ref[0, 0] = s_ref[...]

    x = x_ref[0]
    h = (x * _rms_scale(x) * npre_ref[...]).astype(BF16)
    pkv = _dot(h, wkv_ref[...])
    k = pkv[:, 0:D_K]
    vb = pkv[:, D_K:D_K + D_B].astype(BF16)
    plr = _dot(h, wlr_ref[...])
    g = _log_sigmoid(_dot(plr.astype(BF16), wgk_ref[...]) + bgk_ref[...]) * (1.0 / GATE_NORMALIZER)
    bcum = _chunk_cumsum(utri_ref, g)
    kdec, dec = _decayed_keys(k, bcum, backward=True)
    kt = jnp.transpose(kdec).astype(BF16)
    v_par = _parity_masked_values(vb)
    for c in reversed(range(N_CHUNK)):
        _state_update(s_ref, kt, v_par, dec[c], c)


def _layer_kernel(x_ref, hb_ref, npre_ref, wmain_ref, wlr_ref, wsp_ref, bsp_ref, gva_ref,
                  wgk_ref, bgk_ref, ltri_ref, utri_ref, gnb_ref, wout_ref, npost_ref,
                  out_ref, sf_ref, sb_ref, o_ref, mix_ref):
    j = pl.program_id(1)

    @pl.when(j == 0)
    def _():
        sf_ref[...] = jnp.zeros_like(sf_ref)

    sb_ref[...] = hb_ref[0, 0]

    x = x_ref[0]
    h = (x * _rms_scale(x) * npre_ref[...]).astype(BF16)

    pa = _dot(h, wmain_ref[:, OFF_U:OFF_Q])
    u = _gelu(pa[:, 0:D_A])
    vg = _gelu(pa[:, D_A:2 * D_A])
    za = pa[:, 2 * D_A:3 * D_A]
    vc = vg - jnp.mean(vg, axis=-1, keepdims=True)
    vn = (vc * _rms_scale(vc) * gva_ref[...]).astype(BF16)
    for c in range(TILE // CHUNK_A):
        r0 = c * CHUNK_A
        for hd in range(N_HEADS_A):
            c0 = hd * HEAD_A
            sv = _dot(wsp_ref[hd], vn[r0:r0 + CHUNK_A, c0:c0 + HEAD_A]) + bsp_ref[hd]
            out_a = u[r0:r0 + CHUNK_A, c0:c0 + HEAD_A] * sv
            out_a = out_a * _silu(za[r0:r0 + CHUNK_A, c0:c0 + HEAD_A])
            mix_ref[r0:r0 + CHUNK_A, c0:c0 + HEAD_A] = out_a.astype(BF16)

    pb = _dot(h, wmain_ref[:, OFF_Q:OFF_LRF])
    q = pb[:, 0:D_K] * (HEAD_K ** -0.5)
    k = pb[:, D_K:2 * D_K]
    vb = pb[:, 2 * D_K:2 * D_K + D_B].astype(BF16)
    zb = pb[:, 2 * D_K + D_B:2 * D_K + 2 * D_B]
    plr = _dot(h, wlr_ref[...])
    g = _log_sigmoid(_dot(plr.astype(BF16), wgk_ref[...]) + bgk_ref[...]) * (1.0 / GATE_NORMALIZER)
    v_par = _parity_masked_values(vb)

    lane_head = lax.broadcasted_iota(jnp.int32, (CHUNK_B, D_K), 1) // HEAD_K
    lane_pos = lax.broadcasted_iota(jnp.int32, (CHUNK_B, D_K), 1) % CHUNK_B
    row_pos = lax.broadcasted_iota(jnp.int32, (CHUNK_B, D_K), 0)
    zero_blk = jnp.zeros((CHUNK_B, HEAD_V), BF16)

    for d in range(2):
        backward = d == 1
        s_ref = sb_ref if backward else sf_ref
        bcum = _chunk_cumsum(utri_ref if backward else ltri_ref, g[:, d * D_K:(d + 1) * D_K])
        q_in = (q * jnp.exp(bcum)).astype(BF16)
        k_in = (k * jnp.exp(-bcum)).astype(BF16)
        kdec, dec = _decayed_keys(k, bcum, backward)
        kt = jnp.transpose(kdec).astype(BF16)
        keep = (row_pos < lane_pos) if backward else (row_pos >= lane_pos)
        for c in (reversed(range(N_CHUNK)) if backward else range(N_CHUNK)):
            r0 = c * CHUNK_B
            q_c = q_in[r0:r0 + CHUNK_B, :]
            k_c = k_in[r0:r0 + CHUNK_B, :]
            k_bd = jnp.concatenate(
                [jnp.where(lane_head == hd, k_c, jnp.zeros_like(k_c)) for hd in range(N_HEADS_B)],
                axis=0)
            scores = _dot_nt(q_c, k_bd)
            p_c = jnp.where(keep, scores, 0.0).astype(BF16)
            s_bf = s_ref[...].astype(BF16)
            for p in range(N_PAIR):
                a0 = 2 * p * HEAD_V
                rhs = jnp.concatenate([
                    jnp.concatenate([vb[r0:r0 + CHUNK_B, a0:a0 + HEAD_V], zero_blk], axis=1),
                    jnp.concatenate([zero_blk, vb[r0:r0 + CHUNK_B, a0 + HEAD_V:a0 + 2 * HEAD_V]],
                                    axis=1),
                    jnp.concatenate([s_bf[2 * p * HEAD_K:(2 * p + 1) * HEAD_K, :], zero_blk], axis=1),
                    jnp.concatenate([zero_blk, s_bf[(2 * p + 1) * HEAD_K:(2 * p + 2) * HEAD_K, :]],
                                    axis=1),
                ], axis=0)
                lhs = jnp.concatenate([p_c[:, p * LANES:(p + 1) * LANES],
                                       q_c[:, p * LANES:(p + 1) * LANES]], axis=1)
                o_pair = _dot(lhs, rhs)
                if backward:
                    o_ref[r0:r0 + CHUNK_B, a0:a0 + 2 * HEAD_V] += o_pair
                else:
                    o_ref[r0:r0 + CHUNK_B, a0:a0 + 2 * HEAD_V] = o_pair
            _state_update(s_ref, kt, v_par, dec[c], c)

    o = o_ref[...]
    for hd in range(N_HEADS_B):
        c0 = hd * HEAD_V
        oh = o[:, c0:c0 + HEAD_V]
        out_b = oh * _rms_scale(oh) * gnb_ref[...] * _silu(zb[:, c0:c0 + HEAD_V])
        mix_ref[:, D_A + c0:D_A + c0 + HEAD_V] = out_b.astype(BF16)

    mixed = _dot(mix_ref[...], wout_ref[...])
    out_ref[0] = x + mixed * _rms_scale(mixed) * npost_ref[...]


def _const_spec(shape):
    return pl.BlockSpec(shape, lambda b, j: (0,) * len(shape))


def _hybrid_layer(x, wts):
    bsz, seq, _ = x.shape
    n_tiles = seq // TILE
    grid = (bsz, n_tiles)
    params = pltpu.CompilerParams(dimension_semantics=("arbitrary", "arbitrary"),
                                  vmem_limit_bytes=VMEM_LIMIT)

    hb = pl.pallas_call(
        _bwd_state_kernel,
        grid=grid,
        in_specs=[
            pl.BlockSpec((1, TILE, D_MODEL), lambda b, j: (b, n_tiles - 1 - j, 0)),
            _const_spec((1, D_MODEL)),
            _const_spec((D_MODEL, D_K + D_B)),
            _const_spec((D_MODEL, LANES)),
            _const_spec((LANES, D_K)),
            _const_spec((1, D_K)),
            _const_spec((TILE, TILE)),
        ],
        out_specs=pl.BlockSpec((1, 1, N_HEADS_B * HEAD_K, HEAD_V),
                               lambda b, j: (b, n_tiles - 1 - j, 0, 0)),
        out_shape=jax.ShapeDtypeStruct((bsz, n_tiles, N_HEADS_B * HEAD_K, HEAD_V), F32),
        scratch_shapes=[pltpu.VMEM((N_HEADS_B * HEAD_K, HEAD_V), F32)],
        compiler_params=params,
    )(x, wts["npre"], wts["wkv"], wts["wlr_b"], wts["wgk_b"], wts["bgk_b"], wts["utri"])

    return pl.pallas_call(
        _layer_kernel,
        grid=grid,
        in_specs=[
            pl.BlockSpec((1, TILE, D_MODEL), lambda b, j: (b, j, 0)),
            pl.BlockSpec((1, 1, N_HEADS_B * HEAD_K, HEAD_V), lambda b, j: (b, j, 0, 0)),
            _const_spec((1, D_MODEL)),
            _const_spec((D_MODEL, OFF_LRF)),
            _const_spec((D_MODEL, LANES)),
            _const_spec((N_HEADS_A, CHUNK_A, CHUNK_A)),
            _const_spec((N_HEADS_A, CHUNK_A, HEAD_A)),
            _const_spec((1, D_A)),
            _const_spec((LANES, 2 * D_K)),
            _const_spec((1, 2 * D_K)),
            _const_spec((TILE, TILE)),
            _const_spec((TILE, TILE)),
            _const_spec((1, HEAD_V)),
            _const_spec((D_A + D_B, D_MODEL)),
            _const_spec((1, D_MODEL)),
        ],
        out_specs=pl.BlockSpec((1, TILE, D_MODEL), lambda b, j: (b, j, 0)),
        out_shape=jax.ShapeDtypeStruct(x.shape, x.dtype),
        scratch_shapes=[
            pltpu.VMEM((N_HEADS_B * HEAD_K, HEAD_V), F32),
            pltpu.VMEM((N_HEADS_B * HEAD_K, HEAD_V), F32),
            pltpu.VMEM((TILE, D_B), F32),
            pltpu.VMEM((TILE, D_A + D_B), BF16),
        ],
        compiler_params=params,
    )(x, hb, wts["npre"], wts["wmain"], wts["wlr"], wts["wsp"], wts["bsp"], wts["gva"],
      wts["wgk"], wts["bgk"], wts["ltri"], wts["utri"], wts["gnb"], wts["wout"], wts["npost"])


def _prepare_weights(norm_pre, w_in, w_sp, b_sp, g_v_a, w_gk_fwd, b_gk_fwd, w_gk_bwd, b_gk_bwd,
                     g_norm_b, w_out, norm_post):
    w_in_bf = w_in.astype(BF16)
    wlr = jnp.zeros((D_MODEL, LANES), BF16).at[:, 0:2 * GATE_RANK].set(w_in_bf[:, OFF_LRF:D_IN])
    wlr_b = jnp.zeros((D_MODEL, LANES), BF16).at[:, 0:GATE_RANK].set(w_in_bf[:, OFF_LRB:D_IN])
    wgk = jnp.zeros((LANES, 2 * D_K), BF16)
    wgk = wgk.at[0:GATE_RANK, 0:D_K].set(w_gk_fwd.astype(BF16))
    wgk = wgk.at[GATE_RANK:2 * GATE_RANK, D_K:2 * D_K].set(w_gk_bwd.astype(BF16))
    wgk_b = jnp.zeros((LANES, D_K), BF16).at[0:GATE_RANK, :].set(w_gk_bwd.astype(BF16))
    pos = jnp.arange(TILE)
    same_chunk = (pos[:, None] // CHUNK_B) == (pos[None, :] // CHUNK_B)
    ltri = (same_chunk & (pos[None, :] <= pos[:, None])).astype(BF16)
    utri = (same_chunk & (pos[None, :] >= pos[:, None])).astype(BF16)
    return {
        "npre": norm_pre.reshape(1, D_MODEL),
        "wmain": w_in_bf[:, 0:OFF_LRF],
        "wkv": w_in_bf[:, OFF_K:OFF_ZB],
        "wlr": wlr,
        "wlr_b": wlr_b,
        "wsp": w_sp.astype(BF16),
        "bsp": jnp.broadcast_to(b_sp[:, :, None], (N_HEADS_A, CHUNK_A, HEAD_A)),
        "gva": g_v_a.reshape(1, D_A),
        "wgk": wgk,
        "wgk_b": wgk_b,
        "bgk": jnp.concatenate([b_gk_fwd, b_gk_bwd]).reshape(1, 2 * D_K),
        "bgk_b": b_gk_bwd.reshape(1, D_K),
        "ltri": ltri,
        "utri": utri,
        "gnb": g_norm_b.reshape(1, HEAD_V),
        "wout": w_out.astype(BF16),
        "npost": norm_post.reshape(1, D_MODEL),
    }


def kernel(x_prompt, x_sample, norm_pre, w_in, w_sp, b_sp, g_v_a, w_gk_fwd, b_gk_fwd,
           w_gk_bwd, b_gk_bwd, g_norm_b, w_out, norm_post):
    y_prompt, y_sample = x_prompt, x_sample
    for l in range(norm_pre.shape[0]):
        wts = _prepare_weights(norm_pre[l], w_in[l], w_sp[l], b_sp[l], g_v_a[l], w_gk_fwd[l],
                               b_gk_fwd[l], w_gk_bwd[l], b_gk_bwd[l], g_norm_b[l], w_out[l],
                               norm_post[l])
        y_prompt = _hybrid_layer(y_prompt, wts)
        y_sample = _hybrid_layer(y_sample, wts)
    return (y_prompt, y_sample)
```

```python
import functools

import jax
import jax.numpy as jnp
from jax import lax
from jax.experimental import pallas as pl
from jax.experimental.pallas import tpu as pltpu

F32 = jnp.float32
BF16 = jnp.bfloat16

D_MODEL = 1024
D_A = 512
N_HEADS_A = 4
HEAD_A = 128
CHUNK_A = 128
D_B = 512
N_HEADS_B = 4
HEAD_V = 128
D_K = 256
HEAD_K = 64
GATE_RANK = 16
GATE_NORMALIZER = 16.0
CHUNK_B = 64
EPS = 1e-6

OFF_U, OFF_V, OFF_ZA, OFF_Q, OFF_K, OFF_VB, OFF_ZB, OFF_LRF, OFF_LRB, D_IN = (
    0, 512, 1024, 1536, 1792, 2048, 2560, 3072, 3088, 3104)

LANES = 128
TILE = 256
N_CHUNK = TILE // CHUNK_B
N_PAIR = N_HEADS_B // 2
PIECE_COLS = 256
P_COLS = OFF_LRF + PIECE_COLS
VMEM_LIMIT = 56 * 1024 * 1024


def _dot(a, b):
    return jnp.dot(a, b, preferred_element_type=F32)


def _dot_nt(a, b):
    return lax.dot_general(a, b, (((1,), (1,)), ((), ())), preferred_element_type=F32)


def _gelu(x):
    return 0.5 * x * (1.0 + jnp.tanh(0.7978845608028654 * (x + 0.044715 * (x * x * x))))


def _silu(x):
    return x * (0.5 * jnp.tanh(0.5 * x) + 0.5)


def _log_sigmoid(x):
    return jnp.minimum(x, 0.0) - jnp.log1p(jnp.exp(-jnp.abs(x)))


def _rms_scale(x):
    return lax.rsqrt(jnp.mean(x * x, axis=-1, keepdims=True) + EPS)


def _chunk_cumsum(tri_ref, g):
    hi = g.astype(BF16)
    lo = (g - hi.astype(F32)).astype(BF16)
    tri = tri_ref[...]
    return _dot(tri, hi) + _dot(tri, lo)


def _decayed_keys(k, bcum, backward):
    kd, dec = [], []
    for c in range(N_CHUNK):
        r0 = c * CHUNK_B
        last = r0 if backward else r0 + CHUNK_B - 1
        b_last = bcum[last:last + 1, :]
        kd.append(k[r0:r0 + CHUNK_B, :] * jnp.exp(b_last - bcum[r0:r0 + CHUNK_B, :]))
        dec.append(jnp.exp(b_last))
    return jnp.concatenate(kd, axis=0), dec


def _chunk_kv(kt, v_par, c):
    cp, par = divmod(c, 2)
    parts = []
    for p in range(N_PAIR):
        full = _dot(kt[p * LANES:(p + 1) * LANES, cp * LANES:(cp + 1) * LANES],
                    v_par[cp][par][:, p * 2 * HEAD_V:(p + 1) * 2 * HEAD_V])
        parts.append(full[0:HEAD_K, 0:HEAD_V])
        parts.append(full[HEAD_K:2 * HEAD_K, HEAD_V:2 * HEAD_V])
    return jnp.concatenate(parts, axis=0)


def _decay_column(dec_row):
    return jnp.transpose(jnp.broadcast_to(dec_row, (LANES, D_K)))


def _parity_masked_values(vb):
    out = []
    row = lax.broadcasted_iota(jnp.int32, (2 * CHUNK_B, D_B), 0)
    for cp in range(N_CHUNK // 2):
        vp = vb[cp * 2 * CHUNK_B:(cp + 1) * 2 * CHUNK_B, :]
        zero = jnp.zeros_like(vp)
        out.append((jnp.where(row < CHUNK_B, vp, zero), jnp.where(row >= CHUNK_B, vp, zero)))
    return out


def _bwd_state_kernel(x_ref, npre_ref, wkv_ref, wlr_ref, wgk_ref, bgk_ref, utri_ref,
                      hb_ref, s_ref):
    j = pl.program_id(1)

    @pl.when(j == 0)
    def _():
        s_ref[...] = jnp.zeros_like(s_ref)

    hb_ref[0, 0] = s_ref[...]

    x = x_ref[0]
    h = (x * _rms_scale(x) * npre_ref[...]).astype(BF16)
    pkv = _dot(h, wkv_ref[...])
    k = pkv[:, 0:D_K]
    vb = pkv[:, D_K:D_K + D_B].astype(BF16)
    plr = _dot(h, wlr_ref[...])
    g = _log_sigmoid(_dot(plr.astype(BF16), wgk_ref[...]) + bgk_ref[...]) * (1.0 / GATE_NORMALIZER)
    bcum = _chunk_cumsum(utri_ref, g)
    kdec, dec = _decayed_keys(k, bcum, backward=True)
    kt = jnp.transpose(kdec).astype(BF16)
    v_par = _parity_masked_values(vb)
    for c in reversed(range(N_CHUNK)):
        s_ref[...] = _decay_column(dec[c]) * s_ref[...] + _chunk_kv(kt, v_par, c)


def _projection_pieces(x_ref, npre_ref, win_ref, p_ref):
    x = x_ref[0]
    h = (x * _rms_scale(x) * npre_ref[...]).astype(BF16)

    def piece(c0):
        def run():
            p_ref[:, c0:c0 + PIECE_COLS] = _dot(h, win_ref[:, c0:c0 + PIECE_COLS])
        return run

    return [piece(c0) for c0 in range(0, P_COLS, PIECE_COLS)]


def _mix(p_ref, x_ref, hb_ref, wsp_ref, bsp_ref, gva_ref, wgk_ref, bgk_ref, ltri_ref, utri_ref,
         gnb_ref, wout_ref, npost_ref, out_ref, sf_ref, sb_ref, o_ref, mix_ref, emit):
    sb_ref[...] = hb_ref[0]

    emit(2)
    vg = _gelu(p_ref[:, OFF_V:OFF_ZA])
    vc = vg - jnp.mean(vg, axis=-1, keepdims=True)
    vn = (vc * _rms_scale(vc) * gva_ref[...]).astype(BF16)
    for c in range(TILE // CHUNK_A):
        r0 = c * CHUNK_A
        for hd in range(N_HEADS_A):
            c0 = hd * HEAD_A
            sv = _dot(wsp_ref[hd], vn[r0:r0 + CHUNK_A, c0:c0 + HEAD_A]) + bsp_ref[hd]
            out_a = _gelu(p_ref[r0:r0 + CHUNK_A, OFF_U + c0:OFF_U + c0 + HEAD_A]) * sv
            out_a = out_a * _silu(p_ref[r0:r0 + CHUNK_A, OFF_ZA + c0:OFF_ZA + c0 + HEAD_A])
            mix_ref[r0:r0 + CHUNK_A, c0:c0 + HEAD_A] = out_a.astype(BF16)
            emit(hd % 2)

    q = p_ref[:, OFF_Q:OFF_K] * (HEAD_K ** -0.5)
    k = p_ref[:, OFF_K:OFF_VB]
    vb = p_ref[:, OFF_VB:OFF_ZB].astype(BF16)
    plr = p_ref[:, OFF_LRF:OFF_LRF + LANES]
    emit(1)
    g = _log_sigmoid(_dot(plr.astype(BF16), wgk_ref[...]) + bgk_ref[...]) * (1.0 / GATE_NORMALIZER)
    v_par = _parity_masked_values(vb)

    lane_head = lax.broadcasted_iota(jnp.int32, (CHUNK_B, D_K), 1) // HEAD_K
    lane_pos = lax.broadcasted_iota(jnp.int32, (CHUNK_B, D_K), 1) % CHUNK_B
    row_pos = lax.broadcasted_iota(jnp.int32, (CHUNK_B, D_K), 0)
    zero_blk = jnp.zeros((CHUNK_B, HEAD_V), BF16)

    dirs = []
    for d in range(2):
        backward = d == 1
        emit(1)
        bcum = _chunk_cumsum(utri_ref if backward else ltri_ref, g[:, d * D_K:(d + 1) * D_K])
        q_in = (q * jnp.exp(bcum)).astype(BF16)
        k_in = (k * jnp.exp(-bcum)).astype(BF16)
        kdec, dec = _decayed_keys(k, bcum, backward)
        kt = jnp.transpose(kdec).astype(BF16)
        keep = (row_pos < lane_pos) if backward else (row_pos >= lane_pos)
        dirs.append((sb_ref if backward else sf_ref, q_in, k_in, kt, dec, keep))

    steps = [(t, d, N_CHUNK - 1 - t if d == 1 else t) for t in range(N_CHUNK) for d in range(2)]
    p_blk, kv_blk, dec_col = {}, {}, {}
    for t, d, c in steps:
        _, q_in, k_in, _, _, keep = dirs[d]
        k_c = k_in[c * CHUNK_B:(c + 1) * CHUNK_B, :]
        k_bd = jnp.concatenate(
            [jnp.where(lane_head == hd, k_c, jnp.zeros_like(k_c)) for hd in range(N_HEADS_B)],
            axis=0)
        scores = _dot_nt(q_in[c * CHUNK_B:(c + 1) * CHUNK_B, :], k_bd)
        p_blk[d, c] = jnp.where(keep, scores, 0.0).astype(BF16)
    for t, d, c in steps:
        _, _, _, kt, dec, _ = dirs[d]
        kv_blk[d, c] = _chunk_kv(kt, v_par, c)
        dec_col[d, c] = _decay_column(dec[c])
        emit(d * (t % 2))

    for t, d, c in steps:
        s_ref, q_in = dirs[d][0], dirs[d][1]
        r0 = c * CHUNK_B
        s_old = s_ref[...]
        s_bf = s_old.astype(BF16)
        for p in range(N_PAIR):
            a0 = 2 * p * HEAD_V
            rhs = jnp.concatenate([
                jnp.concatenate([vb[r0:r0 + CHUNK_B, a0:a0 + HEAD_V], zero_blk], axis=1),
                jnp.concatenate([zero_blk, vb[r0:r0 + CHUNK_B, a0 + HEAD_V:a0 + 2 * HEAD_V]], axis=1),
                jnp.concatenate([s_bf[2 * p * HEAD_K:(2 * p + 1) * HEAD_K, :], zero_blk], axis=1),
                jnp.concatenate([zero_blk, s_bf[(2 * p + 1) * HEAD_K:(2 * p + 2) * HEAD_K, :]], axis=1),
            ], axis=0)
            lhs = jnp.concatenate([p_blk[d, c][:, p * LANES:(p + 1) * LANES],
                                   q_in[r0:r0 + CHUNK_B, p * LANES:(p + 1) * LANES]], axis=1)
            o_pair = _dot(lhs, rhs)
            if t < N_CHUNK // 2:
                o_ref[r0:r0 + CHUNK_B, a0:a0 + 2 * HEAD_V] = o_pair
            else:
                o_ref[r0:r0 + CHUNK_B, a0:a0 + 2 * HEAD_V] += o_pair
        s_ref[...] = dec_col[d, c] * s_old + kv_blk[d, c]

    emit(1)
    for hd in range(N_HEADS_B):
        c0 = hd * HEAD_V
        oh = o_ref[:, c0:c0 + HEAD_V]
        out_b = oh * _rms_scale(oh) * gnb_ref[...] * _silu(p_ref[:, OFF_ZB + c0:OFF_ZB + c0 + HEAD_V])
        mix_ref[:, D_A + c0:D_A + c0 + HEAD_V] = out_b.astype(BF16)

    mixed = jnp.concatenate(
        [_dot(mix_ref[...], wout_ref[:, c0:c0 + PIECE_COLS]) for c0 in range(0, D_MODEL, PIECE_COLS)],
        axis=1)
    emit(P_COLS // PIECE_COLS)
    out_ref[0] = x_ref[0] + mixed * _rms_scale(mixed) * npost_ref[...]


def _layer_kernel(n_tiles, xn_ref, xc_ref, hb_ref, npre_ref, win_ref, wsp_ref, bsp_ref, gva_ref,
                  wgk_ref, bgk_ref, ltri_ref, utri_ref, gnb_ref, wout_ref, npost_ref,
                  out_ref, p0_ref, p1_ref, sf_ref, sb_ref, o_ref, mix_ref):
    i = pl.program_id(0)

    @pl.when(i == 0)
    def _():
        p1_ref[...] = jnp.zeros_like(p1_ref)

    @pl.when(jnp.logical_or(i == 0, (i - 1) % n_tiles == 0))
    def _():
        sf_ref[...] = jnp.zeros_like(sf_ref)

    def step(p_write_ref, p_read_ref):
        pieces = _projection_pieces(xn_ref, npre_ref, win_ref, p_write_ref)

        def emit(n):
            for _ in range(min(n, len(pieces))):
                pieces.pop(0)()

        _mix(p_read_ref, xc_ref, hb_ref, wsp_ref, bsp_ref, gva_ref, wgk_ref, bgk_ref, ltri_ref,
             utri_ref, gnb_ref, wout_ref, npost_ref, out_ref, sf_ref, sb_ref, o_ref, mix_ref, emit)

    @pl.when(i % 2 == 0)
    def _():
        step(p0_ref, p1_ref)

    @pl.when(i % 2 == 1)
    def _():
        step(p1_ref, p0_ref)


def _const_spec(shape):
    return pl.BlockSpec(shape, lambda *_: (0,) * len(shape))


def _hybrid_layer(x, wts):
    bsz, seq, _ = x.shape
    n_tiles = seq // TILE
    n_total = bsz * n_tiles

    hb = pl.pallas_call(
        _bwd_state_kernel,
        grid=(bsz, n_tiles),
        in_specs=[
            pl.BlockSpec((1, TILE, D_MODEL), lambda b, j: (b, n_tiles - 1 - j, 0)),
            _const_spec((1, D_MODEL)),
            _const_spec((D_MODEL, D_K + D_B)),
            _const_spec((D_MODEL, LANES)),
            _const_spec((LANES, D_K)),
            _const_spec((1, D_K)),
            _const_spec((TILE, TILE)),
        ],
        out_specs=pl.BlockSpec((1, 1, N_HEADS_B * HEAD_K, HEAD_V),
                               lambda b, j: (b, n_tiles - 1 - j, 0, 0)),
        out_shape=jax.ShapeDtypeStruct((bsz, n_tiles, N_HEADS_B * HEAD_K, HEAD_V), F32),
        scratch_shapes=[pltpu.VMEM((N_HEADS_B * HEAD_K, HEAD_V), F32)],
        compiler_params=pltpu.CompilerParams(dimension_semantics=("arbitrary", "arbitrary"),
                                             vmem_limit_bytes=VMEM_LIMIT),
    )(x, wts["npre"], wts["wkv"], wts["wlr_b"], wts["wgk_b"], wts["bgk_b"], wts["utri"])

    y = pl.pallas_call(
        functools.partial(_layer_kernel, n_tiles),
        grid=(n_total + 1,),
        in_specs=[
            pl.BlockSpec((1, TILE, D_MODEL), lambda i: (jnp.minimum(i, n_total - 1), 0, 0)),
            pl.BlockSpec((1, TILE, D_MODEL), lambda i: (jnp.maximum(i - 1, 0), 0, 0)),
            pl.BlockSpec((1, N_HEADS_B * HEAD_K, HEAD_V),
                         lambda i: (jnp.maximum(i - 1, 0), 0, 0)),
            _const_spec((1, D_MODEL)),
            _const_spec((D_MODEL, P_COLS)),
            _const_spec((N_HEADS_A, CHUNK_A, CHUNK_A)),
            _const_spec((N_HEADS_A, CHUNK_A, HEAD_A)),
            _const_spec((1, D_A)),
            _const_spec((LANES, 2 * D_K)),
            _const_spec((1, 2 * D_K)),
            _const_spec((TILE, TILE)),
            _const_spec((TILE, TILE)),
            _const_spec((1, HEAD_V)),
            _const_spec((D_A + D_B, D_MODEL)),
            _const_spec((1, D_MODEL)),
        ],
        out_specs=pl.BlockSpec((1, TILE, D_MODEL), lambda i: (jnp.maximum(i - 1, 0), 0, 0)),
        out_shape=jax.ShapeDtypeStruct((n_total, TILE, D_MODEL), x.dtype),
        scratch_shapes=[
            pltpu.VMEM((TILE, P_COLS), F32),
            pltpu.VMEM((TILE, P_COLS), F32),
            pltpu.VMEM((N_HEADS_B * HEAD_K, HEAD_V), F32),
            pltpu.VMEM((N_HEADS_B * HEAD_K, HEAD_V), F32),
            pltpu.VMEM((TILE, D_B), F32),
            pltpu.VMEM((TILE, D_A + D_B), BF16),
        ],
        compiler_params=pltpu.CompilerParams(dimension_semantics=("arbitrary",),
                                             vmem_limit_bytes=VMEM_LIMIT),
    )(x.reshape(n_total, TILE, D_MODEL), x.reshape(n_total, TILE, D_MODEL),
      hb.reshape(n_total, N_HEADS_B * HEAD_K, HEAD_V), wts["npre"], wts["win"], wts["wsp"],
      wts["bsp"], wts["gva"], wts["wgk"], wts["bgk"], wts["ltri"], wts["utri"], wts["gnb"],
      wts["wout"], wts["npost"])
    return y.reshape(x.shape)


def _prepare_weights(norm_pre, w_in, w_sp, b_sp, g_v_a, w_gk_fwd, b_gk_fwd, w_gk_bwd, b_gk_bwd,
                     g_norm_b, w_out, norm_post):
    w_in_bf = w_in.astype(BF16)
    wlr = jnp.zeros((D_MODEL, PIECE_COLS), BF16).at[:, 0:2 * GATE_RANK].set(w_in_bf[:, OFF_LRF:D_IN])
    wlr_b = jnp.zeros((D_MODEL, LANES), BF16).at[:, 0:GATE_RANK].set(w_in_bf[:, OFF_LRB:D_IN])
    wgk = jnp.zeros((LANES, 2 * D_K), BF16)
    wgk = wgk.at[0:GATE_RANK, 0:D_K].set(w_gk_fwd.astype(BF16))
    wgk = wgk.at[GATE_RANK:2 * GATE_RANK, D_K:2 * D_K].set(w_gk_bwd.astype(BF16))
    wgk_b = jnp.zeros((LANES, D_K), BF16).at[0:GATE_RANK, :].set(w_gk_bwd.astype(BF16))
    pos = jnp.arange(TILE)
    same_chunk = (pos[:, None] // CHUNK_B) == (pos[None, :] // CHUNK_B)
    ltri = (same_chunk & (pos[None, :] <= pos[:, None])).astype(BF16)
    utri = (same_chunk & (pos[None, :] >= pos[:, None])).astype(BF16)
    return {
        "npre": norm_pre.reshape(1, D_MODEL),
        "win": jnp.concatenate([w_in_bf[:, 0:OFF_LRF], wlr], axis=1),
        "wkv": w_in_bf[:, OFF_K:OFF_ZB],
        "wlr_b": wlr_b,
        "wsp": w_sp.astype(BF16),
        "bsp": jnp.broadcast_to(b_sp[:, :, None], (N_HEADS_A, CHUNK_A, HEAD_A)),
        "gva": g_v_a.reshape(1, D_A),
        "wgk": wgk,
        "wgk_b": wgk_b,
        "bgk": jnp.concatenate([b_gk_fwd, b_gk_bwd]).reshape(1, 2 * D_K),
        "bgk_b": b_gk_bwd.reshape(1, D_K),
        "ltri": ltri,
        "utri": utri,
        "gnb": g_norm_b.reshape(1, HEAD_V),
        "wout": w_out.astype(BF16),
        "npost": norm_post.reshape(1, D_MODEL),
    }


def kernel(x_prompt, x_sample, norm_pre, w_in, w_sp, b_sp, g_v_a, w_gk_fwd, b_gk_fwd,
           w_gk_bwd, b_gk_bwd, g_norm_b, w_out, norm_post):
    y_prompt, y_sample = x_prompt, x_sample
    for l in range(norm_pre.shape[0]):
        wts = _prepare_weights(norm_pre[l], w_in[l], w_sp[l], b_sp[l], g_v_a[l], w_gk_fwd[l],
                               b_gk_fwd[l], w_gk_bwd[l], b_gk_bwd[l], g_norm_b[l], w_out[l],
                               norm_post[l])
        y_prompt = _hybrid_layer(y_prompt, wts)
        y_sample = _hybrid_layer(y_sample, wts)
    return (y_prompt, y_sample)
```

```python
import functools

import jax
import jax.numpy as jnp
from jax import lax
from jax.experimental import pallas as pl
from jax.experimental.pallas import tpu as pltpu

F32 = jnp.float32
BF16 = jnp.bfloat16

D_MODEL = 1024
D_A = 512
N_HEADS_A = 4
HEAD_A = 128
CHUNK_A = 128
D_B = 512
N_HEADS_B = 4
HEAD_V = 128
D_K = 256
HEAD_K = 64
GATE_RANK = 16
GATE_NORMALIZER = 16.0
CHUNK_B = 64
EPS = 1e-6

OFF_U, OFF_V, OFF_ZA, OFF_Q, OFF_K, OFF_VB, OFF_ZB, OFF_LRF, OFF_LRB, D_IN = (
    0, 512, 1024, 1536, 1792, 2048, 2560, 3072, 3088, 3104)

LANES = 128
TILE = 256
N_CHUNK = TILE // CHUNK_B
N_PAIR = N_HEADS_B // 2
PIECE_COLS = 256
P_COLS = OFF_LRF + PIECE_COLS
PRE_COLS = D_K + D_B + PIECE_COLS
VMEM_LIMIT = 56 * 1024 * 1024


def _dot(a, b):
    return jnp.dot(a, b, preferred_element_type=F32)


def _dot_nt(a, b):
    return lax.dot_general(a, b, (((1,), (1,)), ((), ())), preferred_element_type=F32)


def _gelu(x):
    return 0.5 * x * (1.0 + jnp.tanh(0.7978845608028654 * (x + 0.044715 * (x * x * x))))


def _silu(x):
    return x * (0.5 * jnp.tanh(0.5 * x) + 0.5)


def _log_sigmoid(x):
    return jnp.minimum(x, 0.0) - jnp.log1p(jnp.exp(-jnp.abs(x)))


def _rms_scale(x):
    return lax.rsqrt(jnp.mean(x * x, axis=-1, keepdims=True) + EPS)


def _chunk_cumsum(tri_ref, g):
    hi = g.astype(BF16)
    lo = (g - hi.astype(F32)).astype(BF16)
    tri = tri_ref[...]
    return _dot(tri, hi) + _dot(tri, lo)


def _decayed_keys(k, bcum, backward):
    kd, dec = [], []
    for c in range(N_CHUNK):
        r0 = c * CHUNK_B
        last = r0 if backward else r0 + CHUNK_B - 1
        b_last = bcum[last:last + 1, :]
        kd.append(k[r0:r0 + CHUNK_B, :] * jnp.exp(b_last - bcum[r0:r0 + CHUNK_B, :]))
        dec.append(jnp.exp(b_last))
    return jnp.concatenate(kd, axis=0), dec


def _chunk_kv(kt, v_par, c):
    cp, par = divmod(c, 2)
    parts = []
    for p in range(N_PAIR):
        full = _dot(kt[p * LANES:(p + 1) * LANES, cp * LANES:(cp + 1) * LANES],
                    v_par[cp][par][:, p * 2 * HEAD_V:(p + 1) * 2 * HEAD_V])
        parts.append(full[0:HEAD_K, 0:HEAD_V])
        parts.append(full[HEAD_K:2 * HEAD_K, HEAD_V:2 * HEAD_V])
    return jnp.concatenate(parts, axis=0)


def _decay_column(dec_row):
    return jnp.transpose(jnp.broadcast_to(dec_row, (LANES, D_K)))


def _parity_masked_values(vb):
    out = []
    row = lax.broadcasted_iota(jnp.int32, (2 * CHUNK_B, D_B), 0)
    for cp in range(N_CHUNK // 2):
        vp = vb[cp * 2 * CHUNK_B:(cp + 1) * 2 * CHUNK_B, :]
        zero = jnp.zeros_like(vp)
        out.append((jnp.where(row < CHUNK_B, vp, zero), jnp.where(row >= CHUNK_B, vp, zero)))
    return out


def _projection_pieces(x_ref, npre_ref, w_ref, p_ref):
    x = x_ref[0]
    h = (x * _rms_scale(x) * npre_ref[...]).astype(BF16)

    def piece(c0):
        def run():
            p_ref[:, c0:c0 + PIECE_COLS] = _dot(h, w_ref[:, c0:c0 + PIECE_COLS])
        return run

    return [piece(c0) for c0 in range(0, p_ref.shape[1], PIECE_COLS)]


def _emitter(pieces):
    def emit(n):
        for _ in range(min(n, len(pieces))):
            pieces.pop(0)()
    return emit


def _bwd_state_kernel(n_tiles, n_total, xn_ref, npre_ref, wkv_ref, wgk_ref, bgk_ref, utri_ref,
                      hb_ref, p0_ref, p1_ref, s_ref):
    i = pl.program_id(0)

    @pl.when(i == 0)
    def _():
        p1_ref[...] = jnp.zeros_like(p1_ref)

    @pl.when(jnp.logical_or(i == 0, (n_total - i) % n_tiles == n_tiles - 1))
    def _():
        s_ref[...] = jnp.zeros_like(s_ref)

    def step(p_write_ref, p_ref):
        emit = _emitter(_projection_pieces(xn_ref, npre_ref, wkv_ref, p_write_ref))
        hb_ref[0] = s_ref[...]
        emit(1)
        plr = p_ref[:, D_K + D_B:D_K + D_B + LANES]
        g = _log_sigmoid(_dot(plr.astype(BF16), wgk_ref[...]) + bgk_ref[...]) * (
            1.0 / GATE_NORMALIZER)
        emit(1)
        bcum = _chunk_cumsum(utri_ref, g)
        kdec, dec = _decayed_keys(p_ref[:, 0:D_K], bcum, backward=True)
        kt = jnp.transpose(kdec).astype(BF16)
        emit(1)
        v_par = _parity_masked_values(p_ref[:, D_K:D_K + D_B].astype(BF16))
        kv = [_chunk_kv(kt, v_par, c) for c in range(N_CHUNK)]
        dec_col = [_decay_column(dec[c]) for c in range(N_CHUNK)]
        emit(1)
        s = s_ref[...]
        for c in reversed(range(N_CHUNK)):
            s = dec_col[c] * s + kv[c]
        s_ref[...] = s

    @pl.when(i % 2 == 0)
    def _():
        step(p0_ref, p1_ref)

    @pl.when(i % 2 == 1)
    def _():
        step(p1_ref, p0_ref)


def _mix(p_ref, x_ref, hb_ref, wsp_ref, bsp_ref, gva_ref, wgk_ref, bgk_ref, ltri_ref, utri_ref,
         gnb_ref, wout_ref, npost_ref, out_ref, sf_ref, sb_ref, o_ref, mix_ref, emit):
    sb_ref[...] = hb_ref[0]

    emit(2)
    vg = _gelu(p_ref[:, OFF_V:OFF_ZA])
    vc = vg - jnp.mean(vg, axis=-1, keepdims=True)
    vn = (vc * _rms_scale(vc) * gva_ref[...]).astype(BF16)
    for c in range(TILE // CHUNK_A):
        r0 = c * CHUNK_A
        for hd in range(N_HEADS_A):
            c0 = hd * HEAD_A
            sv = _dot(wsp_ref[hd], vn[r0:r0 + CHUNK_A, c0:c0 + HEAD_A]) + bsp_ref[hd]
            out_a = _gelu(p_ref[r0:r0 + CHUNK_A, OFF_U + c0:OFF_U + c0 + HEAD_A]) * sv
            out_a = out_a * _silu(p_ref[r0:r0 + CHUNK_A, OFF_ZA + c0:OFF_ZA + c0 + HEAD_A])
            mix_ref[r0:r0 + CHUNK_A, c0:c0 + HEAD_A] = out_a.astype(BF16)
            emit(hd % 2)

    q = p_ref[:, OFF_Q:OFF_K] * (HEAD_K ** -0.5)
    k = p_ref[:, OFF_K:OFF_VB]
    vb = p_ref[:, OFF_VB:OFF_ZB].astype(BF16)
    plr = p_ref[:, OFF_LRF:OFF_LRF + LANES]
    emit(1)
    g = _log_sigmoid(_dot(plr.astype(BF16), wgk_ref[...]) + bgk_ref[...]) * (1.0 / GATE_NORMALIZER)
    v_par = _parity_masked_values(vb)

    lane_head = lax.broadcasted_iota(jnp.int32, (CHUNK_B, D_K), 1) // HEAD_K
    lane_pos = lax.broadcasted_iota(jnp.int32, (CHUNK_B, D_K), 1) % CHUNK_B
    row_pos = lax.broadcasted_iota(jnp.int32, (CHUNK_B, D_K), 0)
    zero_blk = jnp.zeros((CHUNK_B, HEAD_V), BF16)

    dirs = []
    for d in range(2):
        backward = d == 1
        emit(1)
        bcum = _chunk_cumsum(utri_ref if backward else ltri_ref, g[:, d * D_K:(d + 1) * D_K])
        q_in = (q * jnp.exp(bcum)).astype(BF16)
        k_in = (k * jnp.exp(-bcum)).astype(BF16)
        kdec, dec = _decayed_keys(k, bcum, backward)
        kt = jnp.transpose(kdec).astype(BF16)
        keep = (row_pos < lane_pos) if backward else (row_pos >= lane_pos)
        dirs.append((sb_ref if backward else sf_ref, q_in, k_in, kt, dec, keep))

    steps = [(t, d, N_CHUNK - 1 - t if d == 1 else t) for t in range(N_CHUNK) for d in range(2)]
    p_blk, kv_blk, dec_col = {}, {}, {}
    for t, d, c in steps:
        _, q_in, k_in, _, _, keep = dirs[d]
        k_c = k_in[c * CHUNK_B:(c + 1) * CHUNK_B, :]
        k_bd = jnp.concatenate(
            [jnp.where(lane_head == hd, k_c, jnp.zeros_like(k_c)) for hd in range(N_HEADS_B)],
            axis=0)
        scores = _dot_nt(q_in[c * CHUNK_B:(c + 1) * CHUNK_B, :], k_bd)
        p_blk[d, c] = jnp.where(keep, scores, 0.0).astype(BF16)
    for t, d, c in steps:
        _, _, _, kt, dec, _ = dirs[d]
        kv_blk[d, c] = _chunk_kv(kt, v_par, c)
        dec_col[d, c] = _decay_column(dec[c])
        emit(d * (t % 2))

    for t, d, c in steps:
        s_ref, q_in = dirs[d][0], dirs[d][1]
        r0 = c * CHUNK_B
        s_old = s_ref[...]
        s_bf = s_old.astype(BF16)
        for p in range(N_PAIR):
            a0 = 2 * p * HEAD_V
            rhs = jnp.concatenate([
                jnp.concatenate([vb[r0:r0 + CHUNK_B, a0:a0 + HEAD_V], zero_blk], axis=1),
                jnp.concatenate([zero_blk, vb[r0:r0 + CHUNK_B, a0 + HEAD_V:a0 + 2 * HEAD_V]], axis=1),
                jnp.concatenate([s_bf[2 * p * HEAD_K:(2 * p + 1) * HEAD_K, :], zero_blk], axis=1),
                jnp.concatenate([zero_blk, s_bf[(2 * p + 1) * HEAD_K:(2 * p + 2) * HEAD_K, :]], axis=1),
            ], axis=0)
            lhs = jnp.concatenate([p_blk[d, c][:, p * LANES:(p + 1) * LANES],
                                   q_in[r0:r0 + CHUNK_B, p * LANES:(p + 1) * LANES]], axis=1)
            o_pair = _dot(lhs, rhs)
            if t < N_CHUNK // 2:
                o_ref[r0:r0 + CHUNK_B, a0:a0 + 2 * HEAD_V] = o_pair
            else:
                o_ref[r0:r0 + CHUNK_B, a0:a0 + 2 * HEAD_V] += o_pair
        s_ref[...] = dec_col[d, c] * s_old + kv_blk[d, c]

    emit(1)
    for hd in range(N_HEADS_B):
        c0 = hd * HEAD_V
        oh = o_ref[:, c0:c0 + HEAD_V]
        out_b = oh * _rms_scale(oh) * gnb_ref[...] * _silu(p_ref[:, OFF_ZB + c0:OFF_ZB + c0 + HEAD_V])
        mix_ref[:, D_A + c0:D_A + c0 + HEAD_V] = out_b.astype(BF16)

    mixed = jnp.concatenate(
        [_dot(mix_ref[...], wout_ref[:, c0:c0 + PIECE_COLS]) for c0 in range(0, D_MODEL, PIECE_COLS)],
        axis=1)
    emit(P_COLS // PIECE_COLS)
    out_ref[0] = x_ref[0] + mixed * _rms_scale(mixed) * npost_ref[...]


def _layer_kernel(n_tiles, xn_ref, xc_ref, hb_ref, npre_ref, win_ref, wsp_ref, bsp_ref, gva_ref,
                  wgk_ref, bgk_ref, ltri_ref, utri_ref, gnb_ref, wout_ref, npost_ref,
                  out_ref, p0_ref, p1_ref, sf_ref, sb_ref, o_ref, mix_ref):
    i = pl.program_id(0)

    @pl.when(i == 0)
    def _():
        p1_ref[...] = jnp.zeros_like(p1_ref)

    @pl.when(jnp.logical_or(i == 0, (i - 1) % n_tiles == 0))
    def _():
        sf_ref[...] = jnp.zeros_like(sf_ref)

    def step(p_write_ref, p_read_ref):
        emit = _emitter(_projection_pieces(xn_ref, npre_ref, win_ref, p_write_ref))
        _mix(p_read_ref, xc_ref, hb_ref, wsp_ref, bsp_ref, gva_ref, wgk_ref, bgk_ref, ltri_ref,
             utri_ref, gnb_ref, wout_ref, npost_ref, out_ref, sf_ref, sb_ref, o_ref, mix_ref, emit)

    @pl.when(i % 2 == 0)
    def _():
        step(p0_ref, p1_ref)

    @pl.when(i % 2 == 1)
    def _():
        step(p1_ref, p0_ref)


def _const_spec(shape):
    return pl.BlockSpec(shape, lambda *_: (0,) * len(shape))


def _hybrid_layer(x, wts):
    bsz, seq, _ = x.shape
    n_tiles = seq // TILE
    n_total = bsz * n_tiles

    x_tiles = x.reshape(n_total, TILE, D_MODEL)
    params = pltpu.CompilerParams(dimension_semantics=("arbitrary",), vmem_limit_bytes=VMEM_LIMIT)

    hb = pl.pallas_call(
        functools.partial(_bwd_state_kernel, n_tiles, n_total),
        grid=(n_total + 1,),
        in_specs=[
            pl.BlockSpec((1, TILE, D_MODEL), lambda i: (jnp.maximum(n_total - 1 - i, 0), 0, 0)),
            _const_spec((1, D_MODEL)),
            _const_spec((D_MODEL, PRE_COLS + PIECE_COLS)),
            _const_spec((LANES, D_K)),
            _const_spec((1, D_K)),
            _const_spec((TILE, TILE)),
        ],
        out_specs=pl.BlockSpec((1, N_HEADS_B * HEAD_K, HEAD_V),
                               lambda i: (jnp.minimum(n_total - i, n_total - 1), 0, 0)),
        out_shape=jax.ShapeDtypeStruct((n_total, N_HEADS_B * HEAD_K, HEAD_V), F32),
        scratch_shapes=[
            pltpu.VMEM((TILE, PRE_COLS), F32),
            pltpu.VMEM((TILE, PRE_COLS), F32),
            pltpu.VMEM((N_HEADS_B * HEAD_K, HEAD_V), F32),
        ],
        compiler_params=params,
    )(x_tiles, wts["npre"], wts["wkv"], wts["wgk_b"], wts["bgk_b"], wts["utri"])

    y = pl.pallas_call(
        functools.partial(_layer_kernel, n_tiles),
        grid=(n_total + 1,),
        in_specs=[
            pl.BlockSpec((1, TILE, D_MODEL), lambda i: (jnp.minimum(i, n_total - 1), 0, 0)),
            pl.BlockSpec((1, TILE, D_MODEL), lambda i: (jnp.maximum(i - 1, 0), 0, 0)),
            pl.BlockSpec((1, N_HEADS_B * HEAD_K, HEAD_V),
                         lambda i: (jnp.maximum(i - 1, 0), 0, 0)),
            _const_spec((1, D_MODEL)),
            _const_spec((D_MODEL, P_COLS)),
            _const_spec((N_HEADS_A, CHUNK_A, CHUNK_A)),
            _const_spec((N_HEADS_A, CHUNK_A, HEAD_A)),
            _const_spec((1, D_A)),
            _const_spec((LANES, 2 * D_K)),
            _const_spec((1, 2 * D_K)),
            _const_spec((TILE, TILE)),
            _const_spec((TILE, TILE)),
            _const_spec((1, HEAD_V)),
            _const_spec((D_A + D_B, D_MODEL + PIECE_COLS)),
            _const_spec((1, D_MODEL)),
        ],
        out_specs=pl.BlockSpec((1, TILE, D_MODEL), lambda i: (jnp.maximum(i - 1, 0), 0, 0)),
        out_shape=jax.ShapeDtypeStruct((n_total, TILE, D_MODEL), x.dtype),
        scratch_shapes=[
            pltpu.VMEM((TILE, P_COLS), F32),
            pltpu.VMEM((TILE, P_COLS), F32),
            pltpu.VMEM((N_HEADS_B * HEAD_K, HEAD_V), F32),
            pltpu.VMEM((N_HEADS_B * HEAD_K, HEAD_V), F32),
            pltpu.VMEM((TILE, D_B), F32),
            pltpu.VMEM((TILE, D_A + D_B), BF16),
        ],
        compiler_params=params,
    )(x_tiles, x_tiles, hb, wts["npre"], wts["win"], wts["wsp"],
      wts["bsp"], wts["gva"], wts["wgk"], wts["bgk"], wts["ltri"], wts["utri"], wts["gnb"],
      wts["wout"], wts["npost"])
    return y.reshape(x.shape)


def _prepare_weights(norm_pre, w_in, w_sp, b_sp, g_v_a, w_gk_fwd, b_gk_fwd, w_gk_bwd, b_gk_bwd,
                     g_norm_b, w_out, norm_post):
    w_in_bf = w_in.astype(BF16)
    wlr = jnp.zeros((D_MODEL, PIECE_COLS), BF16).at[:, 0:2 * GATE_RANK].set(w_in_bf[:, OFF_LRF:D_IN])
    wlr_b = jnp.zeros((D_MODEL, PIECE_COLS), BF16).at[:, 0:GATE_RANK].set(w_in_bf[:, OFF_LRB:D_IN])
    wgk = jnp.zeros((LANES, 2 * D_K), BF16)
    wgk = wgk.at[0:GATE_RANK, 0:D_K].set(w_gk_fwd.astype(BF16))
    wgk = wgk.at[GATE_RANK:2 * GATE_RANK, D_K:2 * D_K].set(w_gk_bwd.astype(BF16))
    wgk_b = jnp.zeros((LANES, D_K), BF16).at[0:GATE_RANK, :].set(w_gk_bwd.astype(BF16))
    pad_cols = jnp.zeros((D_MODEL, PIECE_COLS), BF16)
    pos = jnp.arange(TILE)
    same_chunk = (pos[:, None] // CHUNK_B) == (pos[None, :] // CHUNK_B)
    ltri = (same_chunk & (pos[None, :] <= pos[:, None])).astype(BF16)
    utri = (same_chunk & (pos[None, :] >= pos[:, None])).astype(BF16)
    return {
        "npre": norm_pre.reshape(1, D_MODEL),
        "win": jnp.concatenate([w_in_bf[:, 0:OFF_LRF], wlr], axis=1),
        "wkv": jnp.concatenate([w_in_bf[:, OFF_K:OFF_ZB], wlr_b, pad_cols], axis=1),
        "wsp": w_sp.astype(BF16),
        "bsp": jnp.broadcast_to(b_sp[:, :, None], (N_HEADS_A, CHUNK_A, HEAD_A)),
        "gva": g_v_a.reshape(1, D_A),
        "wgk": wgk,
        "wgk_b": wgk_b,
        "bgk": jnp.concatenate([b_gk_fwd, b_gk_bwd]).reshape(1, 2 * D_K),
        "bgk_b": b_gk_bwd.reshape(1, D_K),
        "ltri": ltri,
        "utri": utri,
        "gnb": g_norm_b.reshape(1, HEAD_V),
        "wout": jnp.concatenate([w_out.astype(BF16), pad_cols], axis=1),
        "npost": norm_post.reshape(1, D_MODEL),
    }


def kernel(x_prompt, x_sample, norm_pre, w_in, w_sp, b_sp, g_v_a, w_gk_fwd, b_gk_fwd,
           w_gk_bwd, b_gk_bwd, g_norm_b, w_out, norm_post):
    y_prompt, y_sample = x_prompt, x_sample
    for l in range(norm_pre.shape[0]):
        wts = _prepare_weights(norm_pre[l], w_in[l], w_sp[l], b_sp[l], g_v_a[l], w_gk_fwd[l],
                               b_gk_fwd[l], w_gk_bwd[l], b_gk_bwd[l], g_norm_b[l], w_out[l],
                               norm_post[l])
        y_prompt = _hybrid_layer(y_prompt, wts)
        y_sample = _hybrid_layer(y_sample, wts)
    return (y_prompt, y_sample)
```

```python
import functools

import jax
import jax.numpy as jnp
from jax import lax
from jax.experimental import pallas as pl
from jax.experimental.pallas import tpu as pltpu

F32 = jnp.float32
BF16 = jnp.bfloat16

D_MODEL = 1024
D_A = 512
N_HEADS_A = 4
HEAD_A = 128
CHUNK_A = 128
D_B = 512
N_HEADS_B = 4
HEAD_V = 128
D_K = 256
HEAD_K = 64
GATE_RANK = 16
GATE_NORMALIZER = 16.0
CHUNK_B = 64
EPS = 1e-6

OFF_U, OFF_V, OFF_ZA, OFF_Q, OFF_K, OFF_VB, OFF_ZB, OFF_LRF, OFF_LRB, D_IN = (
    0, 512, 1024, 1536, 1792, 2048, 2560, 3072, 3088, 3104)

LANES = 128
TILE = 256
N_CHUNK = TILE // CHUNK_B
N_PAIR = N_HEADS_B // 2
PIECE_COLS = 256
PRE_COLS = D_K + D_B + PIECE_COLS
M_U, M_V, M_ZA, M_Q, M_ZB, M_LR = 0, 512, 1024, 1536, 1792, 2304
P_COLS = M_LR + PIECE_COLS
VMEM_LIMIT = 56 * 1024 * 1024


def _dot(a, b):
    return jnp.dot(a, b, preferred_element_type=F32)


def _dot_nt(a, b):
    return lax.dot_general(a, b, (((1,), (1,)), ((), ())), preferred_element_type=F32)


def _gelu(x):
    return 0.5 * x * (1.0 + jnp.tanh(0.7978845608028654 * (x + 0.044715 * (x * x * x))))


def _silu(x):
    return x * (0.5 * jnp.tanh(0.5 * x) + 0.5)


def _log_sigmoid(x):
    return jnp.minimum(x, 0.0) - jnp.log1p(jnp.exp(-jnp.abs(x)))


def _rms_scale(x):
    return lax.rsqrt(jnp.mean(x * x, axis=-1, keepdims=True) + EPS)


def _chunk_cumsum(tri_ref, g):
    hi = g.astype(BF16)
    lo = (g - hi.astype(F32)).astype(BF16)
    tri = tri_ref[...]
    return _dot(tri, hi) + _dot(tri, lo)


def _decayed_keys(k, bcum, backward):
    kd, dec = [], []
    for c in range(N_CHUNK):
        r0 = c * CHUNK_B
        last = r0 if backward else r0 + CHUNK_B - 1
        b_last = bcum[last:last + 1, :]
        kd.append(k[r0:r0 + CHUNK_B, :] * jnp.exp(b_last - bcum[r0:r0 + CHUNK_B, :]))
        dec.append(jnp.exp(b_last))
    return jnp.concatenate(kd, axis=0), dec


def _chunk_kv(kt, v_par, c):
    cp, par = divmod(c, 2)
    parts = []
    for p in range(N_PAIR):
        full = _dot(kt[p * LANES:(p + 1) * LANES, cp * LANES:(cp + 1) * LANES],
                    v_par[cp][par][:, p * 2 * HEAD_V:(p + 1) * 2 * HEAD_V])
        parts.append(full[0:HEAD_K, 0:HEAD_V])
        parts.append(full[HEAD_K:2 * HEAD_K, HEAD_V:2 * HEAD_V])
    return jnp.concatenate(parts, axis=0)


def _decay_column(dec_row):
    return jnp.transpose(jnp.broadcast_to(dec_row, (LANES, D_K)))


def _parity_masked_values(vb):
    out = []
    row = lax.broadcasted_iota(jnp.int32, (2 * CHUNK_B, D_B), 0)
    for cp in range(N_CHUNK // 2):
        vp = vb[cp * 2 * CHUNK_B:(cp + 1) * 2 * CHUNK_B, :]
        zero = jnp.zeros_like(vp)
        out.append((jnp.where(row < CHUNK_B, vp, zero), jnp.where(row >= CHUNK_B, vp, zero)))
    return out


def _projection_pieces(x_ref, npre_ref, w_ref, p_ref, sinks=()):
    x = x_ref[0]
    h = (x * _rms_scale(x) * npre_ref[...]).astype(BF16)

    def piece(c0):
        def run():
            block = _dot(h, w_ref[:, c0:c0 + PIECE_COLS])
            p_ref[:, c0:c0 + PIECE_COLS] = block
            if c0 // PIECE_COLS < len(sinks):
                sinks[c0 // PIECE_COLS](block)
        return run

    return [piece(c0) for c0 in range(0, p_ref.shape[1], PIECE_COLS)]


def _emitter(pieces):
    def emit(n):
        for _ in range(min(n, len(pieces))):
            pieces.pop(0)()
    return emit


def _bwd_state_kernel(n_tiles, n_total, xn_ref, npre_ref, wkv_ref, wgk_ref, bgk_ref, utri_ref,
                      hb_ref, k_ref, v_ref, p0_ref, p1_ref, s_ref):
    i = pl.program_id(0)

    @pl.when(i == 0)
    def _():
        p1_ref[...] = jnp.zeros_like(p1_ref)

    @pl.when(jnp.logical_or(i == 0, (n_total - i) % n_tiles == n_tiles - 1))
    def _():
        s_ref[...] = jnp.zeros_like(s_ref)

    def k_sink(block):
        k_ref[0] = block

    def v_sink(half):
        def sink(block):
            v_ref[0, :, half * PIECE_COLS:(half + 1) * PIECE_COLS] = block.astype(BF16)
        return sink

    def step(p_write_ref, p_ref):
        emit = _emitter(_projection_pieces(xn_ref, npre_ref, wkv_ref, p_write_ref,
                                           sinks=(k_sink, v_sink(0), v_sink(1))))
        hb_ref[0] = s_ref[...]
        emit(1)
        plr = p_ref[:, D_K + D_B:D_K + D_B + LANES]
        g = _log_sigmoid(_dot(plr.astype(BF16), wgk_ref[...]) + bgk_ref[...]) * (
            1.0 / GATE_NORMALIZER)
        emit(1)
        bcum = _chunk_cumsum(utri_ref, g)
        kdec, dec = _decayed_keys(p_ref[:, 0:D_K], bcum, backward=True)
        kt = jnp.transpose(kdec).astype(BF16)
        emit(1)
        v_par = _parity_masked_values(p_ref[:, D_K:D_K + D_B].astype(BF16))
        kv = [_chunk_kv(kt, v_par, c) for c in range(N_CHUNK)]
        dec_col = [_decay_column(dec[c]) for c in range(N_CHUNK)]
        emit(1)
        s = s_ref[...]
        for c in reversed(range(N_CHUNK)):
            s = dec_col[c] * s + kv[c]
        s_ref[...] = s

    @pl.when(i % 2 == 0)
    def _():
        step(p0_ref, p1_ref)

    @pl.when(i % 2 == 1)
    def _():
        step(p1_ref, p0_ref)


def _mix(p_ref, x_ref, hb_ref, k_ref, v_ref, wsp_ref, bsp_ref, gva_ref, wgk_ref, bgk_ref,
         ltri_ref, utri_ref, gnb_ref, wout_ref, npost_ref, out_ref, sf_ref, sb_ref, o_ref,
         mix_ref, emit):
    sb_ref[...] = hb_ref[0]

    emit(2)
    vg = _gelu(p_ref[:, M_V:M_ZA])
    vc = vg - jnp.mean(vg, axis=-1, keepdims=True)
    vn = (vc * _rms_scale(vc) * gva_ref[...]).astype(BF16)
    for c in range(TILE // CHUNK_A):
        r0 = c * CHUNK_A
        for hd in range(N_HEADS_A):
            c0 = hd * HEAD_A
            sv = _dot(wsp_ref[hd], vn[r0:r0 + CHUNK_A, c0:c0 + HEAD_A]) + bsp_ref[hd]
            out_a = _gelu(p_ref[r0:r0 + CHUNK_A, M_U + c0:M_U + c0 + HEAD_A]) * sv
            out_a = out_a * _silu(p_ref[r0:r0 + CHUNK_A, M_ZA + c0:M_ZA + c0 + HEAD_A])
            mix_ref[r0:r0 + CHUNK_A, c0:c0 + HEAD_A] = out_a.astype(BF16)
            emit(1 if hd == 1 else 0)

    q = p_ref[:, M_Q:M_ZB] * (HEAD_K ** -0.5)
    k = k_ref[0]
    vb = v_ref[0]
    plr = p_ref[:, M_LR:M_LR + LANES]
    emit(1)
    g = _log_sigmoid(_dot(plr.astype(BF16), wgk_ref[...]) + bgk_ref[...]) * (1.0 / GATE_NORMALIZER)
    v_par = _parity_masked_values(vb)

    lane_head = lax.broadcasted_iota(jnp.int32, (CHUNK_B, D_K), 1) // HEAD_K
    lane_pos = lax.broadcasted_iota(jnp.int32, (CHUNK_B, D_K), 1) % CHUNK_B
    row_pos = lax.broadcasted_iota(jnp.int32, (CHUNK_B, D_K), 0)
    zero_blk = jnp.zeros((CHUNK_B, HEAD_V), BF16)

    dirs = []
    for d in range(2):
        backward = d == 1
        emit(1)
        bcum = _chunk_cumsum(utri_ref if backward else ltri_ref, g[:, d * D_K:(d + 1) * D_K])
        q_in = (q * jnp.exp(bcum)).astype(BF16)
        k_in = (k * jnp.exp(-bcum)).astype(BF16)
        kdec, dec = _decayed_keys(k, bcum, backward)
        kt = jnp.transpose(kdec).astype(BF16)
        keep = (row_pos < lane_pos) if backward else (row_pos >= lane_pos)
        dirs.append((sb_ref if backward else sf_ref, q_in, k_in, kt, dec, keep))

    steps = [(t, d, N_CHUNK - 1 - t if d == 1 else t) for t in range(N_CHUNK) for d in range(2)]
    p_blk, kv_blk, dec_col = {}, {}, {}
    for t, d, c in steps:
        _, q_in, k_in, _, _, keep = dirs[d]
        k_c = k_in[c * CHUNK_B:(c + 1) * CHUNK_B, :]
        k_bd = jnp.concatenate(
            [jnp.where(lane_head == hd, k_c, jnp.zeros_like(k_c)) for hd in range(N_HEADS_B)],
            axis=0)
        scores = _dot_nt(q_in[c * CHUNK_B:(c + 1) * CHUNK_B, :], k_bd)
        p_blk[d, c] = jnp.where(keep, scores, 0.0).astype(BF16)
    for t, d, c in steps:
        _, _, _, kt, dec, _ = dirs[d]
        kv_blk[d, c] = _chunk_kv(kt, v_par, c)
        dec_col[d, c] = _decay_column(dec[c])

    for t, d, c in steps:
        s_ref, q_in = dirs[d][0], dirs[d][1]
        r0 = c * CHUNK_B
        s_old = s_ref[...]
        s_bf = s_old.astype(BF16)
        for p in range(N_PAIR):
            a0 = 2 * p * HEAD_V
            rhs = jnp.concatenate([
                jnp.concatenate([vb[r0:r0 + CHUNK_B, a0:a0 + HEAD_V], zero_blk], axis=1),
                jnp.concatenate([zero_blk, vb[r0:r0 + CHUNK_B, a0 + HEAD_V:a0 + 2 * HEAD_V]], axis=1),
                jnp.concatenate([s_bf[2 * p * HEAD_K:(2 * p + 1) * HEAD_K, :], zero_blk], axis=1),
                jnp.concatenate([zero_blk, s_bf[(2 * p + 1) * HEAD_K:(2 * p + 2) * HEAD_K, :]], axis=1),
            ], axis=0)
            lhs = jnp.concatenate([p_blk[d, c][:, p * LANES:(p + 1) * LANES],
                                   q_in[r0:r0 + CHUNK_B, p * LANES:(p + 1) * LANES]], axis=1)
            o_pair = _dot(lhs, rhs)
            if t < N_CHUNK // 2:
                o_ref[r0:r0 + CHUNK_B, a0:a0 + 2 * HEAD_V] = o_pair
            else:
                o_ref[r0:r0 + CHUNK_B, a0:a0 + 2 * HEAD_V] += o_pair
        s_ref[...] = dec_col[d, c] * s_old + kv_blk[d, c]

    emit(1)
    for hd in range(N_HEADS_B):
        c0 = hd * HEAD_V
        oh = o_ref[:, c0:c0 + HEAD_V]
        out_b = oh * _rms_scale(oh) * gnb_ref[...] * _silu(p_ref[:, M_ZB + c0:M_ZB + c0 + HEAD_V])
        mix_ref[:, D_A + c0:D_A + c0 + HEAD_V] = out_b.astype(BF16)

    mixed = jnp.concatenate(
        [_dot(mix_ref[...], wout_ref[:, c0:c0 + PIECE_COLS]) for c0 in range(0, D_MODEL, PIECE_COLS)],
        axis=1)
    emit(P_COLS // PIECE_COLS)
    out_ref[0] = x_ref[0] + mixed * _rms_scale(mixed) * npost_ref[...]


def _layer_kernel(n_tiles, xn_ref, xc_ref, hb_ref, k_ref, v_ref, npre_ref, win_ref, wsp_ref,
                  bsp_ref, gva_ref, wgk_ref, bgk_ref, ltri_ref, utri_ref, gnb_ref, wout_ref,
                  npost_ref, out_ref, p0_ref, p1_ref, sf_ref, sb_ref, o_ref, mix_ref):
    i = pl.program_id(0)

    @pl.when(i == 0)
    def _():
        p1_ref[...] = jnp.zeros_like(p1_ref)

    @pl.when(jnp.logical_or(i == 0, (i - 1) % n_tiles == 0))
    def _():
        sf_ref[...] = jnp.zeros_like(sf_ref)

    def step(p_write_ref, p_read_ref):
        emit = _emitter(_projection_pieces(xn_ref, npre_ref, win_ref, p_write_ref))
        _mix(p_read_ref, xc_ref, hb_ref, k_ref, v_ref, wsp_ref, bsp_ref, gva_ref, wgk_ref, bgk_ref,
             ltri_ref, utri_ref, gnb_ref, wout_ref, npost_ref, out_ref, sf_ref, sb_ref, o_ref,
             mix_ref, emit)

    @pl.when(i % 2 == 0)
    def _():
        step(p0_ref, p1_ref)

    @pl.when(i % 2 == 1)
    def _():
        step(p1_ref, p0_ref)


def _const_spec(shape):
    return pl.BlockSpec(shape, lambda *_: (0,) * len(shape))


def _hybrid_layer(x, wts):
    bsz, seq, _ = x.shape
    n_tiles = seq // TILE
    n_total = bsz * n_tiles

    x_tiles = x.reshape(n_total, TILE, D_MODEL)
    params = pltpu.CompilerParams(dimension_semantics=("arbitrary",), vmem_limit_bytes=VMEM_LIMIT)

    hb, k_tiles, v_tiles = pl.pallas_call(
        functools.partial(_bwd_state_kernel, n_tiles, n_total),
        grid=(n_total + 1,),
        in_specs=[
            pl.BlockSpec((1, TILE, D_MODEL), lambda i: (jnp.maximum(n_total - 1 - i, 0), 0, 0)),
            _const_spec((1, D_MODEL)),
            _const_spec((D_MODEL, PRE_COLS + PIECE_COLS)),
            _const_spec((LANES, D_K)),
            _const_spec((1, D_K)),
            _const_spec((TILE, TILE)),
        ],
        out_specs=[
            pl.BlockSpec((1, N_HEADS_B * HEAD_K, HEAD_V),
                         lambda i: (jnp.minimum(n_total - i, n_total - 1), 0, 0)),
            pl.BlockSpec((1, TILE, D_K), lambda i: (jnp.maximum(n_total - 1 - i, 0), 0, 0)),
            pl.BlockSpec((1, TILE, D_B), lambda i: (jnp.maximum(n_total - 1 - i, 0), 0, 0)),
        ],
        out_shape=[
            jax.ShapeDtypeStruct((n_total, N_HEADS_B * HEAD_K, HEAD_V), F32),
            jax.ShapeDtypeStruct((n_total, TILE, D_K), F32),
            jax.ShapeDtypeStruct((n_total, TILE, D_B), BF16),
        ],
        scratch_shapes=[
            pltpu.VMEM((TILE, PRE_COLS), F32),
            pltpu.VMEM((TILE, PRE_COLS), F32),
            pltpu.VMEM((N_HEADS_B * HEAD_K, HEAD_V), F32),
        ],
        compiler_params=params,
    )(x_tiles, wts["npre"], wts["wkv"], wts["wgk_b"], wts["bgk_b"], wts["utri"])

    y = pl.pallas_call(
        functools.partial(_layer_kernel, n_tiles),
        grid=(n_total + 1,),
        in_specs=[
            pl.BlockSpec((1, TILE, D_MODEL), lambda i: (jnp.minimum(i, n_total - 1), 0, 0)),
            pl.BlockSpec((1, TILE, D_MODEL), lambda i: (jnp.maximum(i - 1, 0), 0, 0)),
            pl.BlockSpec((1, N_HEADS_B * HEAD_K, HEAD_V),
                         lambda i: (jnp.maximum(i - 1, 0), 0, 0)),
            pl.BlockSpec((1, TILE, D_K), lambda i: (jnp.maximum(i - 1, 0), 0, 0)),
            pl.BlockSpec((1, TILE, D_B), lambda i: (jnp.maximum(i - 1, 0), 0, 0)),
            _const_spec((1, D_MODEL)),
            _const_spec((D_MODEL, P_COLS)),
            _const_spec((N_HEADS_A, CHUNK_A, CHUNK_A)),
            _const_spec((N_HEADS_A, CHUNK_A, HEAD_A)),
            _const_spec((1, D_A)),
            _const_spec((LANES, 2 * D_K)),
            _const_spec((1, 2 * D_K)),
            _const_spec((TILE, TILE)),
            _const_spec((TILE, TILE)),
            _const_spec((1, HEAD_V)),
            _const_spec((D_A + D_B, D_MODEL + PIECE_COLS)),
            _const_spec((1, D_MODEL)),
        ],
        out_specs=pl.BlockSpec((1, TILE, D_MODEL), lambda i: (jnp.maximum(i - 1, 0), 0, 0)),
        out_shape=jax.ShapeDtypeStruct((n_total, TILE, D_MODEL), x.dtype),
        scratch_shapes=[
            pltpu.VMEM((TILE, P_COLS), F32),
            pltpu.VMEM((TILE, P_COLS), F32),
            pltpu.VMEM((N_HEADS_B * HEAD_K, HEAD_V), F32),
            pltpu.VMEM((N_HEADS_B * HEAD_K, HEAD_V), F32),
            pltpu.VMEM((TILE, D_B), F32),
            pltpu.VMEM((TILE, D_A + D_B), BF16),
        ],
        compiler_params=params,
    )(x_tiles, x_tiles, hb, k_tiles, v_tiles, wts["npre"], wts["win"], wts["wsp"],
      wts["bsp"], wts["gva"], wts["wgk"], wts["bgk"], wts["ltri"], wts["utri"], wts["gnb"],
      wts["wout"], wts["npost"])
    return y.reshape(x.shape)


def _prepare_weights(norm_pre, w_in, w_sp, b_sp, g_v_a, w_gk_fwd, b_gk_fwd, w_gk_bwd, b_gk_bwd,
                     g_norm_b, w_out, norm_post):
    w_in_bf = w_in.astype(BF16)
    wlr = jnp.zeros((D_MODEL, PIECE_COLS), BF16).at[:, 0:2 * GATE_RANK].set(w_in_bf[:, OFF_LRF:D_IN])
    wlr_b = jnp.zeros((D_MODEL, PIECE_COLS), BF16).at[:, 0:GATE_RANK].set(w_in_bf[:, OFF_LRB:D_IN])
    wgk = jnp.zeros((LANES, 2 * D_K), BF16)
    wgk = wgk.at[0:GATE_RANK, 0:D_K].set(w_gk_fwd.astype(BF16))
    wgk = wgk.at[GATE_RANK:2 * GATE_RANK, D_K:2 * D_K].set(w_gk_bwd.astype(BF16))
    wgk_b = jnp.zeros((LANES, D_K), BF16).at[0:GATE_RANK, :].set(w_gk_bwd.astype(BF16))
    pad_cols = jnp.zeros((D_MODEL, PIECE_COLS), BF16)
    pos = jnp.arange(TILE)
    same_chunk = (pos[:, None] // CHUNK_B) == (pos[None, :] // CHUNK_B)
    ltri = (same_chunk & (pos[None, :] <= pos[:, None])).astype(BF16)
    utri = (same_chunk & (pos[None, :] >= pos[:, None])).astype(BF16)
    return {
        "npre": norm_pre.reshape(1, D_MODEL),
        "win": jnp.concatenate([w_in_bf[:, OFF_U:OFF_K], w_in_bf[:, OFF_ZB:OFF_LRF], wlr], axis=1),
        "wkv": jnp.concatenate([w_in_bf[:, OFF_K:OFF_ZB], wlr_b, pad_cols], axis=1),
        "wsp": w_sp.astype(BF16),
        "bsp": jnp.broadcast_to(b_sp[:, :, None], (N_HEADS_A, CHUNK_A, HEAD_A)),
        "gva": g_v_a.reshape(1, D_A),
        "wgk": wgk,
        "wgk_b": wgk_b,
        "bgk": jnp.concatenate([b_gk_fwd, b_gk_bwd]).reshape(1, 2 * D_K),
        "bgk_b": b_gk_bwd.reshape(1, D_K),
        "ltri": ltri,
        "utri": utri,
        "gnb": g_norm_b.reshape(1, HEAD_V),
        "wout": jnp.concatenate([w_out.astype(BF16), pad_cols], axis=1),
        "npost": norm_post.reshape(1, D_MODEL),
    }


def kernel(x_prompt, x_sample, norm_pre, w_in, w_sp, b_sp, g_v_a, w_gk_fwd, b_gk_fwd,
           w_gk_bwd, b_gk_bwd, g_norm_b, w_out, norm_post):
    y_prompt, y_sample = x_prompt, x_sample
    for l in range(norm_pre.shape[0]):
        wts = _prepare_weights(norm_pre[l], w_in[l], w_sp[l], b_sp[l], g_v_a[l], w_gk_fwd[l],
                               b_gk_fwd[l], w_gk_bwd[l], b_gk_bwd[l], g_norm_b[l], w_out[l],
                               norm_post[l])
        y_prompt = _hybrid_layer(y_prompt, wts)
        y_sample = _hybrid_layer(y_sample, wts)
    return (y_prompt, y_sample)
```

```python
import functools

import jax
import jax.numpy as jnp
from jax import lax
from jax.experimental import pallas as pl
from jax.experimental.pallas import tpu as pltpu

F32 = jnp.float32
BF16 = jnp.bfloat16

D_MODEL = 1024
D_A = 512
N_HEADS_A = 4
HEAD_A = 128
CHUNK_A = 128
D_B = 512
N_HEADS_B = 4
HEAD_V = 128
D_K = 256
HEAD_K = 64
GATE_RANK = 16
GATE_NORMALIZER = 16.0
CHUNK_B = 64
EPS = 1e-6

OFF_U, OFF_V, OFF_ZA, OFF_Q, OFF_K, OFF_VB, OFF_ZB, OFF_LRF, OFF_LRB, D_IN = (
    0, 512, 1024, 1536, 1792, 2048, 2560, 3072, 3088, 3104)

LANES = 128
TILE = 256
N_CHUNK = TILE // CHUNK_B
N_PAIR = N_HEADS_B // 2
PIECE_COLS = 256
PRE_COLS = D_K + D_B + PIECE_COLS
M_U, M_V, M_ZA, M_Q, M_ZB, P_COLS = 0, 512, 1024, 1536, 1792, 2304
VMEM_LIMIT = 56 * 1024 * 1024


def _dot(a, b):
    return jnp.dot(a, b, preferred_element_type=F32)


def _dot_nt(a, b):
    return lax.dot_general(a, b, (((1,), (1,)), ((), ())), preferred_element_type=F32)


def _gelu(x):
    return 0.5 * x * (1.0 + jnp.tanh(0.7978845608028654 * (x + 0.044715 * (x * x * x))))


def _silu(x):
    return x * (0.5 * jnp.tanh(0.5 * x) + 0.5)


def _log_sigmoid(x):
    return jnp.minimum(x, 0.0) - jnp.log1p(jnp.exp(-jnp.abs(x)))


def _rms_scale(x):
    return lax.rsqrt(jnp.mean(x * x, axis=-1, keepdims=True) + EPS)


def _chunk_cumsum(tri_ref, g):
    hi = g.astype(BF16)
    lo = (g - hi.astype(F32)).astype(BF16)
    tri = tri_ref[...]
    return _dot(tri, hi) + _dot(tri, lo)


def _decayed_keys(k, bcum, backward):
    kd, dec = [], []
    for c in range(N_CHUNK):
        r0 = c * CHUNK_B
        last = r0 if backward else r0 + CHUNK_B - 1
        b_last = bcum[last:last + 1, :]
        kd.append(k[r0:r0 + CHUNK_B, :] * jnp.exp(b_last - bcum[r0:r0 + CHUNK_B, :]))
        dec.append(jnp.exp(b_last))
    return jnp.concatenate(kd, axis=0), dec


def _chunk_kv(kt, v_par, c):
    cp, par = divmod(c, 2)
    parts = []
    for p in range(N_PAIR):
        full = _dot(kt[p * LANES:(p + 1) * LANES, cp * LANES:(cp + 1) * LANES],
                    v_par[cp][par][:, p * 2 * HEAD_V:(p + 1) * 2 * HEAD_V])
        parts.append(full[0:HEAD_K, 0:HEAD_V])
        parts.append(full[HEAD_K:2 * HEAD_K, HEAD_V:2 * HEAD_V])
    return jnp.concatenate(parts, axis=0)


def _decay_column(dec_row):
    return jnp.transpose(jnp.broadcast_to(dec_row, (LANES, D_K)))


def _parity_masked_values(vb):
    out = []
    row = lax.broadcasted_iota(jnp.int32, (2 * CHUNK_B, D_B), 0)
    for cp in range(N_CHUNK // 2):
        vp = vb[cp * 2 * CHUNK_B:(cp + 1) * 2 * CHUNK_B, :]
        zero = jnp.zeros_like(vp)
        out.append((jnp.where(row < CHUNK_B, vp, zero), jnp.where(row >= CHUNK_B, vp, zero)))
    return out


def _projection_pieces(x_ref, npre_ref, w_ref, p_ref, sinks=()):
    x = x_ref[0]
    h = (x * _rms_scale(x) * npre_ref[...]).astype(BF16)

    def piece(c0):
        def run():
            block = _dot(h, w_ref[:, c0:c0 + PIECE_COLS])
            p_ref[:, c0:c0 + PIECE_COLS] = block
            if c0 // PIECE_COLS < len(sinks):
                sinks[c0 // PIECE_COLS](block)
        return run

    return [piece(c0) for c0 in range(0, p_ref.shape[1], PIECE_COLS)]


def _emitter(pieces):
    def emit(n):
        for _ in range(min(n, len(pieces))):
            pieces.pop(0)()
    return emit


def _bwd_state_kernel(n_tiles, n_total, xn_ref, npre_ref, wkv_ref, wgk_ref, bgk_ref, utri_ref,
                      hb_ref, k_ref, v_ref, lr_ref, p0_ref, p1_ref, s_ref):
    i = pl.program_id(0)

    @pl.when(i == 0)
    def _():
        p1_ref[...] = jnp.zeros_like(p1_ref)

    @pl.when(jnp.logical_or(i == 0, (n_total - i) % n_tiles == n_tiles - 1))
    def _():
        s_ref[...] = jnp.zeros_like(s_ref)

    def k_sink(block):
        k_ref[0] = block

    def v_sink(half):
        def sink(block):
            v_ref[0, :, half * PIECE_COLS:(half + 1) * PIECE_COLS] = block.astype(BF16)
        return sink

    def lr_sink(block):
        lr_ref[0] = block[:, 0:LANES]

    def step(p_write_ref, p_ref):
        emit = _emitter(_projection_pieces(xn_ref, npre_ref, wkv_ref, p_write_ref,
                                           sinks=(k_sink, v_sink(0), v_sink(1), lr_sink)))
        hb_ref[0] = s_ref[...]
        emit(1)
        plr = p_ref[:, D_K + D_B:D_K + D_B + LANES]
        g = _log_sigmoid(_dot(plr.astype(BF16), wgk_ref[...]) + bgk_ref[...]) * (
            1.0 / GATE_NORMALIZER)
        emit(1)
        bcum = _chunk_cumsum(utri_ref, g)
        kdec, dec = _decayed_keys(p_ref[:, 0:D_K], bcum, backward=True)
        kt = jnp.transpose(kdec).astype(BF16)
        emit(1)
        v_par = _parity_masked_values(p_ref[:, D_K:D_K + D_B].astype(BF16))
        kv = [_chunk_kv(kt, v_par, c) for c in range(N_CHUNK)]
        dec_col = [_decay_column(dec[c]) for c in range(N_CHUNK)]
        emit(1)
        s = s_ref[...]
        for c in reversed(range(N_CHUNK)):
            s = dec_col[c] * s + kv[c]
        s_ref[...] = s

    @pl.when(i % 2 == 0)
    def _():
        step(p0_ref, p1_ref)

    @pl.when(i % 2 == 1)
    def _():
        step(p1_ref, p0_ref)


def _mix(p_ref, x_ref, hb_ref, k_ref, v_ref, lr_ref, wsp_ref, bsp_ref, gva_ref, wgk_ref, bgk_ref,
         ltri_ref, utri_ref, gnb_ref, wout_ref, npost_ref, out_ref, sf_ref, sb_ref, o_ref,
         mix_ref, emit):
    sb_ref[...] = hb_ref[0]

    emit(2)
    vg = _gelu(p_ref[:, M_V:M_ZA])
    vc = vg - jnp.mean(vg, axis=-1, keepdims=True)
    vn = (vc * _rms_scale(vc) * gva_ref[...]).astype(BF16)
    for c in range(TILE // CHUNK_A):
        r0 = c * CHUNK_A
        for hd in range(N_HEADS_A):
            c0 = hd * HEAD_A
            sv = _dot(wsp_ref[hd], vn[r0:r0 + CHUNK_A, c0:c0 + HEAD_A]) + bsp_ref[hd]
            out_a = _gelu(p_ref[r0:r0 + CHUNK_A, M_U + c0:M_U + c0 + HEAD_A]) * sv
            out_a = out_a * _silu(p_ref[r0:r0 + CHUNK_A, M_ZA + c0:M_ZA + c0 + HEAD_A])
            mix_ref[r0:r0 + CHUNK_A, c0:c0 + HEAD_A] = out_a.astype(BF16)
            emit(1 if hd == 1 else 0)

    q = p_ref[:, M_Q:M_ZB] * (HEAD_K ** -0.5)
    k = k_ref[0]
    vb = v_ref[0]
    plr = lr_ref[0]
    emit(1)
    g = _log_sigmoid(_dot(plr.astype(BF16), wgk_ref[...]) + bgk_ref[...]) * (1.0 / GATE_NORMALIZER)
    v_par = _parity_masked_values(vb)

    lane_head = lax.broadcasted_iota(jnp.int32, (CHUNK_B, D_K), 1) // HEAD_K
    lane_pos = lax.broadcasted_iota(jnp.int32, (CHUNK_B, D_K), 1) % CHUNK_B
    row_pos = lax.broadcasted_iota(jnp.int32, (CHUNK_B, D_K), 0)
    zero_blk = jnp.zeros((CHUNK_B, HEAD_V), BF16)

    dirs = []
    for d in range(2):
        backward = d == 1
        emit(1)
        bcum = _chunk_cumsum(utri_ref if backward else ltri_ref, g[:, d * D_K:(d + 1) * D_K])
        q_in = (q * jnp.exp(bcum)).astype(BF16)
        k_in = (k * jnp.exp(-bcum)).astype(BF16)
        kdec, dec = _decayed_keys(k, bcum, backward)
        kt = jnp.transpose(kdec).astype(BF16)
        keep = (row_pos < lane_pos) if backward else (row_pos >= lane_pos)
        dirs.append((sb_ref if backward else sf_ref, q_in, k_in, kt, dec, keep))

    steps = [(t, d, N_CHUNK - 1 - t if d == 1 else t) for t in range(N_CHUNK) for d in range(2)]
    p_blk, kv_blk, dec_col = {}, {}, {}
    for t, d, c in steps:
        _, q_in, k_in, _, _, keep = dirs[d]
        k_c = k_in[c * CHUNK_B:(c + 1) * CHUNK_B, :]
        k_bd = jnp.concatenate(
            [jnp.where(lane_head == hd, k_c, jnp.zeros_like(k_c)) for hd in range(N_HEADS_B)],
            axis=0)
        scores = _dot_nt(q_in[c * CHUNK_B:(c + 1) * CHUNK_B, :], k_bd)
        p_blk[d, c] = jnp.where(keep, scores, 0.0).astype(BF16)
    for t, d, c in steps:
        _, _, _, kt, dec, _ = dirs[d]
        kv_blk[d, c] = _chunk_kv(kt, v_par, c)
        dec_col[d, c] = _decay_column(dec[c])

    for t, d, c in steps:
        s_ref, q_in = dirs[d][0], dirs[d][1]
        r0 = c * CHUNK_B
        s_old = s_ref[...]
        s_bf = s_old.astype(BF16)
        for p in range(N_PAIR):
            a0 = 2 * p * HEAD_V
            rhs = jnp.concatenate([
                jnp.concatenate([vb[r0:r0 + CHUNK_B, a0:a0 + HEAD_V], zero_blk], axis=1),
                jnp.concatenate([zero_blk, vb[r0:r0 + CHUNK_B, a0 + HEAD_V:a0 + 2 * HEAD_V]], axis=1),
                jnp.concatenate([s_bf[2 * p * HEAD_K:(2 * p + 1) * HEAD_K, :], zero_blk], axis=1),
                jnp.concatenate([zero_blk, s_bf[(2 * p + 1) * HEAD_K:(2 * p + 2) * HEAD_K, :]], axis=1),
            ], axis=0)
            lhs = jnp.concatenate([p_blk[d, c][:, p * LANES:(p + 1) * LANES],
                                   q_in[r0:r0 + CHUNK_B, p * LANES:(p + 1) * LANES]], axis=1)
            o_pair = _dot(lhs, rhs)
            if t < N_CHUNK // 2:
                o_ref[r0:r0 + CHUNK_B, a0:a0 + 2 * HEAD_V] = o_pair
            else:
                o_ref[r0:r0 + CHUNK_B, a0:a0 + 2 * HEAD_V] += o_pair
        s_ref[...] = dec_col[d, c] * s_old + kv_blk[d, c]

    emit(1)
    for hd in range(N_HEADS_B):
        c0 = hd * HEAD_V
        oh = o_ref[:, c0:c0 + HEAD_V]
        out_b = oh * _rms_scale(oh) * gnb_ref[...] * _silu(p_ref[:, M_ZB + c0:M_ZB + c0 + HEAD_V])
        mix_ref[:, D_A + c0:D_A + c0 + HEAD_V] = out_b.astype(BF16)

    mixed = jnp.concatenate(
        [_dot(mix_ref[...], wout_ref[:, c0:c0 + PIECE_COLS]) for c0 in range(0, D_MODEL, PIECE_COLS)],
        axis=1)
    emit(P_COLS // PIECE_COLS)
    out_ref[0] = x_ref[0] + mixed * _rms_scale(mixed) * npost_ref[...]


def _layer_kernel(n_tiles, xn_ref, xc_ref, hb_ref, k_ref, v_ref, lr_ref, npre_ref, win_ref,
                  wsp_ref, bsp_ref, gva_ref, wgk_ref, bgk_ref, ltri_ref, utri_ref, gnb_ref,
                  wout_ref, npost_ref, out_ref, p0_ref, p1_ref, sf_ref, sb_ref, o_ref, mix_ref):
    i = pl.program_id(0)

    @pl.when(i == 0)
    def _():
        p1_ref[...] = jnp.zeros_like(p1_ref)

    @pl.when(jnp.logical_or(i == 0, (i - 1) % n_tiles == 0))
    def _():
        sf_ref[...] = jnp.zeros_like(sf_ref)

    def step(p_write_ref, p_read_ref):
        emit = _emitter(_projection_pieces(xn_ref, npre_ref, win_ref, p_write_ref))
        _mix(p_read_ref, xc_ref, hb_ref, k_ref, v_ref, lr_ref, wsp_ref, bsp_ref, gva_ref, wgk_ref,
             bgk_ref, ltri_ref, utri_ref, gnb_ref, wout_ref, npost_ref, out_ref, sf_ref, sb_ref,
             o_ref, mix_ref, emit)

    @pl.when(i % 2 == 0)
    def _():
        step(p0_ref, p1_ref)

    @pl.when(i % 2 == 1)
    def _():
        step(p1_ref, p0_ref)


def _const_spec(shape):
    return pl.BlockSpec(shape, lambda *_: (0,) * len(shape))


def _hybrid_layer(x, wts):
    bsz, seq, _ = x.shape
    n_tiles = seq // TILE
    n_total = bsz * n_tiles

    x_tiles = x.reshape(n_total, TILE, D_MODEL)
    params = pltpu.CompilerParams(dimension_semantics=("arbitrary",), vmem_limit_bytes=VMEM_LIMIT)

    def projected_tile(i):
        return (jnp.maximum(n_total - 1 - i, 0), 0, 0)

    hb, k_tiles, v_tiles, lr_tiles = pl.pallas_call(
        functools.partial(_bwd_state_kernel, n_tiles, n_total),
        grid=(n_total + 1,),
        in_specs=[
            pl.BlockSpec((1, TILE, D_MODEL), projected_tile),
            _const_spec((1, D_MODEL)),
            _const_spec((D_MODEL, PRE_COLS + PIECE_COLS)),
            _const_spec((LANES, D_K)),
            _const_spec((1, D_K)),
            _const_spec((TILE, TILE)),
        ],
        out_specs=[
            pl.BlockSpec((1, N_HEADS_B * HEAD_K, HEAD_V),
                         lambda i: (jnp.minimum(n_total - i, n_total - 1), 0, 0)),
            pl.BlockSpec((1, TILE, D_K), projected_tile),
            pl.BlockSpec((1, TILE, D_B), projected_tile),
            pl.BlockSpec((1, TILE, LANES), projected_tile),
        ],
        out_shape=[
            jax.ShapeDtypeStruct((n_total, N_HEADS_B * HEAD_K, HEAD_V), F32),
            jax.ShapeDtypeStruct((n_total, TILE, D_K), F32),
            jax.ShapeDtypeStruct((n_total, TILE, D_B), BF16),
            jax.ShapeDtypeStruct((n_total, TILE, LANES), F32),
        ],
        scratch_shapes=[
            pltpu.VMEM((TILE, PRE_COLS), F32),
            pltpu.VMEM((TILE, PRE_COLS), F32),
            pltpu.VMEM((N_HEADS_B * HEAD_K, HEAD_V), F32),
        ],
        compiler_params=params,
    )(x_tiles, wts["npre"], wts["wkv"], wts["wgk_b"], wts["bgk_b"], wts["utri"])

    def finished_tile(i):
        return (jnp.maximum(i - 1, 0), 0, 0)

    y = pl.pallas_call(
        functools.partial(_layer_kernel, n_tiles),
        grid=(n_total + 1,),
        in_specs=[
            pl.BlockSpec((1, TILE, D_MODEL), lambda i: (jnp.minimum(i, n_total - 1), 0, 0)),
            pl.BlockSpec((1, TILE, D_MODEL), finished_tile),
            pl.BlockSpec((1, N_HEADS_B * HEAD_K, HEAD_V), finished_tile),
            pl.BlockSpec((1, TILE, D_K), finished_tile),
            pl.BlockSpec((1, TILE, D_B), finished_tile),
            pl.BlockSpec((1, TILE, LANES), finished_tile),
            _const_spec((1, D_MODEL)),
            _const_spec((D_MODEL, P_COLS)),
            _const_spec((N_HEADS_A, CHUNK_A, CHUNK_A)),
            _const_spec((N_HEADS_A, CHUNK_A, HEAD_A)),
            _const_spec((1, D_A)),
            _const_spec((LANES, 2 * D_K)),
            _const_spec((1, 2 * D_K)),
            _const_spec((TILE, TILE)),
            _const_spec((TILE, TILE)),
            _const_spec((1, HEAD_V)),
            _const_spec((D_A + D_B, D_MODEL + PIECE_COLS)),
            _const_spec((1, D_MODEL)),
        ],
        out_specs=pl.BlockSpec((1, TILE, D_MODEL), finished_tile),
        out_shape=jax.ShapeDtypeStruct((n_total, TILE, D_MODEL), x.dtype),
        scratch_shapes=[
            pltpu.VMEM((TILE, P_COLS), F32),
            pltpu.VMEM((TILE, P_COLS), F32),
            pltpu.VMEM((N_HEADS_B * HEAD_K, HEAD_V), F32),
            pltpu.VMEM((N_HEADS_B * HEAD_K, HEAD_V), F32),
            pltpu.VMEM((TILE, D_B), F32),
            pltpu.VMEM((TILE, D_A + D_B), BF16),
        ],
        compiler_params=params,
    )(x_tiles, x_tiles, hb, k_tiles, v_tiles, lr_tiles, wts["npre"], wts["win"], wts["wsp"],
      wts["bsp"], wts["gva"], wts["wgk"], wts["bgk"], wts["ltri"], wts["utri"], wts["gnb"],
      wts["wout"], wts["npost"])
    return y.reshape(x.shape)


def _prepare_weights(norm_pre, w_in, w_sp, b_sp, g_v_a, w_gk_fwd, b_gk_fwd, w_gk_bwd, b_gk_bwd,
                     g_norm_b, w_out, norm_post):
    w_in_bf = w_in.astype(BF16)
    wlr = jnp.zeros((D_MODEL, PIECE_COLS), BF16).at[:, 0:2 * GATE_RANK].set(w_in_bf[:, OFF_LRF:D_IN])
    wgk = jnp.zeros((LANES, 2 * D_K), BF16)
    wgk = wgk.at[0:GATE_RANK, 0:D_K].set(w_gk_fwd.astype(BF16))
    wgk = wgk.at[GATE_RANK:2 * GATE_RANK, D_K:2 * D_K].set(w_gk_bwd.astype(BF16))
    pad_cols = jnp.zeros((D_MODEL, PIECE_COLS), BF16)
    pos = jnp.arange(TILE)
    same_chunk = (pos[:, None] // CHUNK_B) == (pos[None, :] // CHUNK_B)
    ltri = (same_chunk & (pos[None, :] <= pos[:, None])).astype(BF16)
    utri = (same_chunk & (pos[None, :] >= pos[:, None])).astype(BF16)
    return {
        "npre": norm_pre.reshape(1, D_MODEL),
        "win": jnp.concatenate([w_in_bf[:, OFF_U:OFF_K], w_in_bf[:, OFF_ZB:OFF_LRF]], axis=1),
        "wkv": jnp.concatenate([w_in_bf[:, OFF_K:OFF_ZB], wlr, pad_cols], axis=1),
        "wsp": w_sp.astype(BF16),
        "bsp": jnp.broadcast_to(b_sp[:, :, None], (N_HEADS_A, CHUNK_A, HEAD_A)),
        "gva": g_v_a.reshape(1, D_A),
        "wgk": wgk,
        "wgk_b": wgk[:, D_K:2 * D_K],
        "bgk": jnp.concatenate([b_gk_fwd, b_gk_bwd]).reshape(1, 2 * D_K),
        "bgk_b": b_gk_bwd.reshape(1, D_K),
        "ltri": ltri,
        "utri": utri,
        "gnb": g_norm_b.reshape(1, HEAD_V),
        "wout": jnp.concatenate([w_out.astype(BF16), pad_cols], axis=1),
        "npost": norm_post.reshape(1, D_MODEL),
    }


def kernel(x_prompt, x_sample, norm_pre, w_in, w_sp, b_sp, g_v_a, w_gk_fwd, b_gk_fwd,
           w_gk_bwd, b_gk_bwd, g_norm_b, w_out, norm_post):
    y_prompt, y_sample = x_prompt, x_sample
    for l in range(norm_pre.shape[0]):
        wts = _prepare_weights(norm_pre[l], w_in[l], w_sp[l], b_sp[l], g_v_a[l], w_gk_fwd[l],
                               b_gk_fwd[l], w_gk_bwd[l], b_gk_bwd[l], g_norm_b[l], w_out[l],
                               norm_post[l])
        y_prompt = _hybrid_layer(y_prompt, wts)
        y_sample = _hybrid_layer(y_sample, wts)
    return (y_prompt, y_sample)
```

```python
import functools

import jax
import jax.numpy as jnp
from jax import lax
from jax.experimental import pallas as pl
from jax.experimental.pallas import tpu as pltpu

F32 = jnp.float32
BF16 = jnp.bfloat16

D_MODEL = 1024
D_A = 512
N_HEADS_A = 4
HEAD_A = 128
CHUNK_A = 128
D_B = 512
N_HEADS_B = 4
HEAD_V = 128
D_K = 256
HEAD_K = 64
GATE_RANK = 16
GATE_NORMALIZER = 16.0
CHUNK_B = 64
EPS = 1e-6

OFF_U, OFF_V, OFF_ZA, OFF_Q, OFF_K, OFF_VB, OFF_ZB, OFF_LRF, OFF_LRB, D_IN = (
    0, 512, 1024, 1536, 1792, 2048, 2560, 3072, 3088, 3104)

LANES = 128
TILE = 256
N_CHUNK = TILE // CHUNK_B
N_PAIR = N_HEADS_B // 2
PIECE_COLS = 256
PRE_COLS = D_K + D_B + PIECE_COLS
M_U, M_V, M_ZA, M_Q, M_ZB, P_COLS = 0, 512, 1024, 1536, 1792, 2304
VMEM_LIMIT = 56 * 1024 * 1024


def _dot(a, b):
    return jnp.dot(a, b, preferred_element_type=F32)


def _dot_nt(a, b):
    return lax.dot_general(a, b, (((1,), (1,)), ((), ())), preferred_element_type=F32)


def _gelu(x):
    return 0.5 * x * (1.0 + jnp.tanh(0.7978845608028654 * (x + 0.044715 * (x * x * x))))


def _silu(x):
    return x * (0.5 * jnp.tanh(0.5 * x) + 0.5)


def _log_sigmoid(x):
    return jnp.minimum(x, 0.0) - jnp.log1p(jnp.exp(-jnp.abs(x)))


def _rms_scale(x):
    return lax.rsqrt(jnp.mean(x * x, axis=-1, keepdims=True) + EPS)


def _chunk_cumsum(tri_ref, g):
    hi = g.astype(BF16)
    lo = (g - hi.astype(F32)).astype(BF16)
    tri = tri_ref[...]
    return _dot(tri, hi) + _dot(tri, lo)


def _decayed_keys(k, bcum, backward):
    kd, dec = [], []
    for c in range(N_CHUNK):
        r0 = c * CHUNK_B
        last = r0 if backward else r0 + CHUNK_B - 1
        b_last = bcum[last:last + 1, :]
        kd.append(k[r0:r0 + CHUNK_B, :] * jnp.exp(b_last - bcum[r0:r0 + CHUNK_B, :]))
        dec.append(jnp.exp(b_last))
    return jnp.concatenate(kd, axis=0), dec


def _chunk_kv(kt, v_par, c):
    cp, par = divmod(c, 2)
    parts = []
    for p in range(N_PAIR):
        full = _dot(kt[p * LANES:(p + 1) * LANES, cp * LANES:(cp + 1) * LANES],
                    v_par[cp][par][:, p * 2 * HEAD_V:(p + 1) * 2 * HEAD_V])
        parts.append(full[0:HEAD_K, 0:HEAD_V])
        parts.append(full[HEAD_K:2 * HEAD_K, HEAD_V:2 * HEAD_V])
    return jnp.concatenate(parts, axis=0)


def _decay_column(dec_row):
    return jnp.transpose(jnp.broadcast_to(dec_row, (LANES, D_K)))


def _parity_masked_values(vb):
    out = []
    row = lax.broadcasted_iota(jnp.int32, (2 * CHUNK_B, D_B), 0)
    for cp in range(N_CHUNK // 2):
        vp = vb[cp * 2 * CHUNK_B:(cp + 1) * 2 * CHUNK_B, :]
        zero = jnp.zeros_like(vp)
        out.append((jnp.where(row < CHUNK_B, vp, zero), jnp.where(row >= CHUNK_B, vp, zero)))
    return out


def _projection_pieces(x_ref, npre_ref, w_ref, p_ref, sinks=()):
    x = x_ref[0]
    h = (x * _rms_scale(x) * npre_ref[...]).astype(BF16)

    def piece(c0):
        def run():
            block = _dot(h, w_ref[:, c0:c0 + PIECE_COLS])
            p_ref[:, c0:c0 + PIECE_COLS] = block
            if c0 // PIECE_COLS < len(sinks):
                sinks[c0 // PIECE_COLS](block)
        return run

    return [piece(c0) for c0 in range(0, p_ref.shape[1], PIECE_COLS)]


def _emitter(pieces):
    def emit(n):
        for _ in range(min(n, len(pieces))):
            pieces.pop(0)()
    return emit


def _bwd_state_kernel(n_tiles, n_total, xn_ref, npre_ref, wkv_ref, wgk_ref, bgk_ref, utri_ref,
                      hb_ref, k_ref, v_ref, lr_ref, p0_ref, p1_ref, s_ref):
    i = pl.program_id(0)

    @pl.when(i == 0)
    def _():
        p1_ref[...] = jnp.zeros_like(p1_ref)

    @pl.when(jnp.logical_or(i == 0, (n_total - i) % n_tiles == n_tiles - 1))
    def _():
        s_ref[...] = jnp.zeros_like(s_ref)

    def k_sink(block):
        k_ref[0] = block

    def v_sink(half):
        def sink(block):
            v_ref[0, :, half * PIECE_COLS:(half + 1) * PIECE_COLS] = block.astype(BF16)
        return sink

    def lr_sink(block):
        lr_ref[0] = block[:, 0:LANES]

    def step(p_write_ref, p_ref):
        emit = _emitter(_projection_pieces(xn_ref, npre_ref, wkv_ref, p_write_ref,
                                           sinks=(k_sink, v_sink(0), v_sink(1), lr_sink)))
        emit(1)
        plr = p_ref[:, D_K + D_B:D_K + D_B + LANES]
        g = _log_sigmoid(_dot(plr.astype(BF16), wgk_ref[...]) + bgk_ref[...]) * (
            1.0 / GATE_NORMALIZER)
        emit(1)
        bcum = _chunk_cumsum(utri_ref, g)
        kdec, dec = _decayed_keys(p_ref[:, 0:D_K], bcum, backward=True)
        kt = jnp.transpose(kdec).astype(BF16)
        emit(1)
        v_par = _parity_masked_values(p_ref[:, D_K:D_K + D_B].astype(BF16))
        kv = [_chunk_kv(kt, v_par, c) for c in range(N_CHUNK)]
        dec_col = [_decay_column(dec[c]) for c in range(N_CHUNK)]
        emit(1)
        s = s_ref[...]
        for c in reversed(range(N_CHUNK)):
            hb_ref[0, c] = s
            s = dec_col[c] * s + kv[c]
        s_ref[...] = s

    @pl.when(i % 2 == 0)
    def _():
        step(p0_ref, p1_ref)

    @pl.when(i % 2 == 1)
    def _():
        step(p1_ref, p0_ref)


def _mix(p_ref, x_ref, hb_ref, k_ref, v_ref, lr_ref, wsp_ref, bsp_ref, gva_ref, wgk_ref, bgk_ref,
         ltri_ref, utri_ref, gnb_ref, wout_ref, npost_ref, out_ref, sf_ref, o_ref, mix_ref, emit):
    emit(2)
    vg = _gelu(p_ref[:, M_V:M_ZA])
    vc = vg - jnp.mean(vg, axis=-1, keepdims=True)
    vn = (vc * _rms_scale(vc) * gva_ref[...]).astype(BF16)
    for c in range(TILE // CHUNK_A):
        r0 = c * CHUNK_A
        for hd in range(N_HEADS_A):
            c0 = hd * HEAD_A
            sv = _dot(wsp_ref[hd], vn[r0:r0 + CHUNK_A, c0:c0 + HEAD_A]) + bsp_ref[hd]
            out_a = _gelu(p_ref[r0:r0 + CHUNK_A, M_U + c0:M_U + c0 + HEAD_A]) * sv
            out_a = out_a * _silu(p_ref[r0:r0 + CHUNK_A, M_ZA + c0:M_ZA + c0 + HEAD_A])
            mix_ref[r0:r0 + CHUNK_A, c0:c0 + HEAD_A] = out_a.astype(BF16)
            emit(1 if hd == 1 else 0)

    q = p_ref[:, M_Q:M_ZB] * (HEAD_K ** -0.5)
    k = k_ref[0]
    vb = v_ref[0]
    plr = lr_ref[0]
    emit(1)
    g = _log_sigmoid(_dot(plr.astype(BF16), wgk_ref[...]) + bgk_ref[...]) * (1.0 / GATE_NORMALIZER)
    v_par = _parity_masked_values(vb)

    lane_head = lax.broadcasted_iota(jnp.int32, (CHUNK_B, D_K), 1) // HEAD_K
    lane_pos = lax.broadcasted_iota(jnp.int32, (CHUNK_B, D_K), 1) % CHUNK_B
    row_pos = lax.broadcasted_iota(jnp.int32, (CHUNK_B, D_K), 0)
    zero_blk = jnp.zeros((CHUNK_B, HEAD_V), BF16)

    dirs = []
    for d in range(2):
        backward = d == 1
        emit(1)
        bcum = _chunk_cumsum(utri_ref if backward else ltri_ref, g[:, d * D_K:(d + 1) * D_K])
        q_in = (q * jnp.exp(bcum)).astype(BF16)
        k_in = (k * jnp.exp(-bcum)).astype(BF16)
        keep = (row_pos < lane_pos) if backward else (row_pos >= lane_pos)
        dirs.append((q_in, k_in, keep))
        if not backward:
            kdec, dec = _decayed_keys(k, bcum, backward)
            kt = jnp.transpose(kdec).astype(BF16)

    steps = [(t, d, N_CHUNK - 1 - t if d == 1 else t) for t in range(N_CHUNK) for d in range(2)]
    p_blk, kv_blk, dec_col = {}, {}, {}
    for t, d, c in steps:
        q_in, k_in, keep = dirs[d]
        k_c = k_in[c * CHUNK_B:(c + 1) * CHUNK_B, :]
        k_bd = jnp.concatenate(
            [jnp.where(lane_head == hd, k_c, jnp.zeros_like(k_c)) for hd in range(N_HEADS_B)],
            axis=0)
        scores = _dot_nt(q_in[c * CHUNK_B:(c + 1) * CHUNK_B, :], k_bd)
        p_blk[d, c] = jnp.where(keep, scores, 0.0).astype(BF16)
    for c in range(N_CHUNK):
        kv_blk[c] = _chunk_kv(kt, v_par, c)
        dec_col[c] = _decay_column(dec[c])

    for t, d, c in steps:
        q_in = dirs[d][0]
        r0 = c * CHUNK_B
        s_old = hb_ref[0, c] if d == 1 else sf_ref[...]
        s_bf = s_old.astype(BF16)
        for p in range(N_PAIR):
            a0 = 2 * p * HEAD_V
            rhs = jnp.concatenate([
                jnp.concatenate([vb[r0:r0 + CHUNK_B, a0:a0 + HEAD_V], zero_blk], axis=1),
                jnp.concatenate([zero_blk, vb[r0:r0 + CHUNK_B, a0 + HEAD_V:a0 + 2 * HEAD_V]], axis=1),
                jnp.concatenate([s_bf[2 * p * HEAD_K:(2 * p + 1) * HEAD_K, :], zero_blk], axis=1),
                jnp.concatenate([zero_blk, s_bf[(2 * p + 1) * HEAD_K:(2 * p + 2) * HEAD_K, :]], axis=1),
            ], axis=0)
            lhs = jnp.concatenate([p_blk[d, c][:, p * LANES:(p + 1) * LANES],
                                   q_in[r0:r0 + CHUNK_B, p * LANES:(p + 1) * LANES]], axis=1)
            o_pair = _dot(lhs, rhs)
            if t < N_CHUNK // 2:
                o_ref[r0:r0 + CHUNK_B, a0:a0 + 2 * HEAD_V] = o_pair
            else:
                o_ref[r0:r0 + CHUNK_B, a0:a0 + 2 * HEAD_V] += o_pair
        if d == 0:
            sf_ref[...] = dec_col[c] * s_old + kv_blk[c]

    emit(1)
    for hd in range(N_HEADS_B):
        c0 = hd * HEAD_V
        oh = o_ref[:, c0:c0 + HEAD_V]
        out_b = oh * _rms_scale(oh) * gnb_ref[...] * _silu(p_ref[:, M_ZB + c0:M_ZB + c0 + HEAD_V])
        mix_ref[:, D_A + c0:D_A + c0 + HEAD_V] = out_b.astype(BF16)

    mixed = jnp.concatenate(
        [_dot(mix_ref[...], wout_ref[:, c0:c0 + PIECE_COLS]) for c0 in range(0, D_MODEL, PIECE_COLS)],
        axis=1)
    emit(P_COLS // PIECE_COLS)
    out_ref[0] = x_ref[0] + mixed * _rms_scale(mixed) * npost_ref[...]


def _layer_kernel(n_tiles, xn_ref, xc_ref, hb_ref, k_ref, v_ref, lr_ref, npre_ref, win_ref,
                  wsp_ref, bsp_ref, gva_ref, wgk_ref, bgk_ref, ltri_ref, utri_ref, gnb_ref,
                  wout_ref, npost_ref, out_ref, p0_ref, p1_ref, sf_ref, o_ref, mix_ref):
    i = pl.program_id(0)

    @pl.when(i == 0)
    def _():
        p1_ref[...] = jnp.zeros_like(p1_ref)

    @pl.when(jnp.logical_or(i == 0, (i - 1) % n_tiles == 0))
    def _():
        sf_ref[...] = jnp.zeros_like(sf_ref)

    def step(p_write_ref, p_read_ref):
        emit = _emitter(_projection_pieces(xn_ref, npre_ref, win_ref, p_write_ref))
        _mix(p_read_ref, xc_ref, hb_ref, k_ref, v_ref, lr_ref, wsp_ref, bsp_ref, gva_ref, wgk_ref,
             bgk_ref, ltri_ref, utri_ref, gnb_ref, wout_ref, npost_ref, out_ref, sf_ref, o_ref,
             mix_ref, emit)

    @pl.when(i % 2 == 0)
    def _():
        step(p0_ref, p1_ref)

    @pl.when(i % 2 == 1)
    def _():
        step(p1_ref, p0_ref)


def _const_spec(shape):
    return pl.BlockSpec(shape, lambda *_: (0,) * len(shape))


def _hybrid_layer(x, wts):
    bsz, seq, _ = x.shape
    n_tiles = seq // TILE
    n_total = bsz * n_tiles

    x_tiles = x.reshape(n_total, TILE, D_MODEL)
    params = pltpu.CompilerParams(dimension_semantics=("arbitrary",), vmem_limit_bytes=VMEM_LIMIT)

    def projected_tile(i):
        return (jnp.maximum(n_total - 1 - i, 0), 0, 0)

    hb, k_tiles, v_tiles, lr_tiles = pl.pallas_call(
        functools.partial(_bwd_state_kernel, n_tiles, n_total),
        grid=(n_total + 1,),
        in_specs=[
            pl.BlockSpec((1, TILE, D_MODEL), projected_tile),
            _const_spec((1, D_MODEL)),
            _const_spec((D_MODEL, PRE_COLS + PIECE_COLS)),
            _const_spec((LANES, D_K)),
            _const_spec((1, D_K)),
            _const_spec((TILE, TILE)),
        ],
        out_specs=[
            pl.BlockSpec((1, N_CHUNK, N_HEADS_B * HEAD_K, HEAD_V),
                         lambda i: (jnp.minimum(n_total - i, n_total - 1), 0, 0, 0)),
            pl.BlockSpec((1, TILE, D_K), projected_tile),
            pl.BlockSpec((1, TILE, D_B), projected_tile),
            pl.BlockSpec((1, TILE, LANES), projected_tile),
        ],
        out_shape=[
            jax.ShapeDtypeStruct((n_total, N_CHUNK, N_HEADS_B * HEAD_K, HEAD_V), F32),
            jax.ShapeDtypeStruct((n_total, TILE, D_K), F32),
            jax.ShapeDtypeStruct((n_total, TILE, D_B), BF16),
            jax.ShapeDtypeStruct((n_total, TILE, LANES), F32),
        ],
        scratch_shapes=[
            pltpu.VMEM((TILE, PRE_COLS), F32),
            pltpu.VMEM((TILE, PRE_COLS), F32),
            pltpu.VMEM((N_HEADS_B * HEAD_K, HEAD_V), F32),
        ],
        compiler_params=params,
    )(x_tiles, wts["npre"], wts["wkv"], wts["wgk_b"], wts["bgk_b"], wts["utri"])

    def finished_tile(i):
        return (jnp.maximum(i - 1, 0), 0, 0)

    y = pl.pallas_call(
        functools.partial(_layer_kernel, n_tiles),
        grid=(n_total + 1,),
        in_specs=[
            pl.BlockSpec((1, TILE, D_MODEL), lambda i: (jnp.minimum(i, n_total - 1), 0, 0)),
            pl.BlockSpec((1, TILE, D_MODEL), finished_tile),
            pl.BlockSpec((1, N_CHUNK, N_HEADS_B * HEAD_K, HEAD_V),
                         lambda i: (jnp.maximum(i - 1, 0), 0, 0, 0)),
            pl.BlockSpec((1, TILE, D_K), finished_tile),
            pl.BlockSpec((1, TILE, D_B), finished_tile),
            pl.BlockSpec((1, TILE, LANES), finished_tile),
            _const_spec((1, D_MODEL)),
            _const_spec((D_MODEL, P_COLS)),
            _const_spec((N_HEADS_A, CHUNK_A, CHUNK_A)),
            _const_spec((N_HEADS_A, CHUNK_A, HEAD_A)),
            _const_spec((1, D_A)),
            _const_spec((LANES, 2 * D_K)),
            _const_spec((1, 2 * D_K)),
            _const_spec((TILE, TILE)),
            _const_spec((TILE, TILE)),
            _const_spec((1, HEAD_V)),
            _const_spec((D_A + D_B, D_MODEL + PIECE_COLS)),
            _const_spec((1, D_MODEL)),
        ],
        out_specs=pl.BlockSpec((1, TILE, D_MODEL), finished_tile),
        out_shape=jax.ShapeDtypeStruct((n_total, TILE, D_MODEL), x.dtype),
        scratch_shapes=[
            pltpu.VMEM((TILE, P_COLS), F32),
            pltpu.VMEM((TILE, P_COLS), F32),
            pltpu.VMEM((N_HEADS_B * HEAD_K, HEAD_V), F32),
            pltpu.VMEM((TILE, D_B), F32),
            pltpu.VMEM((TILE, D_A + D_B), BF16),
        ],
        compiler_params=params,
    )(x_tiles, x_tiles, hb, k_tiles, v_tiles, lr_tiles, wts["npre"], wts["win"], wts["wsp"],
      wts["bsp"], wts["gva"], wts["wgk"], wts["bgk"], wts["ltri"], wts["utri"], wts["gnb"],
      wts["wout"], wts["npost"])
    return y.reshape(x.shape)


def _prepare_weights(norm_pre, w_in, w_sp, b_sp, g_v_a, w_gk_fwd, b_gk_fwd, w_gk_bwd, b_gk_bwd,
                     g_norm_b, w_out, norm_post):
    w_in_bf = w_in.astype(BF16)
    wlr = jnp.zeros((D_MODEL, PIECE_COLS), BF16).at[:, 0:2 * GATE_RANK].set(w_in_bf[:, OFF_LRF:D_IN])
    wgk = jnp.zeros((LANES, 2 * D_K), BF16)
    wgk = wgk.at[0:GATE_RANK, 0:D_K].set(w_gk_fwd.astype(BF16))
    wgk = wgk.at[GATE_RANK:2 * GATE_RANK, D_K:2 * D_K].set(w_gk_bwd.astype(BF16))
    pad_cols = jnp.zeros((D_MODEL, PIECE_COLS), BF16)
    pos = jnp.arange(TILE)
    same_chunk = (pos[:, None] // CHUNK_B) == (pos[None, :] // CHUNK_B)
    ltri = (same_chunk & (pos[None, :] <= pos[:, None])).astype(BF16)
    utri = (same_chunk & (pos[None, :] >= pos[:, None])).astype(BF16)
    return {
        "npre": norm_pre.reshape(1, D_MODEL),
        "win": jnp.concatenate([w_in_bf[:, OFF_U:OFF_K], w_in_bf[:, OFF_ZB:OFF_LRF]], axis=1),
        "wkv": jnp.concatenate([w_in_bf[:, OFF_K:OFF_ZB], wlr, pad_cols], axis=1),
        "wsp": w_sp.astype(BF16),
        "bsp": jnp.broadcast_to(b_sp[:, :, None], (N_HEADS_A, CHUNK_A, HEAD_A)),
        "gva": g_v_a.reshape(1, D_A),
        "wgk": wgk,
        "wgk_b": wgk[:, D_K:2 * D_K],
        "bgk": jnp.concatenate([b_gk_fwd, b_gk_bwd]).reshape(1, 2 * D_K),
        "bgk_b": b_gk_bwd.reshape(1, D_K),
        "ltri": ltri,
        "utri": utri,
        "gnb": g_norm_b.reshape(1, HEAD_V),
        "wout": jnp.concatenate([w_out.astype(BF16), pad_cols], axis=1),
        "npost": norm_post.reshape(1, D_MODEL),
    }


def kernel(x_prompt, x_sample, norm_pre, w_in, w_sp, b_sp, g_v_a, w_gk_fwd, b_gk_fwd,
           w_gk_bwd, b_gk_bwd, g_norm_b, w_out, norm_post):
    y_prompt, y_sample = x_prompt, x_sample
    for l in range(norm_pre.shape[0]):
        wts = _prepare_weights(norm_pre[l], w_in[l], w_sp[l], b_sp[l], g_v_a[l], w_gk_fwd[l],
                               b_gk_fwd[l], w_gk_bwd[l], b_gk_bwd[l], g_norm_b[l], w_out[l],
                               norm_post[l])
        y_prompt = _hybrid_layer(y_prompt, wts)
        y_sample = _hybrid_layer(y_sample, wts)
    return (y_prompt, y_sample)
```

```python
import functools

import jax
import jax.numpy as jnp
from jax import lax
from jax.experimental import pallas as pl
from jax.experimental.pallas import tpu as pltpu

F32 = jnp.float32
BF16 = jnp.bfloat16

D_MODEL = 1024
D_A = 512
N_HEADS_A = 4
HEAD_A = 128
CHUNK_A = 128
D_B = 512
N_HEADS_B = 4
HEAD_V = 128
D_K = 256
HEAD_K = 64
GATE_RANK = 16
GATE_NORMALIZER = 16.0
CHUNK_B = 64
EPS = 1e-6

OFF_U, OFF_V, OFF_ZA, OFF_Q, OFF_K, OFF_VB, OFF_ZB, OFF_LRF, OFF_LRB, D_IN = (
    0, 512, 1024, 1536, 1792, 2048, 2560, 3072, 3088, 3104)

LANES = 128
TILE = 256
N_CHUNK = TILE // CHUNK_B
N_PAIR = N_HEADS_B // 2
PIECE_COLS = 256
PRE_COLS = D_K + D_B + PIECE_COLS
M_U, M_V, M_ZA, M_Q, M_ZB, P_COLS = 0, 512, 1024, 1536, 1792, 2304
VMEM_LIMIT = 56 * 1024 * 1024


def _dot(a, b):
    return jnp.dot(a, b, preferred_element_type=F32)


def _dot_nt(a, b):
    return lax.dot_general(a, b, (((1,), (1,)), ((), ())), preferred_element_type=F32)


def _gelu(x):
    return 0.5 * x * (1.0 + jnp.tanh(0.7978845608028654 * (x + 0.044715 * (x * x * x))))


def _silu(x):
    return x * (0.5 * jnp.tanh(0.5 * x) + 0.5)


def _log_sigmoid(x):
    return jnp.minimum(x, 0.0) - jnp.log1p(jnp.exp(-jnp.abs(x)))


def _rms_scale(x):
    return lax.rsqrt(jnp.mean(x * x, axis=-1, keepdims=True) + EPS)


def _chunk_cumsum(tri_ref, g):
    hi = g.astype(BF16)
    lo = (g - hi.astype(F32)).astype(BF16)
    tri = tri_ref[...]
    return _dot(tri, hi) + _dot(tri, lo)


def _decayed_keys(k, bcum, backward):
    kd, dec = [], []
    for c in range(N_CHUNK):
        r0 = c * CHUNK_B
        last = r0 if backward else r0 + CHUNK_B - 1
        b_last = bcum[last:last + 1, :]
        kd.append(k[r0:r0 + CHUNK_B, :] * jnp.exp(b_last - bcum[r0:r0 + CHUNK_B, :]))
        dec.append(jnp.exp(b_last))
    return jnp.concatenate(kd, axis=0), dec


def _chunk_kv(kt, v_par, c):
    cp, par = divmod(c, 2)
    parts = []
    for p in range(N_PAIR):
        full = _dot(kt[p * LANES:(p + 1) * LANES, cp * LANES:(cp + 1) * LANES],
                    v_par[cp][par][:, p * 2 * HEAD_V:(p + 1) * 2 * HEAD_V])
        parts.append(full[0:HEAD_K, 0:HEAD_V])
        parts.append(full[HEAD_K:2 * HEAD_K, HEAD_V:2 * HEAD_V])
    return jnp.concatenate(parts, axis=0)


def _decay_column(dec_row):
    return jnp.transpose(jnp.broadcast_to(dec_row, (LANES, D_K)))


def _parity_masked_values(vb):
    out = []
    row = lax.broadcasted_iota(jnp.int32, (2 * CHUNK_B, D_B), 0)
    for cp in range(N_CHUNK // 2):
        vp = vb[cp * 2 * CHUNK_B:(cp + 1) * 2 * CHUNK_B, :]
        zero = jnp.zeros_like(vp)
        out.append((jnp.where(row < CHUNK_B, vp, zero), jnp.where(row >= CHUNK_B, vp, zero)))
    return out


def _projection_pieces(x_ref, npre_ref, w_ref, p_ref):
    x = x_ref[0]
    h = (x * _rms_scale(x) * npre_ref[...]).astype(BF16)

    def piece(c0):
        def run():
            p_ref[:, c0:c0 + PIECE_COLS] = _dot(h, w_ref[:, c0:c0 + PIECE_COLS])
        return run

    return [piece(c0) for c0 in range(0, p_ref.shape[1], PIECE_COLS)]


def _emitter(pieces):
    def emit(n):
        for _ in range(min(n, len(pieces))):
            pieces.pop(0)()
    return emit


def _bwd_state_kernel(n_tiles, n_total, xn_ref, npre_ref, wkv_ref, wgk_ref, bgk_ref, utri_ref,
                      klr_ref, vs_ref, p0_ref, p1_ref, s_ref):
    i = pl.program_id(0)

    @pl.when(i == 0)
    def _():
        p1_ref[...] = jnp.zeros_like(p1_ref)

    @pl.when(jnp.logical_or(i == 0, (n_total - i) % n_tiles == n_tiles - 1))
    def _():
        s_ref[...] = jnp.zeros_like(s_ref)

    def step(p_write_ref, p_ref):
        emit = _emitter(_projection_pieces(xn_ref, npre_ref, wkv_ref, p_write_ref))
        emit(1)
        k = p_ref[:, 0:D_K]
        vb = p_ref[:, D_K:D_K + D_B].astype(BF16)
        plr = p_ref[:, D_K + D_B:D_K + D_B + LANES]
        klr_ref[0, :, 0:D_K] = k
        klr_ref[0, :, D_K:D_K + LANES] = plr
        vs_ref[0, 0:TILE, :] = vb
        g = _log_sigmoid(_dot(plr.astype(BF16), wgk_ref[...]) + bgk_ref[...]) * (
            1.0 / GATE_NORMALIZER)
        emit(1)
        bcum = _chunk_cumsum(utri_ref, g)
        kdec, dec = _decayed_keys(k, bcum, backward=True)
        kt = jnp.transpose(kdec).astype(BF16)
        emit(1)
        v_par = _parity_masked_values(vb)
        kv = [_chunk_kv(kt, v_par, c) for c in range(N_CHUNK)]
        dec_col = [_decay_column(dec[c]) for c in range(N_CHUNK)]
        emit(1)
        s = s_ref[...]
        for c in reversed(range(N_CHUNK)):
            vs_ref[0, TILE:TILE + N_HEADS_B * HEAD_K, c * HEAD_V:(c + 1) * HEAD_V] = s.astype(BF16)
            s = dec_col[c] * s + kv[c]
        s_ref[...] = s

    @pl.when(i % 2 == 0)
    def _():
        step(p0_ref, p1_ref)

    @pl.when(i % 2 == 1)
    def _():
        step(p1_ref, p0_ref)


def _mix(p_ref, x_ref, klr_ref, vs_ref, wsp_ref, bsp_ref, gva_ref, wgk_ref, bgk_ref,
         ltri_ref, utri_ref, gnb_ref, wout_ref, npost_ref, out_ref, sf_ref, o_ref, mix_ref, emit):
    emit(2)
    vg = _gelu(p_ref[:, M_V:M_ZA])
    vc = vg - jnp.mean(vg, axis=-1, keepdims=True)
    vn = (vc * _rms_scale(vc) * gva_ref[...]).astype(BF16)
    for c in range(TILE // CHUNK_A):
        r0 = c * CHUNK_A
        for hd in range(N_HEADS_A):
            c0 = hd * HEAD_A
            sv = _dot(wsp_ref[hd], vn[r0:r0 + CHUNK_A, c0:c0 + HEAD_A]) + bsp_ref[hd]
            out_a = _gelu(p_ref[r0:r0 + CHUNK_A, M_U + c0:M_U + c0 + HEAD_A]) * sv
            out_a = out_a * _silu(p_ref[r0:r0 + CHUNK_A, M_ZA + c0:M_ZA + c0 + HEAD_A])
            mix_ref[r0:r0 + CHUNK_A, c0:c0 + HEAD_A] = out_a.astype(BF16)
            emit(1 if hd == 1 else 0)

    q = p_ref[:, M_Q:M_ZB] * (HEAD_K ** -0.5)
    k = klr_ref[0, :, 0:D_K]
    vb = vs_ref[0, 0:TILE, :]
    plr = klr_ref[0, :, D_K:D_K + LANES]
    emit(1)
    g = _log_sigmoid(_dot(plr.astype(BF16), wgk_ref[...]) + bgk_ref[...]) * (1.0 / GATE_NORMALIZER)
    v_par = _parity_masked_values(vb)

    lane_head = lax.broadcasted_iota(jnp.int32, (CHUNK_B, D_K), 1) // HEAD_K
    lane_pos = lax.broadcasted_iota(jnp.int32, (CHUNK_B, D_K), 1) % CHUNK_B
    row_pos = lax.broadcasted_iota(jnp.int32, (CHUNK_B, D_K), 0)
    zero_blk = jnp.zeros((CHUNK_B, HEAD_V), BF16)

    dirs = []
    for d in range(2):
        backward = d == 1
        emit(1)
        bcum = _chunk_cumsum(utri_ref if backward else ltri_ref, g[:, d * D_K:(d + 1) * D_K])
        q_in = (q * jnp.exp(bcum)).astype(BF16)
        k_in = (k * jnp.exp(-bcum)).astype(BF16)
        keep = (row_pos < lane_pos) if backward else (row_pos >= lane_pos)
        dirs.append((q_in, k_in, keep))
        if not backward:
            kdec, dec = _decayed_keys(k, bcum, backward)
            kt = jnp.transpose(kdec).astype(BF16)

    steps = [(t, d, N_CHUNK - 1 - t if d == 1 else t) for t in range(N_CHUNK) for d in range(2)]
    p_blk, kv_blk, dec_col = {}, {}, {}
    for t, d, c in steps:
        q_in, k_in, keep = dirs[d]
        k_c = k_in[c * CHUNK_B:(c + 1) * CHUNK_B, :]
        k_bd = jnp.concatenate(
            [jnp.where(lane_head == hd, k_c, jnp.zeros_like(k_c)) for hd in range(N_HEADS_B)],
            axis=0)
        scores = _dot_nt(q_in[c * CHUNK_B:(c + 1) * CHUNK_B, :], k_bd)
        p_blk[d, c] = jnp.where(keep, scores, 0.0).astype(BF16)
    for c in range(N_CHUNK):
        kv_blk[c] = _chunk_kv(kt, v_par, c)
        dec_col[c] = _decay_column(dec[c])

    for t, d, c in steps:
        q_in = dirs[d][0]
        r0 = c * CHUNK_B
        if d == 1:
            s_bf = vs_ref[0, TILE:TILE + N_HEADS_B * HEAD_K, c * HEAD_V:(c + 1) * HEAD_V]
        else:
            s_old = sf_ref[...]
            s_bf = s_old.astype(BF16)
        for p in range(N_PAIR):
            a0 = 2 * p * HEAD_V
            rhs = jnp.concatenate([
                jnp.concatenate([vb[r0:r0 + CHUNK_B, a0:a0 + HEAD_V], zero_blk], axis=1),
                jnp.concatenate([zero_blk, vb[r0:r0 + CHUNK_B, a0 + HEAD_V:a0 + 2 * HEAD_V]], axis=1),
                jnp.concatenate([s_bf[2 * p * HEAD_K:(2 * p + 1) * HEAD_K, :], zero_blk], axis=1),
                jnp.concatenate([zero_blk, s_bf[(2 * p + 1) * HEAD_K:(2 * p + 2) * HEAD_K, :]], axis=1),
            ], axis=0)
            lhs = jnp.concatenate([p_blk[d, c][:, p * LANES:(p + 1) * LANES],
                                   q_in[r0:r0 + CHUNK_B, p * LANES:(p + 1) * LANES]], axis=1)
            o_pair = _dot(lhs, rhs)
            if t < N_CHUNK // 2:
                o_ref[r0:r0 + CHUNK_B, a0:a0 + 2 * HEAD_V] = o_pair
            else:
                o_ref[r0:r0 + CHUNK_B, a0:a0 + 2 * HEAD_V] += o_pair
        if d == 0:
            sf_ref[...] = dec_col[c] * s_old + kv_blk[c]

    emit(1)
    for hd in range(N_HEADS_B):
        c0 = hd * HEAD_V
        oh = o_ref[:, c0:c0 + HEAD_V]
        out_b = oh * _rms_scale(oh) * gnb_ref[...] * _silu(p_ref[:, M_ZB + c0:M_ZB + c0 + HEAD_V])
        mix_ref[:, D_A + c0:D_A + c0 + HEAD_V] = out_b.astype(BF16)

    mixed = jnp.concatenate(
        [_dot(mix_ref[...], wout_ref[:, c0:c0 + PIECE_COLS]) for c0 in range(0, D_MODEL, PIECE_COLS)],
        axis=1)
    emit(P_COLS // PIECE_COLS)
    out_ref[0] = x_ref[0] + mixed * _rms_scale(mixed) * npost_ref[...]


def _layer_kernel(n_tiles, xn_ref, xc_ref, klr_ref, vs_ref, npre_ref, win_ref,
                  wsp_ref, bsp_ref, gva_ref, wgk_ref, bgk_ref, ltri_ref, utri_ref, gnb_ref,
                  wout_ref, npost_ref, out_ref, p0_ref, p1_ref, sf_ref, o_ref, mix_ref):
    i = pl.program_id(0)

    @pl.when(i == 0)
    def _():
        p1_ref[...] = jnp.zeros_like(p1_ref)

    @pl.when(jnp.logical_or(i == 0, (i - 1) % n_tiles == 0))
    def _():
        sf_ref[...] = jnp.zeros_like(sf_ref)

    def step(p_write_ref, p_read_ref):
        emit = _emitter(_projection_pieces(xn_ref, npre_ref, win_ref, p_write_ref))
        _mix(p_read_ref, xc_ref, klr_ref, vs_ref, wsp_ref, bsp_ref, gva_ref, wgk_ref,
             bgk_ref, ltri_ref, utri_ref, gnb_ref, wout_ref, npost_ref, out_ref, sf_ref, o_ref,
             mix_ref, emit)

    @pl.when(i % 2 == 0)
    def _():
        step(p0_ref, p1_ref)

    @pl.when(i % 2 == 1)
    def _():
        step(p1_ref, p0_ref)


def _const_spec(shape):
    return pl.BlockSpec(shape, lambda *_: (0,) * len(shape))


def _hybrid_layer(x, wts):
    bsz, seq, _ = x.shape
    n_tiles = seq // TILE
    n_total = bsz * n_tiles

    x_tiles = x.reshape(n_total, TILE, D_MODEL)
    params = pltpu.CompilerParams(dimension_semantics=("arbitrary",), vmem_limit_bytes=VMEM_LIMIT)

    def projected_tile(i):
        return (jnp.maximum(n_total - 1 - i, 0), 0, 0)

    def recurred_tile(i):
        return (jnp.minimum(n_total - i, n_total - 1), 0, 0)

    klr_tiles, vs_tiles = pl.pallas_call(
        functools.partial(_bwd_state_kernel, n_tiles, n_total),
        grid=(n_total + 1,),
        in_specs=[
            pl.BlockSpec((1, TILE, D_MODEL), projected_tile),
            _const_spec((1, D_MODEL)),
            _const_spec((D_MODEL, PRE_COLS + PIECE_COLS)),
            _const_spec((LANES, D_K)),
            _const_spec((1, D_K)),
            _const_spec((TILE, TILE)),
        ],
        out_specs=[
            pl.BlockSpec((1, TILE, D_K + LANES), recurred_tile),
            pl.BlockSpec((1, TILE + N_HEADS_B * HEAD_K, D_B), recurred_tile),
        ],
        out_shape=[
            jax.ShapeDtypeStruct((n_total, TILE, D_K + LANES), F32),
            jax.ShapeDtypeStruct((n_total, TILE + N_HEADS_B * HEAD_K, D_B), BF16),
        ],
        scratch_shapes=[
            pltpu.VMEM((TILE, PRE_COLS), F32),
            pltpu.VMEM((TILE, PRE_COLS), F32),
            pltpu.VMEM((N_HEADS_B * HEAD_K, HEAD_V), F32),
        ],
        compiler_params=params,
    )(x_tiles, wts["npre"], wts["wkv"], wts["wgk_b"], wts["bgk_b"], wts["utri"])

    def finished_tile(i):
        return (jnp.maximum(i - 1, 0), 0, 0)

    y = pl.pallas_call(
        functools.partial(_layer_kernel, n_tiles),
        grid=(n_total + 1,),
        in_specs=[
            pl.BlockSpec((1, TILE, D_MODEL), lambda i: (jnp.minimum(i, n_total - 1), 0, 0)),
            pl.BlockSpec((1, TILE, D_MODEL), finished_tile),
            pl.BlockSpec((1, TILE, D_K + LANES), finished_tile),
            pl.BlockSpec((1, TILE + N_HEADS_B * HEAD_K, D_B), finished_tile),
            _const_spec((1, D_MODEL)),
            _const_spec((D_MODEL, P_COLS)),
            _const_spec((N_HEADS_A, CHUNK_A, CHUNK_A)),
            _const_spec((N_HEADS_A, CHUNK_A, HEAD_A)),
            _const_spec((1, D_A)),
            _const_spec((LANES, 2 * D_K)),
            _const_spec((1, 2 * D_K)),
            _const_spec((TILE, TILE)),
            _const_spec((TILE, TILE)),
            _const_spec((1, HEAD_V)),
            _const_spec((D_A + D_B, D_MODEL + PIECE_COLS)),
            _const_spec((1, D_MODEL)),
        ],
        out_specs=pl.BlockSpec((1, TILE, D_MODEL), finished_tile),
        out_shape=jax.ShapeDtypeStruct((n_total, TILE, D_MODEL), x.dtype),
        scratch_shapes=[
            pltpu.VMEM((TILE, P_COLS), F32),
            pltpu.VMEM((TILE, P_COLS), F32),
            pltpu.VMEM((N_HEADS_B * HEAD_K, HEAD_V), F32),
            pltpu.VMEM((TILE, D_B), F32),
            pltpu.VMEM((TILE, D_A + D_B), BF16),
        ],
        compiler_params=params,
    )(x_tiles, x_tiles, klr_tiles, vs_tiles, wts["npre"], wts["win"], wts["wsp"],
      wts["bsp"], wts["gva"], wts["wgk"], wts["bgk"], wts["ltri"], wts["utri"], wts["gnb"],
      wts["wout"], wts["npost"])
    return y.reshape(x.shape)


def _prepare_weights(norm_pre, w_in, w_sp, b_sp, g_v_a, w_gk_fwd, b_gk_fwd, w_gk_bwd, b_gk_bwd,
                     g_norm_b, w_out, norm_post):
    w_in_bf = w_in.astype(BF16)
    wlr = jnp.zeros((D_MODEL, PIECE_COLS), BF16).at[:, 0:2 * GATE_RANK].set(w_in_bf[:, OFF_LRF:D_IN])
    wgk = jnp.zeros((LANES, 2 * D_K), BF16)
    wgk = wgk.at[0:GATE_RANK, 0:D_K].set(w_gk_fwd.astype(BF16))
    wgk = wgk.at[GATE_RANK:2 * GATE_RANK, D_K:2 * D_K].set(w_gk_bwd.astype(BF16))
    pad_cols = jnp.zeros((D_MODEL, PIECE_COLS), BF16)
    pos = jnp.arange(TILE)
    same_chunk = (pos[:, None] // CHUNK_B) == (pos[None, :] // CHUNK_B)
    ltri = (same_chunk & (pos[None, :] <= pos[:, None])).astype(BF16)
    utri = (same_chunk & (pos[None, :] >= pos[:, None])).astype(BF16)
    return {
        "npre": norm_pre.reshape(1, D_MODEL),
        "win": jnp.concatenate([w_in_bf[:, OFF_U:OFF_K], w_in_bf[:, OFF_ZB:OFF_LRF]], axis=1),
        "wkv": jnp.concatenate([w_in_bf[:, OFF_K:OFF_ZB], wlr, pad_cols], axis=1),
        "wsp": w_sp.astype(BF16),
        "bsp": jnp.broadcast_to(b_sp[:, :, None], (N_HEADS_A, CHUNK_A, HEAD_A)),
        "gva": g_v_a.reshape(1, D_A),
        "wgk": wgk,
        "wgk_b": wgk[:, D_K:2 * D_K],
        "bgk": jnp.concatenate([b_gk_fwd, b_gk_bwd]).reshape(1, 2 * D_K),
        "bgk_b": b_gk_bwd.reshape(1, D_K),
        "ltri": ltri,
        "utri": utri,
        "gnb": g_norm_b.reshape(1, HEAD_V),
        "wout": jnp.concatenate([w_out.astype(BF16), pad_cols], axis=1),
        "npost": norm_post.reshape(1, D_MODEL),
    }


def kernel(x_prompt, x_sample, norm_pre, w_in, w_sp, b_sp, g_v_a, w_gk_fwd, b_gk_fwd,
           w_gk_bwd, b_gk_bwd, g_norm_b, w_out, norm_post):
    y_prompt, y_sample = x_prompt, x_sample
    for l in range(norm_pre.shape[0]):
        wts = _prepare_weights(norm_pre[l], w_in[l], w_sp[l], b_sp[l], g_v_a[l], w_gk_fwd[l],
                               b_gk_fwd[l], w_gk_bwd[l], b_gk_bwd[l], g_norm_b[l], w_out[l],
                               norm_post[l])
        y_prompt = _hybrid_layer(y_prompt, wts)
        y_sample = _hybrid_layer(y_sample, wts)
    return (y_prompt, y_sample)
```

```python
import functools

import jax
import jax.numpy as jnp
from jax import lax
from jax.experimental import pallas as pl
from jax.experimental.pallas import tpu as pltpu

F32 = jnp.float32
BF16 = jnp.bfloat16

D_MODEL = 1024
D_A = 512
N_HEADS_A = 4
HEAD_A = 128
CHUNK_A = 128
D_B = 512
N_HEADS_B = 4
HEAD_V = 128
D_K = 256
HEAD_K = 64
GATE_RANK = 16
GATE_NORMALIZER = 16.0
CHUNK_B = 64
EPS = 1e-6

OFF_U, OFF_V, OFF_ZA, OFF_Q, OFF_K, OFF_VB, OFF_ZB, OFF_LRF, OFF_LRB, D_IN = (
    0, 512, 1024, 1536, 1792, 2048, 2560, 3072, 3088, 3104)

LANES = 128
TILE = 256
N_CHUNK = TILE // CHUNK_B
N_PAIR = N_HEADS_B // 2
PIECE_COLS = 256
PRE_COLS = D_K + D_B + PIECE_COLS
M_U, M_V, M_ZA, M_Q, M_ZB, P_COLS = 0, 512, 1024, 1536, 1792, 2304
VMEM_LIMIT = 56 * 1024 * 1024


def _dot(a, b):
    return jnp.dot(a, b, preferred_element_type=F32)


def _dot_nt(a, b):
    return lax.dot_general(a, b, (((1,), (1,)), ((), ())), preferred_element_type=F32)


def _gelu(x):
    return 0.5 * x * (1.0 + jnp.tanh(0.7978845608028654 * (x + 0.044715 * (x * x * x))))


def _silu(x):
    return x * (0.5 * jnp.tanh(0.5 * x) + 0.5)


def _log_sigmoid(x):
    return jnp.minimum(x, 0.0) - jnp.log1p(jnp.exp(-jnp.abs(x)))


def _rms_scale(x):
    return lax.rsqrt(jnp.mean(x * x, axis=-1, keepdims=True) + EPS)


def _chunk_cumsum(tri_ref, g):
    hi = g.astype(BF16)
    lo = (g - hi.astype(F32)).astype(BF16)
    tri = tri_ref[...]
    return _dot(tri, hi) + _dot(tri, lo)


def _decayed_keys(k, bcum, backward):
    kd, dec = [], []
    for c in range(N_CHUNK):
        r0 = c * CHUNK_B
        last = r0 if backward else r0 + CHUNK_B - 1
        b_last = bcum[last:last + 1, :]
        kd.append(k[r0:r0 + CHUNK_B, :] * jnp.exp(b_last - bcum[r0:r0 + CHUNK_B, :]))
        dec.append(jnp.exp(b_last))
    return jnp.concatenate(kd, axis=0), dec


def _chunk_kv(kt, v_par, c):
    cp, par = divmod(c, 2)
    parts = []
    for p in range(N_PAIR):
        full = _dot(kt[p * LANES:(p + 1) * LANES, cp * LANES:(cp + 1) * LANES],
                    v_par[cp][par][:, p * 2 * HEAD_V:(p + 1) * 2 * HEAD_V])
        parts.append(full[0:HEAD_K, 0:HEAD_V])
        parts.append(full[HEAD_K:2 * HEAD_K, HEAD_V:2 * HEAD_V])
    return jnp.concatenate(parts, axis=0)


def _decay_column(dec_row):
    return jnp.transpose(jnp.broadcast_to(dec_row, (LANES, D_K)))


def _parity_masked_values(vb):
    out = []
    row = lax.broadcasted_iota(jnp.int32, (2 * CHUNK_B, D_B), 0)
    for cp in range(N_CHUNK // 2):
        vp = vb[cp * 2 * CHUNK_B:(cp + 1) * 2 * CHUNK_B, :]
        zero = jnp.zeros_like(vp)
        out.append((jnp.where(row < CHUNK_B, vp, zero), jnp.where(row >= CHUNK_B, vp, zero)))
    return out


def _projection_pieces(x_ref, npre_ref, w_ref, p_ref, x_keep_ref=None):
    x = x_ref[0]
    if x_keep_ref is not None:
        x_keep_ref[...] = x
    h = (x * _rms_scale(x) * npre_ref[...]).astype(BF16)

    def piece(c0):
        def run():
            p_ref[:, c0:c0 + PIECE_COLS] = _dot(h, w_ref[:, c0:c0 + PIECE_COLS])
        return run

    return [piece(c0) for c0 in range(0, p_ref.shape[1], PIECE_COLS)]


def _emitter(pieces):
    def emit(n):
        for _ in range(min(n, len(pieces))):
            pieces.pop(0)()
    return emit


def _bwd_state_kernel(n_tiles, n_total, xn_ref, npre_ref, wkv_ref, wgk_ref, bgk_ref, utri_ref,
                      klr_ref, vs_ref, p0_ref, p1_ref, s_ref):
    i = pl.program_id(0)

    @pl.when(i == 0)
    def _():
        p1_ref[...] = jnp.zeros_like(p1_ref)

    @pl.when(jnp.logical_or(i == 0, (n_total - i) % n_tiles == n_tiles - 1))
    def _():
        s_ref[...] = jnp.zeros_like(s_ref)

    def step(p_write_ref, p_ref):
        emit = _emitter(_projection_pieces(xn_ref, npre_ref, wkv_ref, p_write_ref))
        emit(1)
        k = p_ref[:, 0:D_K]
        vb = p_ref[:, D_K:D_K + D_B].astype(BF16)
        plr = p_ref[:, D_K + D_B:D_K + D_B + LANES]
        klr_ref[0, :, 0:D_K] = k
        klr_ref[0, :, D_K:D_K + LANES] = plr
        vs_ref[0, 0:TILE, :] = vb
        g = _log_sigmoid(_dot(plr.astype(BF16), wgk_ref[...]) + bgk_ref[...]) * (
            1.0 / GATE_NORMALIZER)
        emit(1)
        bcum = _chunk_cumsum(utri_ref, g)
        kdec, dec = _decayed_keys(k, bcum, backward=True)
        kt = jnp.transpose(kdec).astype(BF16)
        emit(1)
        v_par = _parity_masked_values(vb)
        kv = [_chunk_kv(kt, v_par, c) for c in range(N_CHUNK)]
        dec_col = [_decay_column(dec[c]) for c in range(N_CHUNK)]
        emit(1)
        s = s_ref[...]
        for c in reversed(range(N_CHUNK)):
            vs_ref[0, TILE:TILE + N_HEADS_B * HEAD_K, c * HEAD_V:(c + 1) * HEAD_V] = s.astype(BF16)
            s = dec_col[c] * s + kv[c]
        s_ref[...] = s

    @pl.when(i % 2 == 0)
    def _():
        step(p0_ref, p1_ref)

    @pl.when(i % 2 == 1)
    def _():
        step(p1_ref, p0_ref)


def _mix(p_ref, x_ref, klr_ref, vs_ref, wsp_ref, bsp_ref, gva_ref, wgk_ref, bgk_ref,
         ltri_ref, utri_ref, gnb_ref, wout_ref, npost_ref, out_ref, sf_ref, o_ref, mix_ref, emit):
    emit(2)
    vg = _gelu(p_ref[:, M_V:M_ZA])
    vc = vg - jnp.mean(vg, axis=-1, keepdims=True)
    vn = (vc * _rms_scale(vc) * gva_ref[...]).astype(BF16)
    for c in range(TILE // CHUNK_A):
        r0 = c * CHUNK_A
        for hd in range(N_HEADS_A):
            c0 = hd * HEAD_A
            sv = _dot(wsp_ref[hd], vn[r0:r0 + CHUNK_A, c0:c0 + HEAD_A]) + bsp_ref[hd]
            out_a = _gelu(p_ref[r0:r0 + CHUNK_A, M_U + c0:M_U + c0 + HEAD_A]) * sv
            out_a = out_a * _silu(p_ref[r0:r0 + CHUNK_A, M_ZA + c0:M_ZA + c0 + HEAD_A])
            mix_ref[r0:r0 + CHUNK_A, c0:c0 + HEAD_A] = out_a.astype(BF16)
            emit(1 if hd == 1 else 0)

    q = p_ref[:, M_Q:M_ZB] * (HEAD_K ** -0.5)
    k = klr_ref[0, :, 0:D_K]
    vb = vs_ref[0, 0:TILE, :]
    plr = klr_ref[0, :, D_K:D_K + LANES]
    emit(1)
    g = _log_sigmoid(_dot(plr.astype(BF16), wgk_ref[...]) + bgk_ref[...]) * (1.0 / GATE_NORMALIZER)
    v_par = _parity_masked_values(vb)

    lane_head = lax.broadcasted_iota(jnp.int32, (CHUNK_B, D_K), 1) // HEAD_K
    lane_pos = lax.broadcasted_iota(jnp.int32, (CHUNK_B, D_K), 1) % CHUNK_B
    row_pos = lax.broadcasted_iota(jnp.int32, (CHUNK_B, D_K), 0)
    zero_blk = jnp.zeros((CHUNK_B, HEAD_V), BF16)

    dirs = []
    for d in range(2):
        backward = d == 1
        emit(1)
        bcum = _chunk_cumsum(utri_ref if backward else ltri_ref, g[:, d * D_K:(d + 1) * D_K])
        q_in = (q * jnp.exp(bcum)).astype(BF16)
        k_in = (k * jnp.exp(-bcum)).astype(BF16)
        keep = (row_pos < lane_pos) if backward else (row_pos >= lane_pos)
        dirs.append((q_in, k_in, keep))
        if not backward:
            kdec, dec = _decayed_keys(k, bcum, backward)
            kt = jnp.transpose(kdec).astype(BF16)

    steps = [(t, d, N_CHUNK - 1 - t if d == 1 else t) for t in range(N_CHUNK) for d in range(2)]
    p_blk, kv_blk, dec_col = {}, {}, {}
    for t, d, c in steps:
        q_in, k_in, keep = dirs[d]
        k_c = k_in[c * CHUNK_B:(c + 1) * CHUNK_B, :]
        k_bd = jnp.concatenate(
            [jnp.where(lane_head == hd, k_c, jnp.zeros_like(k_c)) for hd in range(N_HEADS_B)],
            axis=0)
        scores = _dot_nt(q_in[c * CHUNK_B:(c + 1) * CHUNK_B, :], k_bd)
        p_blk[d, c] = jnp.where(keep, scores, 0.0).astype(BF16)
    for c in range(N_CHUNK):
        kv_blk[c] = _chunk_kv(kt, v_par, c)
        dec_col[c] = _decay_column(dec[c])

    for t, d, c in steps:
        q_in = dirs[d][0]
        r0 = c * CHUNK_B
        if d == 1:
            s_bf = vs_ref[0, TILE:TILE + N_HEADS_B * HEAD_K, c * HEAD_V:(c + 1) * HEAD_V]
        else:
            s_old = sf_ref[...]
            s_bf = s_old.astype(BF16)
        for p in range(N_PAIR):
            a0 = 2 * p * HEAD_V
            rhs = jnp.concatenate([
                jnp.concatenate([vb[r0:r0 + CHUNK_B, a0:a0 + HEAD_V], zero_blk], axis=1),
                jnp.concatenate([zero_blk, vb[r0:r0 + CHUNK_B, a0 + HEAD_V:a0 + 2 * HEAD_V]], axis=1),
                jnp.concatenate([s_bf[2 * p * HEAD_K:(2 * p + 1) * HEAD_K, :], zero_blk], axis=1),
                jnp.concatenate([zero_blk, s_bf[(2 * p + 1) * HEAD_K:(2 * p + 2) * HEAD_K, :]], axis=1),
            ], axis=0)
            lhs = jnp.concatenate([p_blk[d, c][:, p * LANES:(p + 1) * LANES],
                                   q_in[r0:r0 + CHUNK_B, p * LANES:(p + 1) * LANES]], axis=1)
            o_pair = _dot(lhs, rhs)
            if t < N_CHUNK // 2:
                o_ref[r0:r0 + CHUNK_B, a0:a0 + 2 * HEAD_V] = o_pair
            else:
                o_ref[r0:r0 + CHUNK_B, a0:a0 + 2 * HEAD_V] += o_pair
        if d == 0:
            sf_ref[...] = dec_col[c] * s_old + kv_blk[c]

    emit(1)
    for hd in range(N_HEADS_B):
        c0 = hd * HEAD_V
        oh = o_ref[:, c0:c0 + HEAD_V]
        out_b = oh * _rms_scale(oh) * gnb_ref[...] * _silu(p_ref[:, M_ZB + c0:M_ZB + c0 + HEAD_V])
        mix_ref[:, D_A + c0:D_A + c0 + HEAD_V] = out_b.astype(BF16)

    mixed = jnp.concatenate(
        [_dot(mix_ref[...], wout_ref[:, c0:c0 + PIECE_COLS]) for c0 in range(0, D_MODEL, PIECE_COLS)],
        axis=1)
    emit(P_COLS // PIECE_COLS)
    out_ref[0] = x_ref[...] + mixed * _rms_scale(mixed) * npost_ref[...]


def _layer_kernel(n_tiles, xn_ref, klr_ref, vs_ref, npre_ref, win_ref,
                  wsp_ref, bsp_ref, gva_ref, wgk_ref, bgk_ref, ltri_ref, utri_ref, gnb_ref,
                  wout_ref, npost_ref, out_ref, p0_ref, p1_ref, x0_ref, x1_ref, sf_ref, o_ref, mix_ref):
    i = pl.program_id(0)

    @pl.when(i == 0)
    def _():
        p1_ref[...] = jnp.zeros_like(p1_ref)
        x1_ref[...] = jnp.zeros_like(x1_ref)

    @pl.when(jnp.logical_or(i == 0, (i - 1) % n_tiles == 0))
    def _():
        sf_ref[...] = jnp.zeros_like(sf_ref)

    def step(p_write_ref, p_read_ref, x_write_ref, x_read_ref):
        emit = _emitter(_projection_pieces(xn_ref, npre_ref, win_ref, p_write_ref, x_write_ref))
        _mix(p_read_ref, x_read_ref, klr_ref, vs_ref, wsp_ref, bsp_ref, gva_ref, wgk_ref,
             bgk_ref, ltri_ref, utri_ref, gnb_ref, wout_ref, npost_ref, out_ref, sf_ref, o_ref,
             mix_ref, emit)

    @pl.when(i % 2 == 0)
    def _():
        step(p0_ref, p1_ref, x0_ref, x1_ref)

    @pl.when(i % 2 == 1)
    def _():
        step(p1_ref, p0_ref, x1_ref, x0_ref)


def _const_spec(shape):
    return pl.BlockSpec(shape, lambda *_: (0,) * len(shape))


def _hybrid_layer(x, wts):
    bsz, seq, _ = x.shape
    n_tiles = seq // TILE
    n_total = bsz * n_tiles

    x_tiles = x.reshape(n_total, TILE, D_MODEL)
    params = pltpu.CompilerParams(dimension_semantics=("arbitrary",), vmem_limit_bytes=VMEM_LIMIT)

    def projected_tile(i):
        return (jnp.maximum(n_total - 1 - i, 0), 0, 0)

    def recurred_tile(i):
        return (jnp.minimum(n_total - i, n_total - 1), 0, 0)

    klr_tiles, vs_tiles = pl.pallas_call(
        functools.partial(_bwd_state_kernel, n_tiles, n_total),
        grid=(n_total + 1,),
        in_specs=[
            pl.BlockSpec((1, TILE, D_MODEL), projected_tile),
            _const_spec((1, D_MODEL)),
            _const_spec((D_MODEL, PRE_COLS + PIECE_COLS)),
            _const_spec((LANES, D_K)),
            _const_spec((1, D_K)),
            _const_spec((TILE, TILE)),
        ],
        out_specs=[
            pl.BlockSpec((1, TILE, D_K + LANES), recurred_tile),
            pl.BlockSpec((1, TILE + N_HEADS_B * HEAD_K, D_B), recurred_tile),
        ],
        out_shape=[
            jax.ShapeDtypeStruct((n_total, TILE, D_K + LANES), F32),
            jax.ShapeDtypeStruct((n_total, TILE + N_HEADS_B * HEAD_K, D_B), BF16),
        ],
        scratch_shapes=[
            pltpu.VMEM((TILE, PRE_COLS), F32),
            pltpu.VMEM((TILE, PRE_COLS), F32),
            pltpu.VMEM((N_HEADS_B * HEAD_K, HEAD_V), F32),
        ],
        compiler_params=params,
    )(x_tiles, wts["npre"], wts["wkv"], wts["wgk_b"], wts["bgk_b"], wts["utri"])

    def finished_tile(i):
        return (jnp.maximum(i - 1, 0), 0, 0)

    y = pl.pallas_call(
        functools.partial(_layer_kernel, n_tiles),
        grid=(n_total + 1,),
        in_specs=[
            pl.BlockSpec((1, TILE, D_MODEL), lambda i: (jnp.minimum(i, n_total - 1), 0, 0)),
            pl.BlockSpec((1, TILE, D_K + LANES), finished_tile),
            pl.BlockSpec((1, TILE + N_HEADS_B * HEAD_K, D_B), finished_tile),
            _const_spec((1, D_MODEL)),
            _const_spec((D_MODEL, P_COLS)),
            _const_spec((N_HEADS_A, CHUNK_A, CHUNK_A)),
            _const_spec((N_HEADS_A, CHUNK_A, HEAD_A)),
            _const_spec((1, D_A)),
            _const_spec((LANES, 2 * D_K)),
            _const_spec((1, 2 * D_K)),
            _const_spec((TILE, TILE)),
            _const_spec((TILE, TILE)),
            _const_spec((1, HEAD_V)),
            _const_spec((D_A + D_B, D_MODEL + PIECE_COLS)),
            _const_spec((1, D_MODEL)),
        ],
        out_specs=pl.BlockSpec((1, TILE, D_MODEL), finished_tile),
        out_shape=jax.ShapeDtypeStruct((n_total, TILE, D_MODEL), x.dtype),
        scratch_shapes=[
            pltpu.VMEM((TILE, P_COLS), F32),
            pltpu.VMEM((TILE, P_COLS), F32),
            pltpu.VMEM((TILE, D_MODEL), F32),
            pltpu.VMEM((TILE, D_MODEL), F32),
            pltpu.VMEM((N_HEADS_B * HEAD_K, HEAD_V), F32),
            pltpu.VMEM((TILE, D_B), F32),
            pltpu.VMEM((TILE, D_A + D_B), BF16),
        ],
        compiler_params=params,
    )(x_tiles, klr_tiles, vs_tiles, wts["npre"], wts["win"], wts["wsp"],
      wts["bsp"], wts["gva"], wts["wgk"], wts["bgk"], wts["ltri"], wts["utri"], wts["gnb"],
      wts["wout"], wts["npost"])
    return y.reshape(x.shape)


def _prepare_weights(norm_pre, w_in, w_sp, b_sp, g_v_a, w_gk_fwd, b_gk_fwd, w_gk_bwd, b_gk_bwd,
                     g_norm_b, w_out, norm_post):
    w_in_bf = w_in.astype(BF16)
    wlr = jnp.zeros((D_MODEL, PIECE_COLS), BF16).at[:, 0:2 * GATE_RANK].set(w_in_bf[:, OFF_LRF:D_IN])
    wgk = jnp.zeros((LANES, 2 * D_K), BF16)
    wgk = wgk.at[0:GATE_RANK, 0:D_K].set(w_gk_fwd.astype(BF16))
    wgk = wgk.at[GATE_RANK:2 * GATE_RANK, D_K:2 * D_K].set(w_gk_bwd.astype(BF16))
    pad_cols = jnp.zeros((D_MODEL, PIECE_COLS), BF16)
    pos = jnp.arange(TILE)
    same_chunk = (pos[:, None] // CHUNK_B) == (pos[None, :] // CHUNK_B)
    ltri = (same_chunk & (pos[None, :] <= pos[:, None])).astype(BF16)
    utri = (same_chunk & (pos[None, :] >= pos[:, None])).astype(BF16)
    return {
        "npre": norm_pre.reshape(1, D_MODEL),
        "win": jnp.concatenate([w_in_bf[:, OFF_U:OFF_K], w_in_bf[:, OFF_ZB:OFF_LRF]], axis=1),
        "wkv": jnp.concatenate([w_in_bf[:, OFF_K:OFF_ZB], wlr, pad_cols], axis=1),
        "wsp": w_sp.astype(BF16),
        "bsp": jnp.broadcast_to(b_sp[:, :, None], (N_HEADS_A, CHUNK_A, HEAD_A)),
        "gva": g_v_a.reshape(1, D_A),
        "wgk": wgk,
        "wgk_b": wgk[:, D_K:2 * D_K],
        "bgk": jnp.concatenate([b_gk_fwd, b_gk_bwd]).reshape(1, 2 * D_K),
        "bgk_b": b_gk_bwd.reshape(1, D_K),
        "ltri": ltri,
        "utri": utri,
        "gnb": g_norm_b.reshape(1, HEAD_V),
        "wout": jnp.concatenate([w_out.astype(BF16), pad_cols], axis=1),
        "npost": norm_post.reshape(1, D_MODEL),
    }


def kernel(x_prompt, x_sample, norm_pre, w_in, w_sp, b_sp, g_v_a, w_gk_fwd, b_gk_fwd,
           w_gk_bwd, b_gk_bwd, g_norm_b, w_out, norm_post):
    y_prompt, y_sample = x_prompt, x_sample
    for l in range(norm_pre.shape[0]):
        wts = _prepare_weights(norm_pre[l], w_in[l], w_sp[l], b_sp[l], g_v_a[l], w_gk_fwd[l],
                               b_gk_fwd[l], w_gk_bwd[l], b_gk_bwd[l], g_norm_b[l], w_out[l],
                               norm_post[l])
        y_prompt = _hybrid_layer(y_prompt, wts)
        y_sample = _hybrid_layer(y_sample, wts)
    return (y_prompt, y_sample)
```

```python
import functools

import jax
import jax.numpy as jnp
from jax import lax
from jax.experimental import pallas as pl
from jax.experimental.pallas import tpu as pltpu

F32 = jnp.float32
BF16 = jnp.bfloat16

D_MODEL = 1024
D_A = 512
N_HEADS_A = 4
HEAD_A = 128
CHUNK_A = 128
D_B = 512
N_HEADS_B = 4
HEAD_V = 128
D_K = 256
HEAD_K = 64
GATE_RANK = 16
GATE_NORMALIZER = 16.0
CHUNK_B = 64
EPS = 1e-6

OFF_U, OFF_V, OFF_ZA, OFF_Q, OFF_K, OFF_VB, OFF_ZB, OFF_LRF, OFF_LRB, D_IN = (
    0, 512, 1024, 1536, 1792, 2048, 2560, 3072, 3088, 3104)

LANES = 128
TILE = 256
N_CHUNK = TILE // CHUNK_B
N_PAIR = N_HEADS_B // 2
PIECE_COLS = 256
PRE_COLS = D_K + D_B + PIECE_COLS
PRE_SUB = 2
M_U, M_V, M_ZA, M_Q, M_ZB, P_COLS = 0, 512, 1024, 1536, 1792, 2304
VMEM_LIMIT = 56 * 1024 * 1024


def _dot(a, b):
    return jnp.dot(a, b, preferred_element_type=F32)


def _dot_nt(a, b):
    return lax.dot_general(a, b, (((1,), (1,)), ((), ())), preferred_element_type=F32)


def _gelu(x):
    return 0.5 * x * (1.0 + jnp.tanh(0.7978845608028654 * (x + 0.044715 * (x * x * x))))


def _silu(x):
    return x * (0.5 * jnp.tanh(0.5 * x) + 0.5)


def _log_sigmoid(x):
    return jnp.minimum(x, 0.0) - jnp.log1p(jnp.exp(-jnp.abs(x)))


def _rms_scale(x):
    return lax.rsqrt(jnp.mean(x * x, axis=-1, keepdims=True) + EPS)


def _chunk_cumsum(tri_ref, g):
    hi = g.astype(BF16)
    lo = (g - hi.astype(F32)).astype(BF16)
    tri = tri_ref[...]
    return _dot(tri, hi) + _dot(tri, lo)


def _decayed_keys(k, bcum, backward):
    kd, dec = [], []
    for c in range(N_CHUNK):
        r0 = c * CHUNK_B
        last = r0 if backward else r0 + CHUNK_B - 1
        b_last = bcum[last:last + 1, :]
        kd.append(k[r0:r0 + CHUNK_B, :] * jnp.exp(b_last - bcum[r0:r0 + CHUNK_B, :]))
        dec.append(jnp.exp(b_last))
    return jnp.concatenate(kd, axis=0), dec


def _chunk_kv(kt, v_par, c):
    cp, par = divmod(c, 2)
    parts = []
    for p in range(N_PAIR):
        full = _dot(kt[p * LANES:(p + 1) * LANES, cp * LANES:(cp + 1) * LANES],
                    v_par[cp][par][:, p * 2 * HEAD_V:(p + 1) * 2 * HEAD_V])
        parts.append(full[0:HEAD_K, 0:HEAD_V])
        parts.append(full[HEAD_K:2 * HEAD_K, HEAD_V:2 * HEAD_V])
    return jnp.concatenate(parts, axis=0)


def _decay_column(dec_row):
    return jnp.transpose(jnp.broadcast_to(dec_row, (LANES, D_K)))


def _parity_masked_values(vb):
    out = []
    row = lax.broadcasted_iota(jnp.int32, (2 * CHUNK_B, D_B), 0)
    for cp in range(N_CHUNK // 2):
        vp = vb[cp * 2 * CHUNK_B:(cp + 1) * 2 * CHUNK_B, :]
        zero = jnp.zeros_like(vp)
        out.append((jnp.where(row < CHUNK_B, vp, zero), jnp.where(row >= CHUNK_B, vp, zero)))
    return out


def _projection_pieces(x_ref, npre_ref, w_ref, p_ref, x_keep_ref=None):
    normed = {}

    def piece(r0, c0):
        def run():
            if r0 not in normed:
                x = x_ref[0, r0:r0 + TILE, :]
                if x_keep_ref is not None:
                    x_keep_ref[r0:r0 + TILE, :] = x
                normed[r0] = (x * _rms_scale(x) * npre_ref[...]).astype(BF16)
            p_ref[r0:r0 + TILE, c0:c0 + PIECE_COLS] = _dot(normed[r0], w_ref[:, c0:c0 + PIECE_COLS])
        return run

    return [piece(r0, c0) for r0 in range(0, p_ref.shape[0], TILE)
            for c0 in range(0, p_ref.shape[1], PIECE_COLS)]


def _emitter(pieces):
    def emit(n):
        for _ in range(min(n, len(pieces))):
            pieces.pop(0)()
    return emit


def _bwd_state_kernel(steps_per_row, n_steps, xn_ref, npre_ref, wkv_ref, wgk_ref, bgk_ref, utri_ref,
                      klr_ref, vs_ref, p0_ref, p1_ref, s_ref):
    i = pl.program_id(0)

    @pl.when(i == 0)
    def _():
        p1_ref[...] = jnp.zeros_like(p1_ref)

    @pl.when(jnp.logical_or(i == 0, (n_steps - i) % steps_per_row == steps_per_row - 1))
    def _():
        s_ref[...] = jnp.zeros_like(s_ref)

    def step(p_write_ref, p_ref):
        emit = _emitter(_projection_pieces(xn_ref, npre_ref, wkv_ref, p_write_ref))
        s = s_ref[...]
        for j in reversed(range(PRE_SUB)):
            r0 = j * TILE
            emit(1)
            k = p_ref[r0:r0 + TILE, 0:D_K]
            vb = p_ref[r0:r0 + TILE, D_K:D_K + D_B].astype(BF16)
            plr = p_ref[r0:r0 + TILE, D_K + D_B:D_K + D_B + LANES]
            klr_ref[j, :, 0:D_K] = k
            klr_ref[j, :, D_K:D_K + LANES] = plr
            vs_ref[j, 0:TILE, :] = vb
            g = _log_sigmoid(_dot(plr.astype(BF16), wgk_ref[...]) + bgk_ref[...]) * (
                1.0 / GATE_NORMALIZER)
            emit(1)
            bcum = _chunk_cumsum(utri_ref, g)
            kdec, dec = _decayed_keys(k, bcum, backward=True)
            kt = jnp.transpose(kdec).astype(BF16)
            emit(1)
            v_par = _parity_masked_values(vb)
            kv = [_chunk_kv(kt, v_par, c) for c in range(N_CHUNK)]
            dec_col = [_decay_column(dec[c]) for c in range(N_CHUNK)]
            emit(1)
            for c in reversed(range(N_CHUNK)):
                vs_ref[j, TILE:TILE + N_HEADS_B * HEAD_K, c * HEAD_V:(c + 1) * HEAD_V] = (
                    s.astype(BF16))
                s = dec_col[c] * s + kv[c]
        s_ref[...] = s

    @pl.when(i % 2 == 0)
    def _():
        step(p0_ref, p1_ref)

    @pl.when(i % 2 == 1)
    def _():
        step(p1_ref, p0_ref)


def _mix(p_ref, x_ref, klr_ref, vs_ref, wsp_ref, bsp_ref, gva_ref, wgk_ref, bgk_ref,
         ltri_ref, utri_ref, gnb_ref, wout_ref, npost_ref, out_ref, sf_ref, o_ref, mix_ref, emit):
    emit(2)
    vg = _gelu(p_ref[:, M_V:M_ZA])
    vc = vg - jnp.mean(vg, axis=-1, keepdims=True)
    vn = (vc * _rms_scale(vc) * gva_ref[...]).astype(BF16)
    for c in range(TILE // CHUNK_A):
        r0 = c * CHUNK_A
        for hd in range(N_HEADS_A):
            c0 = hd * HEAD_A
            sv = _dot(wsp_ref[hd], vn[r0:r0 + CHUNK_A, c0:c0 + HEAD_A]) + bsp_ref[hd]
            out_a = _gelu(p_ref[r0:r0 + CHUNK_A, M_U + c0:M_U + c0 + HEAD_A]) * sv
            out_a = out_a * _silu(p_ref[r0:r0 + CHUNK_A, M_ZA + c0:M_ZA + c0 + HEAD_A])
            mix_ref[r0:r0 + CHUNK_A, c0:c0 + HEAD_A] = out_a.astype(BF16)
            emit(1 if hd == 1 else 0)

    q = p_ref[:, M_Q:M_ZB] * (HEAD_K ** -0.5)
    k = klr_ref[0, :, 0:D_K]
    vb = vs_ref[0, 0:TILE, :]
    plr = klr_ref[0, :, D_K:D_K + LANES]
    emit(1)
    g = _log_sigmoid(_dot(plr.astype(BF16), wgk_ref[...]) + bgk_ref[...]) * (1.0 / GATE_NORMALIZER)
    v_par = _parity_masked_values(vb)

    lane_head = lax.broadcasted_iota(jnp.int32, (CHUNK_B, D_K), 1) // HEAD_K
    lane_pos = lax.broadcasted_iota(jnp.int32, (CHUNK_B, D_K), 1) % CHUNK_B
    row_pos = lax.broadcasted_iota(jnp.int32, (CHUNK_B, D_K), 0)
    zero_blk = jnp.zeros((CHUNK_B, HEAD_V), BF16)

    dirs = []
    for d in range(2):
        backward = d == 1
        emit(1)
        bcum = _chunk_cumsum(utri_ref if backward else ltri_ref, g[:, d * D_K:(d + 1) * D_K])
        q_in = (q * jnp.exp(bcum)).astype(BF16)
        k_in = (k * jnp.exp(-bcum)).astype(BF16)
        keep = (row_pos < lane_pos) if backward else (row_pos >= lane_pos)
        dirs.append((q_in, k_in, keep))
        if not backward:
            kdec, dec = _decayed_keys(k, bcum, backward)
            kt = jnp.transpose(kdec).astype(BF16)

    steps = [(t, d, N_CHUNK - 1 - t if d == 1 else t) for t in range(N_CHUNK) for d in range(2)]
    p_blk, kv_blk, dec_col = {}, {}, {}
    for t, d, c in steps:
        q_in, k_in, keep = dirs[d]
        k_c = k_in[c * CHUNK_B:(c + 1) * CHUNK_B, :]
        k_bd = jnp.concatenate(
            [jnp.where(lane_head == hd, k_c, jnp.zeros_like(k_c)) for hd in range(N_HEADS_B)],
            axis=0)
        scores = _dot_nt(q_in[c * CHUNK_B:(c + 1) * CHUNK_B, :], k_bd)
        p_blk[d, c] = jnp.where(keep, scores, 0.0).astype(BF16)
    for c in range(N_CHUNK):
        kv_blk[c] = _chunk_kv(kt, v_par, c)
        dec_col[c] = _decay_column(dec[c])

    for t, d, c in steps:
        q_in = dirs[d][0]
        r0 = c * CHUNK_B
        if d == 1:
            s_bf = vs_ref[0, TILE:TILE + N_HEADS_B * HEAD_K, c * HEAD_V:(c + 1) * HEAD_V]
        else:
            s_old = sf_ref[...]
            s_bf = s_old.astype(BF16)
        for p in range(N_PAIR):
            a0 = 2 * p * HEAD_V
            rhs = jnp.concatenate([
                jnp.concatenate([vb[r0:r0 + CHUNK_B, a0:a0 + HEAD_V], zero_blk], axis=1),
                jnp.concatenate([zero_blk, vb[r0:r0 + CHUNK_B, a0 + HEAD_V:a0 + 2 * HEAD_V]], axis=1),
                jnp.concatenate([s_bf[2 * p * HEAD_K:(2 * p + 1) * HEAD_K, :], zero_blk], axis=1),
                jnp.concatenate([zero_blk, s_bf[(2 * p + 1) * HEAD_K:(2 * p + 2) * HEAD_K, :]], axis=1),
            ], axis=0)
            lhs = jnp.concatenate([p_blk[d, c][:, p * LANES:(p + 1) * LANES],
                                   q_in[r0:r0 + CHUNK_B, p * LANES:(p + 1) * LANES]], axis=1)
            o_pair = _dot(lhs, rhs)
            if t < N_CHUNK // 2:
                o_ref[r0:r0 + CHUNK_B, a0:a0 + 2 * HEAD_V] = o_pair
            else:
                o_ref[r0:r0 + CHUNK_B, a0:a0 + 2 * HEAD_V] += o_pair
        if d == 0:
            sf_ref[...] = dec_col[c] * s_old + kv_blk[c]

    emit(1)
    for hd in range(N_HEADS_B):
        c0 = hd * HEAD_V
        oh = o_ref[:, c0:c0 + HEAD_V]
        out_b = oh * _rms_scale(oh) * gnb_ref[...] * _silu(p_ref[:, M_ZB + c0:M_ZB + c0 + HEAD_V])
        mix_ref[:, D_A + c0:D_A + c0 + HEAD_V] = out_b.astype(BF16)

    mixed = jnp.concatenate(
        [_dot(mix_ref[...], wout_ref[:, c0:c0 + PIECE_COLS]) for c0 in range(0, D_MODEL, PIECE_COLS)],
        axis=1)
    emit(P_COLS // PIECE_COLS)
    out_ref[0] = x_ref[...] + mixed * _rms_scale(mixed) * npost_ref[...]


def _layer_kernel(n_tiles, xn_ref, klr_ref, vs_ref, npre_ref, win_ref,
                  wsp_ref, bsp_ref, gva_ref, wgk_ref, bgk_ref, ltri_ref, utri_ref, gnb_ref,
                  wout_ref, npost_ref, out_ref, p0_ref, p1_ref, x0_ref, x1_ref, sf_ref, o_ref, mix_ref):
    i = pl.program_id(0)

    @pl.when(i == 0)
    def _():
        p1_ref[...] = jnp.zeros_like(p1_ref)
        x1_ref[...] = jnp.zeros_like(x1_ref)

    @pl.when(jnp.logical_or(i == 0, (i - 1) % n_tiles == 0))
    def _():
        sf_ref[...] = jnp.zeros_like(sf_ref)

    def step(p_write_ref, p_read_ref, x_write_ref, x_read_ref):
        emit = _emitter(_projection_pieces(xn_ref, npre_ref, win_ref, p_write_ref, x_write_ref))
        _mix(p_read_ref, x_read_ref, klr_ref, vs_ref, wsp_ref, bsp_ref, gva_ref, wgk_ref,
             bgk_ref, ltri_ref, utri_ref, gnb_ref, wout_ref, npost_ref, out_ref, sf_ref, o_ref,
             mix_ref, emit)

    @pl.when(i % 2 == 0)
    def _():
        step(p0_ref, p1_ref, x0_ref, x1_ref)

    @pl.when(i % 2 == 1)
    def _():
        step(p1_ref, p0_ref, x1_ref, x0_ref)


def _const_spec(shape):
    return pl.BlockSpec(shape, lambda *_: (0,) * len(shape))


def _hybrid_layer(x, wts):
    bsz, seq, _ = x.shape
    n_tiles = seq // TILE
    n_total = bsz * n_tiles

    x_tiles = x.reshape(n_total, TILE, D_MODEL)
    params = pltpu.CompilerParams(dimension_semantics=("arbitrary",), vmem_limit_bytes=VMEM_LIMIT)

    n_steps = n_total // PRE_SUB

    def recurred_block(i):
        return (jnp.minimum(n_steps - i, n_steps - 1), 0, 0)

    klr_tiles, vs_tiles = pl.pallas_call(
        functools.partial(_bwd_state_kernel, n_tiles // PRE_SUB, n_steps),
        grid=(n_steps + 1,),
        in_specs=[
            pl.BlockSpec((1, PRE_SUB * TILE, D_MODEL),
                         lambda i: (jnp.maximum(n_steps - 1 - i, 0), 0, 0)),
            _const_spec((1, D_MODEL)),
            _const_spec((D_MODEL, PRE_COLS + PIECE_COLS)),
            _const_spec((LANES, D_K)),
            _const_spec((1, D_K)),
            _const_spec((TILE, TILE)),
        ],
        out_specs=[
            pl.BlockSpec((PRE_SUB, TILE, D_K + LANES), recurred_block),
            pl.BlockSpec((PRE_SUB, TILE + N_HEADS_B * HEAD_K, D_B), recurred_block),
        ],
        out_shape=[
            jax.ShapeDtypeStruct((n_total, TILE, D_K + LANES), F32),
            jax.ShapeDtypeStruct((n_total, TILE + N_HEADS_B * HEAD_K, D_B), BF16),
        ],
        scratch_shapes=[
            pltpu.VMEM((PRE_SUB * TILE, PRE_COLS), F32),
            pltpu.VMEM((PRE_SUB * TILE, PRE_COLS), F32),
            pltpu.VMEM((N_HEADS_B * HEAD_K, HEAD_V), F32),
        ],
        compiler_params=params,
    )(x.reshape(n_steps, PRE_SUB * TILE, D_MODEL), wts["npre"], wts["wkv"], wts["wgk_b"],
      wts["bgk_b"], wts["utri"])

    def finished_tile(i):
        return (jnp.maximum(i - 1, 0), 0, 0)

    y = pl.pallas_call(
        functools.partial(_layer_kernel, n_tiles),
        grid=(n_total + 1,),
        in_specs=[
            pl.BlockSpec((1, TILE, D_MODEL), lambda i: (jnp.minimum(i, n_total - 1), 0, 0)),
            pl.BlockSpec((1, TILE, D_K + LANES), finished_tile),
            pl.BlockSpec((1, TILE + N_HEADS_B * HEAD_K, D_B), finished_tile),
            _const_spec((1, D_MODEL)),
            _const_spec((D_MODEL, P_COLS)),
            _const_spec((N_HEADS_A, CHUNK_A, CHUNK_A)),
            _const_spec((N_HEADS_A, CHUNK_A, HEAD_A)),
            _const_spec((1, D_A)),
            _const_spec((LANES, 2 * D_K)),
            _const_spec((1, 2 * D_K)),
            _const_spec((TILE, TILE)),
            _const_spec((TILE, TILE)),
            _const_spec((1, HEAD_V)),
            _const_spec((D_A + D_B, D_MODEL + PIECE_COLS)),
            _const_spec((1, D_MODEL)),
        ],
        out_specs=pl.BlockSpec((1, TILE, D_MODEL), finished_tile),
        out_shape=jax.ShapeDtypeStruct((n_total, TILE, D_MODEL), x.dtype),
        scratch_shapes=[
            pltpu.VMEM((TILE, P_COLS), F32),
            pltpu.VMEM((TILE, P_COLS), F32),
            pltpu.VMEM((TILE, D_MODEL), F32),
            pltpu.VMEM((TILE, D_MODEL), F32),
            pltpu.VMEM((N_HEADS_B * HEAD_K, HEAD_V), F32),
            pltpu.VMEM((TILE, D_B), F32),
            pltpu.VMEM((TILE, D_A + D_B), BF16),
        ],
        compiler_params=params,
    )(x_tiles, klr_tiles, vs_tiles, wts["npre"], wts["win"], wts["wsp"],
      wts["bsp"], wts["gva"], wts["wgk"], wts["bgk"], wts["ltri"], wts["utri"], wts["gnb"],
      wts["wout"], wts["npost"])
    return y.reshape(x.shape)


def _prepare_weights(norm_pre, w_in, w_sp, b_sp, g_v_a, w_gk_fwd, b_gk_fwd, w_gk_bwd, b_gk_bwd,
                     g_norm_b, w_out, norm_post):
    w_in_bf = w_in.astype(BF16)
    wlr = jnp.zeros((D_MODEL, PIECE_COLS), BF16).at[:, 0:2 * GATE_RANK].set(w_in_bf[:, OFF_LRF:D_IN])
    wgk = jnp.zeros((LANES, 2 * D_K), BF16)
    wgk = wgk.at[0:GATE_RANK, 0:D_K].set(w_gk_fwd.astype(BF16))
    wgk = wgk.at[GATE_RANK:2 * GATE_RANK, D_K:2 * D_K].set(w_gk_bwd.astype(BF16))
    pad_cols = jnp.zeros((D_MODEL, PIECE_COLS), BF16)
    pos = jnp.arange(TILE)
    same_chunk = (pos[:, None] // CHUNK_B) == (pos[None, :] // CHUNK_B)
    ltri = (same_chunk & (pos[None, :] <= pos[:, None])).astype(BF16)
    utri = (same_chunk & (pos[None, :] >= pos[:, None])).astype(BF16)
    return {
        "npre": norm_pre.reshape(1, D_MODEL),
        "win": jnp.concatenate([w_in_bf[:, OFF_U:OFF_K], w_in_bf[:, OFF_ZB:OFF_LRF]], axis=1),
        "wkv": jnp.concatenate([w_in_bf[:, OFF_K:OFF_ZB], wlr, pad_cols], axis=1),
        "wsp": w_sp.astype(BF16),
        "bsp": jnp.broadcast_to(b_sp[:, :, None], (N_HEADS_A, CHUNK_A, HEAD_A)),
        "gva": g_v_a.reshape(1, D_A),
        "wgk": wgk,
        "wgk_b": wgk[:, D_K:2 * D_K],
        "bgk": jnp.concatenate([b_gk_fwd, b_gk_bwd]).reshape(1, 2 * D_K),
        "bgk_b": b_gk_bwd.reshape(1, D_K),
        "ltri": ltri,
        "utri": utri,
        "gnb": g_norm_b.reshape(1, HEAD_V),
        "wout": jnp.concatenate([w_out.astype(BF16), pad_cols], axis=1),
        "npost": norm_post.reshape(1, D_MODEL),
    }


def kernel(x_prompt, x_sample, norm_pre, w_in, w_sp, b_sp, g_v_a, w_gk_fwd, b_gk_fwd,
           w_gk_bwd, b_gk_bwd, g_norm_b, w_out, norm_post):
    y_prompt, y_sample = x_prompt, x_sample
    for l in range(norm_pre.shape[0]):
        wts = _prepare_weights(norm_pre[l], w_in[l], w_sp[l], b_sp[l], g_v_a[l], w_gk_fwd[l],
                               b_gk_fwd[l], w_gk_bwd[l], b_gk_bwd[l], g_norm_b[l], w_out[l],
                               norm_post[l])
        y_prompt = _hybrid_layer(y_prompt, wts)
        y_sample = _hybrid_layer(y_sample, wts)
    return (y_prompt, y_sample)
```

```python
import functools

import jax
import jax.numpy as jnp
from jax import lax
from jax.experimental import pallas as pl
from jax.experimental.pallas import tpu as pltpu

F32 = jnp.float32
BF16 = jnp.bfloat16

D_MODEL = 1024
D_A = 512
N_HEADS_A = 4
HEAD_A = 128
CHUNK_A = 128
D_B = 512
N_HEADS_B = 4
HEAD_V = 128
D_K = 256
HEAD_K = 64
GATE_RANK = 16
GATE_NORMALIZER = 16.0
CHUNK_B = 64
EPS = 1e-6

OFF_U, OFF_V, OFF_ZA, OFF_Q, OFF_K, OFF_VB, OFF_ZB, OFF_LRF, OFF_LRB, D_IN = (
    0, 512, 1024, 1536, 1792, 2048, 2560, 3072, 3088, 3104)

LANES = 128
TILE = 256
N_CHUNK = TILE // CHUNK_B
N_PAIR = N_HEADS_B // 2
PIECE_COLS = 256
PRE_COLS = D_K + D_B + PIECE_COLS
PRE_SUB = 4
M_U, M_V, M_ZA, M_Q, M_ZB, P_COLS = 0, 512, 1024, 1536, 1792, 2304
VMEM_LIMIT = 56 * 1024 * 1024


def _dot(a, b):
    return jnp.dot(a, b, preferred_element_type=F32)


def _dot_nt(a, b):
    return lax.dot_general(a, b, (((1,), (1,)), ((), ())), preferred_element_type=F32)


def _gelu(x):
    return 0.5 * x * (1.0 + jnp.tanh(0.7978845608028654 * (x + 0.044715 * (x * x * x))))


def _silu(x):
    return x * (0.5 * jnp.tanh(0.5 * x) + 0.5)


def _log_sigmoid(x):
    return jnp.minimum(x, 0.0) - jnp.log1p(jnp.exp(-jnp.abs(x)))


def _rms_scale(x):
    return lax.rsqrt(jnp.mean(x * x, axis=-1, keepdims=True) + EPS)


def _chunk_cumsum(tri_ref, g):
    hi = g.astype(BF16)
    lo = (g - hi.astype(F32)).astype(BF16)
    tri = tri_ref[...]
    return _dot(tri, hi) + _dot(tri, lo)


def _decayed_keys(k, bcum, backward):
    kd, dec = [], []
    for c in range(N_CHUNK):
        r0 = c * CHUNK_B
        last = r0 if backward else r0 + CHUNK_B - 1
        b_last = bcum[last:last + 1, :]
        kd.append(k[r0:r0 + CHUNK_B, :] * jnp.exp(b_last - bcum[r0:r0 + CHUNK_B, :]))
        dec.append(jnp.exp(b_last))
    return jnp.concatenate(kd, axis=0), dec


def _chunk_kv(kt, v_par, c):
    cp, par = divmod(c, 2)
    parts = []
    for p in range(N_PAIR):
        full = _dot(kt[p * LANES:(p + 1) * LANES, cp * LANES:(cp + 1) * LANES],
                    v_par[cp][par][:, p * 2 * HEAD_V:(p + 1) * 2 * HEAD_V])
        parts.append(full[0:HEAD_K, 0:HEAD_V])
        parts.append(full[HEAD_K:2 * HEAD_K, HEAD_V:2 * HEAD_V])
    return jnp.concatenate(parts, axis=0)


def _decay_column(dec_row):
    return jnp.transpose(jnp.broadcast_to(dec_row, (LANES, D_K)))


def _parity_masked_values(vb):
    out = []
    row = lax.broadcasted_iota(jnp.int32, (2 * CHUNK_B, D_B), 0)
    for cp in range(N_CHUNK // 2):
        vp = vb[cp * 2 * CHUNK_B:(cp + 1) * 2 * CHUNK_B, :]
        zero = jnp.zeros_like(vp)
        out.append((jnp.where(row < CHUNK_B, vp, zero), jnp.where(row >= CHUNK_B, vp, zero)))
    return out


def _projection_pieces(x_ref, npre_ref, w_ref, p_ref, x_keep_ref=None):
    normed = {}

    def piece(r0, c0):
        def run():
            if r0 not in normed:
                x = x_ref[0, r0:r0 + TILE, :]
                if x_keep_ref is not None:
                    x_keep_ref[r0:r0 + TILE, :] = x
                normed[r0] = (x * _rms_scale(x) * npre_ref[...]).astype(BF16)
            p_ref[r0:r0 + TILE, c0:c0 + PIECE_COLS] = _dot(normed[r0], w_ref[:, c0:c0 + PIECE_COLS])
        return run

    return [piece(r0, c0) for r0 in range(0, p_ref.shape[0], TILE)
            for c0 in range(0, p_ref.shape[1], PIECE_COLS)]


def _emitter(pieces):
    def emit(n):
        for _ in range(min(n, len(pieces))):
            pieces.pop(0)()
    return emit


def _bwd_state_kernel(steps_per_row, n_steps, xn_ref, npre_ref, wkv_ref, wgk_ref, bgk_ref, utri_ref,
                      klr_ref, vs_ref, p0_ref, p1_ref, s_ref):
    i = pl.program_id(0)

    @pl.when(i == 0)
    def _():
        p1_ref[...] = jnp.zeros_like(p1_ref)

    @pl.when(jnp.logical_or(i == 0, (n_steps - i) % steps_per_row == steps_per_row - 1))
    def _():
        s_ref[...] = jnp.zeros_like(s_ref)

    def step(p_write_ref, p_ref):
        emit = _emitter(_projection_pieces(xn_ref, npre_ref, wkv_ref, p_write_ref))
        s = s_ref[...]
        for j in reversed(range(PRE_SUB)):
            r0 = j * TILE
            emit(1)
            k = p_ref[r0:r0 + TILE, 0:D_K]
            vb = p_ref[r0:r0 + TILE, D_K:D_K + D_B].astype(BF16)
            plr = p_ref[r0:r0 + TILE, D_K + D_B:D_K + D_B + LANES]
            klr_ref[j, :, 0:D_K] = k
            klr_ref[j, :, D_K:D_K + LANES] = plr
            vs_ref[j, 0:TILE, :] = vb
            g = _log_sigmoid(_dot(plr.astype(BF16), wgk_ref[...]) + bgk_ref[...]) * (
                1.0 / GATE_NORMALIZER)
            emit(1)
            bcum = _chunk_cumsum(utri_ref, g)
            kdec, dec = _decayed_keys(k, bcum, backward=True)
            kt = jnp.transpose(kdec).astype(BF16)
            emit(1)
            v_par = _parity_masked_values(vb)
            kv = [_chunk_kv(kt, v_par, c) for c in range(N_CHUNK)]
            dec_col = [_decay_column(dec[c]) for c in range(N_CHUNK)]
            emit(1)
            for c in reversed(range(N_CHUNK)):
                vs_ref[j, TILE:TILE + N_HEADS_B * HEAD_K, c * HEAD_V:(c + 1) * HEAD_V] = (
                    s.astype(BF16))
                s = dec_col[c] * s + kv[c]
        s_ref[...] = s

    @pl.when(i % 2 == 0)
    def _():
        step(p0_ref, p1_ref)

    @pl.when(i % 2 == 1)
    def _():
        step(p1_ref, p0_ref)


def _mix(p_ref, x_ref, klr_ref, vs_ref, wsp_ref, bsp_ref, gva_ref, wgk_ref, bgk_ref,
         ltri_ref, utri_ref, gnb_ref, wout_ref, npost_ref, out_ref, sf_ref, o_ref, mix_ref, emit):
    emit(2)
    vg = _gelu(p_ref[:, M_V:M_ZA])
    vc = vg - jnp.mean(vg, axis=-1, keepdims=True)
    vn = (vc * _rms_scale(vc) * gva_ref[...]).astype(BF16)
    for c in range(TILE // CHUNK_A):
        r0 = c * CHUNK_A
        for hd in range(N_HEADS_A):
            c0 = hd * HEAD_A
            sv = _dot(wsp_ref[hd], vn[r0:r0 + CHUNK_A, c0:c0 + HEAD_A]) + bsp_ref[hd]
            out_a = _gelu(p_ref[r0:r0 + CHUNK_A, M_U + c0:M_U + c0 + HEAD_A]) * sv
            out_a = out_a * _silu(p_ref[r0:r0 + CHUNK_A, M_ZA + c0:M_ZA + c0 + HEAD_A])
            mix_ref[r0:r0 + CHUNK_A, c0:c0 + HEAD_A] = out_a.astype(BF16)
            emit(1 if hd == 1 else 0)

    q = p_ref[:, M_Q:M_ZB] * (HEAD_K ** -0.5)
    k = klr_ref[0, :, 0:D_K]
    vb = vs_ref[0, 0:TILE, :]
    plr = klr_ref[0, :, D_K:D_K + LANES]
    emit(1)
    g = _log_sigmoid(_dot(plr.astype(BF16), wgk_ref[...]) + bgk_ref[...]) * (1.0 / GATE_NORMALIZER)
    v_par = _parity_masked_values(vb)

    lane_head = lax.broadcasted_iota(jnp.int32, (CHUNK_B, D_K), 1) // HEAD_K
    lane_pos = lax.broadcasted_iota(jnp.int32, (CHUNK_B, D_K), 1) % CHUNK_B
    row_pos = lax.broadcasted_iota(jnp.int32, (CHUNK_B, D_K), 0)
    zero_blk = jnp.zeros((CHUNK_B, HEAD_V), BF16)

    dirs = []
    for d in range(2):
        backward = d == 1
        emit(1)
        bcum = _chunk_cumsum(utri_ref if backward else ltri_ref, g[:, d * D_K:(d + 1) * D_K])
        q_in = (q * jnp.exp(bcum)).astype(BF16)
        k_in = (k * jnp.exp(-bcum)).astype(BF16)
        keep = (row_pos < lane_pos) if backward else (row_pos >= lane_pos)
        dirs.append((q_in, k_in, keep))
        if not backward:
            kdec, dec = _decayed_keys(k, bcum, backward)
            kt = jnp.transpose(kdec).astype(BF16)

    steps = [(t, d, N_CHUNK - 1 - t if d == 1 else t) for t in range(N_CHUNK) for d in range(2)]
    p_blk, kv_blk, dec_col = {}, {}, {}
    for t, d, c in steps:
        q_in, k_in, keep = dirs[d]
        k_c = k_in[c * CHUNK_B:(c + 1) * CHUNK_B, :]
        k_bd = jnp.concatenate(
            [jnp.where(lane_head == hd, k_c, jnp.zeros_like(k_c)) for hd in range(N_HEADS_B)],
            axis=0)
        scores = _dot_nt(q_in[c * CHUNK_B:(c + 1) * CHUNK_B, :], k_bd)
        p_blk[d, c] = jnp.where(keep, scores, 0.0).astype(BF16)
    for c in range(N_CHUNK):
        kv_blk[c] = _chunk_kv(kt, v_par, c)
        dec_col[c] = _decay_column(dec[c])

    for t, d, c in steps:
        q_in = dirs[d][0]
        r0 = c * CHUNK_B
        if d == 1:
            s_bf = vs_ref[0, TILE:TILE + N_HEADS_B * HEAD_K, c * HEAD_V:(c + 1) * HEAD_V]
        else:
            s_old = sf_ref[...]
            s_bf = s_old.astype(BF16)
        for p in range(N_PAIR):
            a0 = 2 * p * HEAD_V
            rhs = jnp.concatenate([
                jnp.concatenate([vb[r0:r0 + CHUNK_B, a0:a0 + HEAD_V], zero_blk], axis=1),
                jnp.concatenate([zero_blk, vb[r0:r0 + CHUNK_B, a0 + HEAD_V:a0 + 2 * HEAD_V]], axis=1),
                jnp.concatenate([s_bf[2 * p * HEAD_K:(2 * p + 1) * HEAD_K, :], zero_blk], axis=1),
                jnp.concatenate([zero_blk, s_bf[(2 * p + 1) * HEAD_K:(2 * p + 2) * HEAD_K, :]], axis=1),
            ], axis=0)
            lhs = jnp.concatenate([p_blk[d, c][:, p * LANES:(p + 1) * LANES],
                                   q_in[r0:r0 + CHUNK_B, p * LANES:(p + 1) * LANES]], axis=1)
            o_pair = _dot(lhs, rhs)
            if t < N_CHUNK // 2:
                o_ref[r0:r0 + CHUNK_B, a0:a0 + 2 * HEAD_V] = o_pair
            else:
                o_ref[r0:r0 + CHUNK_B, a0:a0 + 2 * HEAD_V] += o_pair
        if d == 0:
            sf_ref[...] = dec_col[c] * s_old + kv_blk[c]

    emit(1)
    for hd in range(N_HEADS_B):
        c0 = hd * HEAD_V
        oh = o_ref[:, c0:c0 + HEAD_V]
        out_b = oh * _rms_scale(oh) * gnb_ref[...] * _silu(p_ref[:, M_ZB + c0:M_ZB + c0 + HEAD_V])
        mix_ref[:, D_A + c0:D_A + c0 + HEAD_V] = out_b.astype(BF16)

    mixed = jnp.concatenate(
        [_dot(mix_ref[...], wout_ref[:, c0:c0 + PIECE_COLS]) for c0 in range(0, D_MODEL, PIECE_COLS)],
        axis=1)
    emit(P_COLS // PIECE_COLS)
    out_ref[0] = x_ref[...] + mixed * _rms_scale(mixed) * npost_ref[...]


def _layer_kernel(n_tiles, xn_ref, klr_ref, vs_ref, npre_ref, win_ref,
                  wsp_ref, bsp_ref, gva_ref, wgk_ref, bgk_ref, ltri_ref, utri_ref, gnb_ref,
                  wout_ref, npost_ref, out_ref, p0_ref, p1_ref, x0_ref, x1_ref, sf_ref, o_ref, mix_ref):
    i = pl.program_id(0)

    @pl.when(i == 0)
    def _():
        p1_ref[...] = jnp.zeros_like(p1_ref)
        x1_ref[...] = jnp.zeros_like(x1_ref)

    @pl.when(jnp.logical_or(i == 0, (i - 1) % n_tiles == 0))
    def _():
        sf_ref[...] = jnp.zeros_like(sf_ref)

    def step(p_write_ref, p_read_ref, x_write_ref, x_read_ref):
        emit = _emitter(_projection_pieces(xn_ref, npre_ref, win_ref, p_write_ref, x_write_ref))
        _mix(p_read_ref, x_read_ref, klr_ref, vs_ref, wsp_ref, bsp_ref, gva_ref, wgk_ref,
             bgk_ref, ltri_ref, utri_ref, gnb_ref, wout_ref, npost_ref, out_ref, sf_ref, o_ref,
             mix_ref, emit)

    @pl.when(i % 2 == 0)
    def _():
        step(p0_ref, p1_ref, x0_ref, x1_ref)

    @pl.when(i % 2 == 1)
    def _():
        step(p1_ref, p0_ref, x1_ref, x0_ref)


def _const_spec(shape):
    return pl.BlockSpec(shape, lambda *_: (0,) * len(shape))


def _hybrid_layer(x, wts):
    bsz, seq, _ = x.shape
    n_tiles = seq // TILE
    n_total = bsz * n_tiles

    x_tiles = x.reshape(n_total, TILE, D_MODEL)
    params = pltpu.CompilerParams(dimension_semantics=("arbitrary",), vmem_limit_bytes=VMEM_LIMIT)

    n_steps = n_total // PRE_SUB

    def recurred_block(i):
        return (jnp.minimum(n_steps - i, n_steps - 1), 0, 0)

    klr_tiles, vs_tiles = pl.pallas_call(
        functools.partial(_bwd_state_kernel, n_tiles // PRE_SUB, n_steps),
        grid=(n_steps + 1,),
        in_specs=[
            pl.BlockSpec((1, PRE_SUB * TILE, D_MODEL),
                         lambda i: (jnp.maximum(n_steps - 1 - i, 0), 0, 0)),
            _const_spec((1, D_MODEL)),
            _const_spec((D_MODEL, PRE_COLS + PIECE_COLS)),
            _const_spec((LANES, D_K)),
            _const_spec((1, D_K)),
            _const_spec((TILE, TILE)),
        ],
        out_specs=[
            pl.BlockSpec((PRE_SUB, TILE, D_K + LANES), recurred_block),
            pl.BlockSpec((PRE_SUB, TILE + N_HEADS_B * HEAD_K, D_B), recurred_block),
        ],
        out_shape=[
            jax.ShapeDtypeStruct((n_total, TILE, D_K + LANES), F32),
            jax.ShapeDtypeStruct((n_total, TILE + N_HEADS_B * HEAD_K, D_B), BF16),
        ],
        scratch_shapes=[
            pltpu.VMEM((PRE_SUB * TILE, PRE_COLS), F32),
            pltpu.VMEM((PRE_SUB * TILE, PRE_COLS), F32),
            pltpu.VMEM((N_HEADS_B * HEAD_K, HEAD_V), F32),
        ],
        compiler_params=params,
    )(x.reshape(n_steps, PRE_SUB * TILE, D_MODEL), wts["npre"], wts["wkv"], wts["wgk_b"],
      wts["bgk_b"], wts["utri"])

    def finished_tile(i):
        return (jnp.maximum(i - 1, 0), 0, 0)

    y = pl.pallas_call(
        functools.partial(_layer_kernel, n_tiles),
        grid=(n_total + 1,),
        in_specs=[
            pl.BlockSpec((1, TILE, D_MODEL), lambda i: (jnp.minimum(i, n_total - 1), 0, 0)),
            pl.BlockSpec((1, TILE, D_K + LANES), finished_tile),
            pl.BlockSpec((1, TILE + N_HEADS_B * HEAD_K, D_B), finished_tile),
            _const_spec((1, D_MODEL)),
            _const_spec((D_MODEL, P_COLS)),
            _const_spec((N_HEADS_A, CHUNK_A, CHUNK_A)),
            _const_spec((N_HEADS_A, CHUNK_A, HEAD_A)),
            _const_spec((1, D_A)),
            _const_spec((LANES, 2 * D_K)),
            _const_spec((1, 2 * D_K)),
            _const_spec((TILE, TILE)),
            _const_spec((TILE, TILE)),
            _const_spec((1, HEAD_V)),
            _const_spec((D_A + D_B, D_MODEL + PIECE_COLS)),
            _const_spec((1, D_MODEL)),
        ],
        out_specs=pl.BlockSpec((1, TILE, D_MODEL), finished_tile),
        out_shape=jax.ShapeDtypeStruct((n_total, TILE, D_MODEL), x.dtype),
        scratch_shapes=[
            pltpu.VMEM((TILE, P_COLS), F32),
            pltpu.VMEM((TILE, P_COLS), F32),
            pltpu.VMEM((TILE, D_MODEL), F32),
            pltpu.VMEM((TILE, D_MODEL), F32),
            pltpu.VMEM((N_HEADS_B * HEAD_K, HEAD_V), F32),
            pltpu.VMEM((TILE, D_B), F32),
            pltpu.VMEM((TILE, D_A + D_B), BF16),
        ],
        compiler_params=params,
    )(x_tiles, klr_tiles, vs_tiles, wts["npre"], wts["win"], wts["wsp"],
      wts["bsp"], wts["gva"], wts["wgk"], wts["bgk"], wts["ltri"], wts["utri"], wts["gnb"],
      wts["wout"], wts["npost"])
    return y.reshape(x.shape)


def _prepare_weights(norm_pre, w_in, w_sp, b_sp, g_v_a, w_gk_fwd, b_gk_fwd, w_gk_bwd, b_gk_bwd,
                     g_norm_b, w_out, norm_post):
    w_in_bf = w_in.astype(BF16)
    wlr = jnp.zeros((D_MODEL, PIECE_COLS), BF16).at[:, 0:2 * GATE_RANK].set(w_in_bf[:, OFF_LRF:D_IN])
    wgk = jnp.zeros((LANES, 2 * D_K), BF16)
    wgk = wgk.at[0:GATE_RANK, 0:D_K].set(w_gk_fwd.astype(BF16))
    wgk = wgk.at[GATE_RANK:2 * GATE_RANK, D_K:2 * D_K].set(w_gk_bwd.astype(BF16))
    pad_cols = jnp.zeros((D_MODEL, PIECE_COLS), BF16)
    pos = jnp.arange(TILE)
    same_chunk = (pos[:, None] // CHUNK_B) == (pos[None, :] // CHUNK_B)
    ltri = (same_chunk & (pos[None, :] <= pos[:, None])).astype(BF16)
    utri = (same_chunk & (pos[None, :] >= pos[:, None])).astype(BF16)
    return {
        "npre": norm_pre.reshape(1, D_MODEL),
        "win": jnp.concatenate([w_in_bf[:, OFF_U:OFF_K], w_in_bf[:, OFF_ZB:OFF_LRF]], axis=1),
        "wkv": jnp.concatenate([w_in_bf[:, OFF_K:OFF_ZB], wlr, pad_cols], axis=1),
        "wsp": w_sp.astype(BF16),
        "bsp": jnp.broadcast_to(b_sp[:, :, None], (N_HEADS_A, CHUNK_A, HEAD_A)),
        "gva": g_v_a.reshape(1, D_A),
        "wgk": wgk,
        "wgk_b": wgk[:, D_K:2 * D_K],
        "bgk": jnp.concatenate([b_gk_fwd, b_gk_bwd]).reshape(1, 2 * D_K),
        "bgk_b": b_gk_bwd.reshape(1, D_K),
        "ltri": ltri,
        "utri": utri,
        "gnb": g_norm_b.reshape(1, HEAD_V),
        "wout": jnp.concatenate([w_out.astype(BF16), pad_cols], axis=1),
        "npost": norm_post.reshape(1, D_MODEL),
    }


def kernel(x_prompt, x_sample, norm_pre, w_in, w_sp, b_sp, g_v_a, w_gk_fwd, b_gk_fwd,
           w_gk_bwd, b_gk_bwd, g_norm_b, w_out, norm_post):
    y_prompt, y_sample = x_prompt, x_sample
    for l in range(norm_pre.shape[0]):
        wts = _prepare_weights(norm_pre[l], w_in[l], w_sp[l], b_sp[l], g_v_a[l], w_gk_fwd[l],
                               b_gk_fwd[l], w_gk_bwd[l], b_gk_bwd[l], g_norm_b[l], w_out[l],
                               norm_post[l])
        y_prompt = _hybrid_layer(y_prompt, wts)
        y_sample = _hybrid_layer(y_sample, wts)
    return (y_prompt, y_sample)
```

```python
import functools

import jax
import jax.numpy as jnp
from jax import lax
from jax.experimental import pallas as pl
from jax.experimental.pallas import tpu as pltpu

F32 = jnp.float32
BF16 = jnp.bfloat16

D_MODEL = 1024
D_A = 512
N_HEADS_A = 4
HEAD_A = 128
CHUNK_A = 128
D_B = 512
N_HEADS_B = 4
HEAD_V = 128
D_K = 256
HEAD_K = 64
GATE_RANK = 16
GATE_NORMALIZER = 16.0
CHUNK_B = 64
EPS = 1e-6

OFF_U, OFF_V, OFF_ZA, OFF_Q, OFF_K, OFF_VB, OFF_ZB, OFF_LRF, OFF_LRB, D_IN = (
    0, 512, 1024, 1536, 1792, 2048, 2560, 3072, 3088, 3104)

LANES = 128
TILE = 256
N_CHUNK = TILE // CHUNK_B
N_PAIR = N_HEADS_B // 2
PIECE_COLS = 256
PRE_COLS = D_K + D_B + PIECE_COLS
PRE_SUB = 4
MAIN_SUB = 2
M_U, M_V, M_ZA, M_Q, M_ZB, P_COLS = 0, 512, 1024, 1536, 1792, 2304
VMEM_LIMIT = 56 * 1024 * 1024


def _dot(a, b):
    return jnp.dot(a, b, preferred_element_type=F32)


def _dot_nt(a, b):
    return lax.dot_general(a, b, (((1,), (1,)), ((), ())), preferred_element_type=F32)


def _gelu(x):
    return 0.5 * x * (1.0 + jnp.tanh(0.7978845608028654 * (x + 0.044715 * (x * x * x))))


def _silu(x):
    return x * (0.5 * jnp.tanh(0.5 * x) + 0.5)


def _log_sigmoid(x):
    return jnp.minimum(x, 0.0) - jnp.log1p(jnp.exp(-jnp.abs(x)))


def _rms_scale(x):
    return lax.rsqrt(jnp.mean(x * x, axis=-1, keepdims=True) + EPS)


def _chunk_cumsum(tri_ref, g):
    hi = g.astype(BF16)
    lo = (g - hi.astype(F32)).astype(BF16)
    tri = tri_ref[...]
    return _dot(tri, hi) + _dot(tri, lo)


def _decayed_keys(k, bcum, backward):
    kd, dec = [], []
    for c in range(N_CHUNK):
        r0 = c * CHUNK_B
        last = r0 if backward else r0 + CHUNK_B - 1
        b_last = bcum[last:last + 1, :]
        kd.append(k[r0:r0 + CHUNK_B, :] * jnp.exp(b_last - bcum[r0:r0 + CHUNK_B, :]))
        dec.append(jnp.exp(b_last))
    return jnp.concatenate(kd, axis=0), dec


def _chunk_kv(kt, v_par, c):
    cp, par = divmod(c, 2)
    parts = []
    for p in range(N_PAIR):
        full = _dot(kt[p * LANES:(p + 1) * LANES, cp * LANES:(cp + 1) * LANES],
                    v_par[cp][par][:, p * 2 * HEAD_V:(p + 1) * 2 * HEAD_V])
        parts.append(full[0:HEAD_K, 0:HEAD_V])
        parts.append(full[HEAD_K:2 * HEAD_K, HEAD_V:2 * HEAD_V])
    return jnp.concatenate(parts, axis=0)


def _decay_column(dec_row):
    return jnp.transpose(jnp.broadcast_to(dec_row, (LANES, D_K)))


def _parity_masked_values(vb):
    out = []
    row = lax.broadcasted_iota(jnp.int32, (2 * CHUNK_B, D_B), 0)
    for cp in range(N_CHUNK // 2):
        vp = vb[cp * 2 * CHUNK_B:(cp + 1) * 2 * CHUNK_B, :]
        zero = jnp.zeros_like(vp)
        out.append((jnp.where(row < CHUNK_B, vp, zero), jnp.where(row >= CHUNK_B, vp, zero)))
    return out


def _projection_pieces(x_ref, npre_ref, w_ref, p_ref, x_keep_ref=None):
    normed = {}

    def piece(r0, c0):
        def run():
            if r0 not in normed:
                x = x_ref[0, r0:r0 + TILE, :]
                if x_keep_ref is not None:
                    x_keep_ref[r0:r0 + TILE, :] = x
                normed[r0] = (x * _rms_scale(x) * npre_ref[...]).astype(BF16)
            p_ref[r0:r0 + TILE, c0:c0 + PIECE_COLS] = _dot(normed[r0], w_ref[:, c0:c0 + PIECE_COLS])
        return run

    return [piece(r0, c0) for r0 in range(0, p_ref.shape[0], TILE)
            for c0 in range(0, p_ref.shape[1], PIECE_COLS)]


def _emitter(pieces):
    def emit(n):
        for _ in range(min(n, len(pieces))):
            pieces.pop(0)()
    return emit


def _bwd_state_kernel(steps_per_row, n_steps, xn_ref, npre_ref, wkv_ref, wgk_ref, bgk_ref, utri_ref,
                      klr_ref, vs_ref, p0_ref, p1_ref, s_ref):
    i = pl.program_id(0)

    @pl.when(i == 0)
    def _():
        p1_ref[...] = jnp.zeros_like(p1_ref)

    @pl.when(jnp.logical_or(i == 0, (n_steps - i) % steps_per_row == steps_per_row - 1))
    def _():
        s_ref[...] = jnp.zeros_like(s_ref)

    def step(p_write_ref, p_ref):
        emit = _emitter(_projection_pieces(xn_ref, npre_ref, wkv_ref, p_write_ref))
        s = s_ref[...]
        for j in reversed(range(PRE_SUB)):
            r0 = j * TILE
            emit(1)
            k = p_ref[r0:r0 + TILE, 0:D_K]
            vb = p_ref[r0:r0 + TILE, D_K:D_K + D_B].astype(BF16)
            plr = p_ref[r0:r0 + TILE, D_K + D_B:D_K + D_B + LANES]
            klr_ref[j, :, 0:D_K] = k
            klr_ref[j, :, D_K:D_K + LANES] = plr
            vs_ref[j, 0:TILE, :] = vb
            g = _log_sigmoid(_dot(plr.astype(BF16), wgk_ref[...]) + bgk_ref[...]) * (
                1.0 / GATE_NORMALIZER)
            emit(1)
            bcum = _chunk_cumsum(utri_ref, g)
            kdec, dec = _decayed_keys(k, bcum, backward=True)
            kt = jnp.transpose(kdec).astype(BF16)
            emit(1)
            v_par = _parity_masked_values(vb)
            kv = [_chunk_kv(kt, v_par, c) for c in range(N_CHUNK)]
            dec_col = [_decay_column(dec[c]) for c in range(N_CHUNK)]
            emit(1)
            for c in reversed(range(N_CHUNK)):
                vs_ref[j, TILE:TILE + N_HEADS_B * HEAD_K, c * HEAD_V:(c + 1) * HEAD_V] = (
                    s.astype(BF16))
                s = dec_col[c] * s + kv[c]
        s_ref[...] = s

    @pl.when(i % 2 == 0)
    def _():
        step(p0_ref, p1_ref)

    @pl.when(i % 2 == 1)
    def _():
        step(p1_ref, p0_ref)


def _mix(j, p_ref, x_ref, klr_ref, vs_ref, wsp_ref, bsp_ref, gva_ref, wgk_ref, bgk_ref,
         ltri_ref, utri_ref, gnb_ref, wout_ref, npost_ref, out_ref, sf_ref, o_ref, mix_ref, emit):
    t0 = j * TILE

    emit(2)
    vg = _gelu(p_ref[t0:t0 + TILE, M_V:M_ZA])
    vc = vg - jnp.mean(vg, axis=-1, keepdims=True)
    vn = (vc * _rms_scale(vc) * gva_ref[...]).astype(BF16)
    for c in range(TILE // CHUNK_A):
        r0 = c * CHUNK_A
        for hd in range(N_HEADS_A):
            c0 = hd * HEAD_A
            sv = _dot(wsp_ref[hd], vn[r0:r0 + CHUNK_A, c0:c0 + HEAD_A]) + bsp_ref[hd]
            out_a = _gelu(p_ref[t0 + r0:t0 + r0 + CHUNK_A, M_U + c0:M_U + c0 + HEAD_A]) * sv
            out_a = out_a * _silu(
                p_ref[t0 + r0:t0 + r0 + CHUNK_A, M_ZA + c0:M_ZA + c0 + HEAD_A])
            mix_ref[j, r0:r0 + CHUNK_A, c0:c0 + HEAD_A] = out_a.astype(BF16)
            emit(1 if hd == 1 else 0)

    q = p_ref[t0:t0 + TILE, M_Q:M_ZB] * (HEAD_K ** -0.5)
    k = klr_ref[j, :, 0:D_K]
    vb = vs_ref[j, 0:TILE, :]
    plr = klr_ref[j, :, D_K:D_K + LANES]
    emit(1)
    g = _log_sigmoid(_dot(plr.astype(BF16), wgk_ref[...]) + bgk_ref[...]) * (1.0 / GATE_NORMALIZER)
    v_par = _parity_masked_values(vb)

    lane_head = lax.broadcasted_iota(jnp.int32, (CHUNK_B, D_K), 1) // HEAD_K
    lane_pos = lax.broadcasted_iota(jnp.int32, (CHUNK_B, D_K), 1) % CHUNK_B
    row_pos = lax.broadcasted_iota(jnp.int32, (CHUNK_B, D_K), 0)
    zero_blk = jnp.zeros((CHUNK_B, HEAD_V), BF16)

    dirs = []
    for d in range(2):
        backward = d == 1
        emit(1)
        bcum = _chunk_cumsum(utri_ref if backward else ltri_ref, g[:, d * D_K:(d + 1) * D_K])
        q_in = (q * jnp.exp(bcum)).astype(BF16)
        k_in = (k * jnp.exp(-bcum)).astype(BF16)
        keep = (row_pos < lane_pos) if backward else (row_pos >= lane_pos)
        dirs.append((q_in, k_in, keep))
        if not backward:
            kdec, dec = _decayed_keys(k, bcum, backward)
            kt = jnp.transpose(kdec).astype(BF16)

    steps = [(t, d, N_CHUNK - 1 - t if d == 1 else t) for t in range(N_CHUNK) for d in range(2)]
    p_blk, kv_blk, dec_col = {}, {}, {}
    for t, d, c in steps:
        q_in, k_in, keep = dirs[d]
        k_c = k_in[c * CHUNK_B:(c + 1) * CHUNK_B, :]
        k_bd = jnp.concatenate(
            [jnp.where(lane_head == hd, k_c, jnp.zeros_like(k_c)) for hd in range(N_HEADS_B)],
            axis=0)
        scores = _dot_nt(q_in[c * CHUNK_B:(c + 1) * CHUNK_B, :], k_bd)
        p_blk[d, c] = jnp.where(keep, scores, 0.0).astype(BF16)
    for c in range(N_CHUNK):
        kv_blk[c] = _chunk_kv(kt, v_par, c)
        dec_col[c] = _decay_column(dec[c])

    for t, d, c in steps:
        q_in = dirs[d][0]
        r0 = c * CHUNK_B
        if d == 1:
            s_bf = vs_ref[j, TILE:TILE + N_HEADS_B * HEAD_K, c * HEAD_V:(c + 1) * HEAD_V]
        else:
            s_old = sf_ref[...]
            s_bf = s_old.astype(BF16)
        for p in range(N_PAIR):
            a0 = 2 * p * HEAD_V
            rhs = jnp.concatenate([
                jnp.concatenate([vb[r0:r0 + CHUNK_B, a0:a0 + HEAD_V], zero_blk], axis=1),
                jnp.concatenate([zero_blk, vb[r0:r0 + CHUNK_B, a0 + HEAD_V:a0 + 2 * HEAD_V]], axis=1),
                jnp.concatenate([s_bf[2 * p * HEAD_K:(2 * p + 1) * HEAD_K, :], zero_blk], axis=1),
                jnp.concatenate([zero_blk, s_bf[(2 * p + 1) * HEAD_K:(2 * p + 2) * HEAD_K, :]], axis=1),
            ], axis=0)
            lhs = jnp.concatenate([p_blk[d, c][:, p * LANES:(p + 1) * LANES],
                                   q_in[r0:r0 + CHUNK_B, p * LANES:(p + 1) * LANES]], axis=1)
            o_pair = _dot(lhs, rhs)
            if t < N_CHUNK // 2:
                o_ref[j, r0:r0 + CHUNK_B, a0:a0 + 2 * HEAD_V] = o_pair
            else:
                o_ref[j, r0:r0 + CHUNK_B, a0:a0 + 2 * HEAD_V] += o_pair
        if d == 0:
            sf_ref[...] = dec_col[c] * s_old + kv_blk[c]

    emit(1)
    for hd in range(N_HEADS_B):
        c0 = hd * HEAD_V
        oh = o_ref[j, :, c0:c0 + HEAD_V]
        out_b = oh * _rms_scale(oh) * gnb_ref[...] * _silu(
            p_ref[t0:t0 + TILE, M_ZB + c0:M_ZB + c0 + HEAD_V])
        mix_ref[j, :, D_A + c0:D_A + c0 + HEAD_V] = out_b.astype(BF16)

    mixed = jnp.concatenate(
        [_dot(mix_ref[j], wout_ref[:, c0:c0 + PIECE_COLS]) for c0 in range(0, D_MODEL, PIECE_COLS)],
        axis=1)
    emit(1)
    out_ref[0, t0:t0 + TILE, :] = (
        x_ref[t0:t0 + TILE, :] + mixed * _rms_scale(mixed) * npost_ref[...])


def _layer_kernel(steps_per_row, xn_ref, klr_ref, vs_ref, npre_ref, win_ref,
                  wsp_ref, bsp_ref, gva_ref, wgk_ref, bgk_ref, ltri_ref, utri_ref, gnb_ref,
                  wout_ref, npost_ref, out_ref, p0_ref, p1_ref, x0_ref, x1_ref, sf_ref, o_ref, mix_ref):
    i = pl.program_id(0)

    @pl.when(i == 0)
    def _():
        p1_ref[...] = jnp.zeros_like(p1_ref)
        x1_ref[...] = jnp.zeros_like(x1_ref)

    @pl.when(jnp.logical_or(i == 0, (i - 1) % steps_per_row == 0))
    def _():
        sf_ref[...] = jnp.zeros_like(sf_ref)

    def step(p_write_ref, p_read_ref, x_write_ref, x_read_ref):
        emit = _emitter(_projection_pieces(xn_ref, npre_ref, win_ref, p_write_ref, x_write_ref))
        for j in range(MAIN_SUB):
            _mix(j, p_read_ref, x_read_ref, klr_ref, vs_ref, wsp_ref, bsp_ref, gva_ref, wgk_ref,
                 bgk_ref, ltri_ref, utri_ref, gnb_ref, wout_ref, npost_ref, out_ref, sf_ref, o_ref,
                 mix_ref, emit)
        emit(MAIN_SUB * P_COLS // PIECE_COLS)

    @pl.when(i % 2 == 0)
    def _():
        step(p0_ref, p1_ref, x0_ref, x1_ref)

    @pl.when(i % 2 == 1)
    def _():
        step(p1_ref, p0_ref, x1_ref, x0_ref)


def _const_spec(shape):
    return pl.BlockSpec(shape, lambda *_: (0,) * len(shape))


def _hybrid_layer(x, wts):
    bsz, seq, _ = x.shape
    n_tiles = seq // TILE
    n_total = bsz * n_tiles

    params = pltpu.CompilerParams(dimension_semantics=("arbitrary",), vmem_limit_bytes=VMEM_LIMIT)

    n_steps = n_total // PRE_SUB

    def recurred_block(i):
        return (jnp.minimum(n_steps - i, n_steps - 1), 0, 0)

    klr_tiles, vs_tiles = pl.pallas_call(
        functools.partial(_bwd_state_kernel, n_tiles // PRE_SUB, n_steps),
        grid=(n_steps + 1,),
        in_specs=[
            pl.BlockSpec((1, PRE_SUB * TILE, D_MODEL),
                         lambda i: (jnp.maximum(n_steps - 1 - i, 0), 0, 0)),
            _const_spec((1, D_MODEL)),
            _const_spec((D_MODEL, PRE_COLS + PIECE_COLS)),
            _const_spec((LANES, D_K)),
            _const_spec((1, D_K)),
            _const_spec((TILE, TILE)),
        ],
        out_specs=[
            pl.BlockSpec((PRE_SUB, TILE, D_K + LANES), recurred_block),
            pl.BlockSpec((PRE_SUB, TILE + N_HEADS_B * HEAD_K, D_B), recurred_block),
        ],
        out_shape=[
            jax.ShapeDtypeStruct((n_total, TILE, D_K + LANES), F32),
            jax.ShapeDtypeStruct((n_total, TILE + N_HEADS_B * HEAD_K, D_B), BF16),
        ],
        scratch_shapes=[
            pltpu.VMEM((PRE_SUB * TILE, PRE_COLS), F32),
            pltpu.VMEM((PRE_SUB * TILE, PRE_COLS), F32),
            pltpu.VMEM((N_HEADS_B * HEAD_K, HEAD_V), F32),
        ],
        compiler_params=params,
    )(x.reshape(n_steps, PRE_SUB * TILE, D_MODEL), wts["npre"], wts["wkv"], wts["wgk_b"],
      wts["bgk_b"], wts["utri"])

    m_steps = n_total // MAIN_SUB

    def finished_block(i):
        return (jnp.maximum(i - 1, 0), 0, 0)

    y = pl.pallas_call(
        functools.partial(_layer_kernel, n_tiles // MAIN_SUB),
        grid=(m_steps + 1,),
        in_specs=[
            pl.BlockSpec((1, MAIN_SUB * TILE, D_MODEL),
                         lambda i: (jnp.minimum(i, m_steps - 1), 0, 0)),
            pl.BlockSpec((MAIN_SUB, TILE, D_K + LANES), finished_block),
            pl.BlockSpec((MAIN_SUB, TILE + N_HEADS_B * HEAD_K, D_B), finished_block),
            _const_spec((1, D_MODEL)),
            _const_spec((D_MODEL, P_COLS)),
            _const_spec((N_HEADS_A, CHUNK_A, CHUNK_A)),
            _const_spec((N_HEADS_A, CHUNK_A, HEAD_A)),
            _const_spec((1, D_A)),
            _const_spec((LANES, 2 * D_K)),
            _const_spec((1, 2 * D_K)),
            _const_spec((TILE, TILE)),
            _const_spec((TILE, TILE)),
            _const_spec((1, HEAD_V)),
            _const_spec((D_A + D_B, D_MODEL + PIECE_COLS)),
            _const_spec((1, D_MODEL)),
        ],
        out_specs=pl.BlockSpec((1, MAIN_SUB * TILE, D_MODEL), finished_block),
        out_shape=jax.ShapeDtypeStruct((m_steps, MAIN_SUB * TILE, D_MODEL), x.dtype),
        scratch_shapes=[
            pltpu.VMEM((MAIN_SUB * TILE, P_COLS), F32),
            pltpu.VMEM((MAIN_SUB * TILE, P_COLS), F32),
            pltpu.VMEM((MAIN_SUB * TILE, D_MODEL), F32),
            pltpu.VMEM((MAIN_SUB * TILE, D_MODEL), F32),
            pltpu.VMEM((N_HEADS_B * HEAD_K, HEAD_V), F32),
            pltpu.VMEM((MAIN_SUB, TILE, D_B), F32),
            pltpu.VMEM((MAIN_SUB, TILE, D_A + D_B), BF16),
        ],
        compiler_params=params,
    )(x.reshape(m_steps, MAIN_SUB * TILE, D_MODEL), klr_tiles, vs_tiles, wts["npre"], wts["win"],
      wts["wsp"], wts["bsp"], wts["gva"], wts["wgk"], wts["bgk"], wts["ltri"], wts["utri"],
      wts["gnb"], wts["wout"], wts["npost"])
    return y.reshape(x.shape)


def _prepare_weights(norm_pre, w_in, w_sp, b_sp, g_v_a, w_gk_fwd, b_gk_fwd, w_gk_bwd, b_gk_bwd,
                     g_norm_b, w_out, norm_post):
    w_in_bf = w_in.astype(BF16)
    wlr = jnp.zeros((D_MODEL, PIECE_COLS), BF16).at[:, 0:2 * GATE_RANK].set(w_in_bf[:, OFF_LRF:D_IN])
    wgk = jnp.zeros((LANES, 2 * D_K), BF16)
    wgk = wgk.at[0:GATE_RANK, 0:D_K].set(w_gk_fwd.astype(BF16))
    wgk = wgk.at[GATE_RANK:2 * GATE_RANK, D_K:2 * D_K].set(w_gk_bwd.astype(BF16))
    pad_cols = jnp.zeros((D_MODEL, PIECE_COLS), BF16)
    pos = jnp.arange(TILE)
    same_chunk = (pos[:, None] // CHUNK_B) == (pos[None, :] // CHUNK_B)
    ltri = (same_chunk & (pos[None, :] <= pos[:, None])).astype(BF16)
    utri = (same_chunk & (pos[None, :] >= pos[:, None])).astype(BF16)
    return {
        "npre": norm_pre.reshape(1, D_MODEL),
        "win": jnp.concatenate([w_in_bf[:, OFF_U:OFF_K], w_in_bf[:, OFF_ZB:OFF_LRF]], axis=1),
        "wkv": jnp.concatenate([w_in_bf[:, OFF_K:OFF_ZB], wlr, pad_cols], axis=1),
        "wsp": w_sp.astype(BF16),
        "bsp": jnp.broadcast_to(b_sp[:, :, None], (N_HEADS_A, CHUNK_A, HEAD_A)),
        "gva": g_v_a.reshape(1, D_A),
        "wgk": wgk,
        "wgk_b": wgk[:, D_K:2 * D_K],
        "bgk": jnp.concatenate([b_gk_fwd, b_gk_bwd]).reshape(1, 2 * D_K),
        "bgk_b": b_gk_bwd.reshape(1, D_K),
        "ltri": ltri,
        "utri": utri,
        "gnb": g_norm_b.reshape(1, HEAD_V),
        "wout": jnp.concatenate([w_out.astype(BF16), pad_cols], axis=1),
        "npost": norm_post.reshape(1, D_MODEL),
    }


def kernel(x_prompt, x_sample, norm_pre, w_in, w_sp, b_sp, g_v_a, w_gk_fwd, b_gk_fwd,
           w_gk_bwd, b_gk_bwd, g_norm_b, w_out, norm_post):
    y_prompt, y_sample = x_prompt, x_sample
    for l in range(norm_pre.shape[0]):
        wts = _prepare_weights(norm_pre[l], w_in[l], w_sp[l], b_sp[l], g_v_a[l], w_gk_fwd[l],
                               b_gk_fwd[l], w_gk_bwd[l], b_gk_bwd[l], g_norm_b[l], w_out[l],
                               norm_post[l])
        y_prompt = _hybrid_layer(y_prompt, wts)
        y_sample = _hybrid_layer(y_sample, wts)
    return (y_prompt, y_sample)
```

```python
import functools

import jax
import jax.numpy as jnp
from jax import lax
from jax.experimental import pallas as pl
from jax.experimental.pallas import tpu as pltpu

F32 = jnp.float32
BF16 = jnp.bfloat16

D_MODEL = 1024
D_A = 512
N_HEADS_A = 4
HEAD_A = 128
CHUNK_A = 128
D_B = 512
N_HEADS_B = 4
HEAD_V = 128
D_K = 256
HEAD_K = 64
GATE_RANK = 16
GATE_NORMALIZER = 16.0
CHUNK_B = 64
EPS = 1e-6

OFF_U, OFF_V, OFF_ZA, OFF_Q, OFF_K, OFF_VB, OFF_ZB, OFF_LRF, OFF_LRB, D_IN = (
    0, 512, 1024, 1536, 1792, 2048, 2560, 3072, 3088, 3104)

LANES = 128
TILE = 256
N_CHUNK = TILE // CHUNK_B
N_PAIR = N_HEADS_B // 2
PIECE_COLS = 256
PRE_COLS = D_K + D_B + PIECE_COLS
PRE_SUB = 8
MAIN_SUB = 2
M_U, M_V, M_ZA, M_Q, M_ZB, P_COLS = 0, 512, 1024, 1536, 1792, 2304
VMEM_LIMIT = 56 * 1024 * 1024


def _dot(a, b):
    return jnp.dot(a, b, preferred_element_type=F32)


def _dot_nt(a, b):
    return lax.dot_general(a, b, (((1,), (1,)), ((), ())), preferred_element_type=F32)


def _gelu(x):
    return 0.5 * x * (1.0 + jnp.tanh(0.7978845608028654 * (x + 0.044715 * (x * x * x))))


def _silu(x):
    return x * (0.5 * jnp.tanh(0.5 * x) + 0.5)


def _log_sigmoid(x):
    return jnp.minimum(x, 0.0) - jnp.log1p(jnp.exp(-jnp.abs(x)))


def _rms_scale(x):
    return lax.rsqrt(jnp.mean(x * x, axis=-1, keepdims=True) + EPS)


def _chunk_cumsum(tri_ref, g):
    hi = g.astype(BF16)
    lo = (g - hi.astype(F32)).astype(BF16)
    tri = tri_ref[...]
    return _dot(tri, hi) + _dot(tri, lo)


def _decayed_keys(k, bcum, backward):
    kd, dec = [], []
    for c in range(N_CHUNK):
        r0 = c * CHUNK_B
        last = r0 if backward else r0 + CHUNK_B - 1
        b_last = bcum[last:last + 1, :]
        kd.append(k[r0:r0 + CHUNK_B, :] * jnp.exp(b_last - bcum[r0:r0 + CHUNK_B, :]))
        dec.append(jnp.exp(b_last))
    return jnp.concatenate(kd, axis=0), dec


def _chunk_kv(kt, v_par, c):
    cp, par = divmod(c, 2)
    parts = []
    for p in range(N_PAIR):
        full = _dot(kt[p * LANES:(p + 1) * LANES, cp * LANES:(cp + 1) * LANES],
                    v_par[cp][par][:, p * 2 * HEAD_V:(p + 1) * 2 * HEAD_V])
        parts.append(full[0:HEAD_K, 0:HEAD_V])
        parts.append(full[HEAD_K:2 * HEAD_K, HEAD_V:2 * HEAD_V])
    return jnp.concatenate(parts, axis=0)


def _decay_column(dec_row):
    return jnp.transpose(jnp.broadcast_to(dec_row, (LANES, D_K)))


def _parity_masked_values(vb):
    out = []
    row = lax.broadcasted_iota(jnp.int32, (2 * CHUNK_B, D_B), 0)
    for cp in range(N_CHUNK // 2):
        vp = vb[cp * 2 * CHUNK_B:(cp + 1) * 2 * CHUNK_B, :]
        zero = jnp.zeros_like(vp)
        out.append((jnp.where(row < CHUNK_B, vp, zero), jnp.where(row >= CHUNK_B, vp, zero)))
    return out


def _projection_pieces(x_ref, npre_ref, w_ref, p_ref, x_keep_ref=None):
    normed = {}

    def piece(r0, c0):
        def run():
            if r0 not in normed:
                x = x_ref[0, r0:r0 + TILE, :]
                if x_keep_ref is not None:
                    x_keep_ref[r0:r0 + TILE, :] = x
                normed[r0] = (x * _rms_scale(x) * npre_ref[...]).astype(BF16)
            p_ref[r0:r0 + TILE, c0:c0 + PIECE_COLS] = _dot(normed[r0], w_ref[:, c0:c0 + PIECE_COLS])
        return run

    return [piece(r0, c0) for r0 in range(0, p_ref.shape[0], TILE)
            for c0 in range(0, p_ref.shape[1], PIECE_COLS)]


def _emitter(pieces):
    def emit(n):
        for _ in range(min(n, len(pieces))):
            pieces.pop(0)()
    return emit


def _bwd_state_kernel(steps_per_row, n_steps, xn_ref, npre_ref, wkv_ref, wgk_ref, bgk_ref, utri_ref,
                      klr_ref, vs_ref, p0_ref, p1_ref, s_ref):
    i = pl.program_id(0)

    @pl.when(i == 0)
    def _():
        p1_ref[...] = jnp.zeros_like(p1_ref)

    @pl.when(jnp.logical_or(i == 0, (n_steps - i) % steps_per_row == steps_per_row - 1))
    def _():
        s_ref[...] = jnp.zeros_like(s_ref)

    def step(p_write_ref, p_ref):
        emit = _emitter(_projection_pieces(xn_ref, npre_ref, wkv_ref, p_write_ref))
        s = s_ref[...]
        for j in reversed(range(PRE_SUB)):
            r0 = j * TILE
            emit(1)
            k = p_ref[r0:r0 + TILE, 0:D_K]
            vb = p_ref[r0:r0 + TILE, D_K:D_K + D_B].astype(BF16)
            plr = p_ref[r0:r0 + TILE, D_K + D_B:D_K + D_B + LANES]
            klr_ref[j, :, 0:D_K] = k
            klr_ref[j, :, D_K:D_K + LANES] = plr
            vs_ref[j, 0:TILE, :] = vb
            g = _log_sigmoid(_dot(plr.astype(BF16), wgk_ref[...]) + bgk_ref[...]) * (
                1.0 / GATE_NORMALIZER)
            emit(1)
            bcum = _chunk_cumsum(utri_ref, g)
            kdec, dec = _decayed_keys(k, bcum, backward=True)
            kt = jnp.transpose(kdec).astype(BF16)
            emit(1)
            v_par = _parity_masked_values(vb)
            kv = [_chunk_kv(kt, v_par, c) for c in range(N_CHUNK)]
            dec_col = [_decay_column(dec[c]) for c in range(N_CHUNK)]
            emit(1)
            for c in reversed(range(N_CHUNK)):
                vs_ref[j, TILE:TILE + N_HEADS_B * HEAD_K, c * HEAD_V:(c + 1) * HEAD_V] = (
                    s.astype(BF16))
                s = dec_col[c] * s + kv[c]
        s_ref[...] = s

    @pl.when(i % 2 == 0)
    def _():
        step(p0_ref, p1_ref)

    @pl.when(i % 2 == 1)
    def _():
        step(p1_ref, p0_ref)


def _mix(j, p_ref, x_ref, klr_ref, vs_ref, wsp_ref, bsp_ref, gva_ref, wgk_ref, bgk_ref,
         ltri_ref, utri_ref, gnb_ref, wout_ref, npost_ref, out_ref, sf_ref, o_ref, mix_ref, emit):
    t0 = j * TILE

    emit(2)
    vg = _gelu(p_ref[t0:t0 + TILE, M_V:M_ZA])
    vc = vg - jnp.mean(vg, axis=-1, keepdims=True)
    vn = (vc * _rms_scale(vc) * gva_ref[...]).astype(BF16)
    for c in range(TILE // CHUNK_A):
        r0 = c * CHUNK_A
        for hd in range(N_HEADS_A):
            c0 = hd * HEAD_A
            sv = _dot(wsp_ref[hd], vn[r0:r0 + CHUNK_A, c0:c0 + HEAD_A]) + bsp_ref[hd]
            out_a = _gelu(p_ref[t0 + r0:t0 + r0 + CHUNK_A, M_U + c0:M_U + c0 + HEAD_A]) * sv
            out_a = out_a * _silu(
                p_ref[t0 + r0:t0 + r0 + CHUNK_A, M_ZA + c0:M_ZA + c0 + HEAD_A])
            mix_ref[j, r0:r0 + CHUNK_A, c0:c0 + HEAD_A] = out_a.astype(BF16)
            emit(1 if hd == 1 else 0)

    q = p_ref[t0:t0 + TILE, M_Q:M_ZB] * (HEAD_K ** -0.5)
    k = klr_ref[j, :, 0:D_K]
    vb = vs_ref[j, 0:TILE, :]
    plr = klr_ref[j, :, D_K:D_K + LANES]
    emit(1)
    g = _log_sigmoid(_dot(plr.astype(BF16), wgk_ref[...]) + bgk_ref[...]) * (1.0 / GATE_NORMALIZER)
    v_par = _parity_masked_values(vb)

    lane_head = lax.broadcasted_iota(jnp.int32, (CHUNK_B, D_K), 1) // HEAD_K
    lane_pos = lax.broadcasted_iota(jnp.int32, (CHUNK_B, D_K), 1) % CHUNK_B
    row_pos = lax.broadcasted_iota(jnp.int32, (CHUNK_B, D_K), 0)
    zero_blk = jnp.zeros((CHUNK_B, HEAD_V), BF16)

    dirs = []
    for d in range(2):
        backward = d == 1
        emit(1)
        bcum = _chunk_cumsum(utri_ref if backward else ltri_ref, g[:, d * D_K:(d + 1) * D_K])
        q_in = (q * jnp.exp(bcum)).astype(BF16)
        k_in = (k * jnp.exp(-bcum)).astype(BF16)
        keep = (row_pos < lane_pos) if backward else (row_pos >= lane_pos)
        dirs.append((q_in, k_in, keep))
        if not backward:
            kdec, dec = _decayed_keys(k, bcum, backward)
            kt = jnp.transpose(kdec).astype(BF16)

    steps = [(t, d, N_CHUNK - 1 - t if d == 1 else t) for t in range(N_CHUNK) for d in range(2)]
    p_blk, kv_blk, dec_col = {}, {}, {}
    for t, d, c in steps:
        q_in, k_in, keep = dirs[d]
        k_c = k_in[c * CHUNK_B:(c + 1) * CHUNK_B, :]
        k_bd = jnp.concatenate(
            [jnp.where(lane_head == hd, k_c, jnp.zeros_like(k_c)) for hd in range(N_HEADS_B)],
            axis=0)
        scores = _dot_nt(q_in[c * CHUNK_B:(c + 1) * CHUNK_B, :], k_bd)
        p_blk[d, c] = jnp.where(keep, scores, 0.0).astype(BF16)
    for c in range(N_CHUNK):
        kv_blk[c] = _chunk_kv(kt, v_par, c)
        dec_col[c] = _decay_column(dec[c])

    for t, d, c in steps:
        q_in = dirs[d][0]
        r0 = c * CHUNK_B
        if d == 1:
            s_bf = vs_ref[j, TILE:TILE + N_HEADS_B * HEAD_K, c * HEAD_V:(c + 1) * HEAD_V]
        else:
            s_old = sf_ref[...]
            s_bf = s_old.astype(BF16)
        for p in range(N_PAIR):
            a0 = 2 * p * HEAD_V
            rhs = jnp.concatenate([
                jnp.concatenate([vb[r0:r0 + CHUNK_B, a0:a0 + HEAD_V], zero_blk], axis=1),
                jnp.concatenate([zero_blk, vb[r0:r0 + CHUNK_B, a0 + HEAD_V:a0 + 2 * HEAD_V]], axis=1),
                jnp.concatenate([s_bf[2 * p * HEAD_K:(2 * p + 1) * HEAD_K, :], zero_blk], axis=1),
                jnp.concatenate([zero_blk, s_bf[(2 * p + 1) * HEAD_K:(2 * p + 2) * HEAD_K, :]], axis=1),
            ], axis=0)
            lhs = jnp.concatenate([p_blk[d, c][:, p * LANES:(p + 1) * LANES],
                                   q_in[r0:r0 + CHUNK_B, p * LANES:(p + 1) * LANES]], axis=1)
            o_pair = _dot(lhs, rhs)
            if t < N_CHUNK // 2:
                o_ref[j, r0:r0 + CHUNK_B, a0:a0 + 2 * HEAD_V] = o_pair
            else:
                o_ref[j, r0:r0 + CHUNK_B, a0:a0 + 2 * HEAD_V] += o_pair
        if d == 0:
            sf_ref[...] = dec_col[c] * s_old + kv_blk[c]

    emit(1)
    for hd in range(N_HEADS_B):
        c0 = hd * HEAD_V
        oh = o_ref[j, :, c0:c0 + HEAD_V]
        out_b = oh * _rms_scale(oh) * gnb_ref[...] * _silu(
            p_ref[t0:t0 + TILE, M_ZB + c0:M_ZB + c0 + HEAD_V])
        mix_ref[j, :, D_A + c0:D_A + c0 + HEAD_V] = out_b.astype(BF16)

    mixed = jnp.concatenate(
        [_dot(mix_ref[j], wout_ref[:, c0:c0 + PIECE_COLS]) for c0 in range(0, D_MODEL, PIECE_COLS)],
        axis=1)
    emit(1)
    out_ref[0, t0:t0 + TILE, :] = (
        x_ref[t0:t0 + TILE, :] + mixed * _rms_scale(mixed) * npost_ref[...])


def _layer_kernel(steps_per_row, xn_ref, klr_ref, vs_ref, npre_ref, win_ref,
                  wsp_ref, bsp_ref, gva_ref, wgk_ref, bgk_ref, ltri_ref, utri_ref, gnb_ref,
                  wout_ref, npost_ref, out_ref, p0_ref, p1_ref, x0_ref, x1_ref, sf_ref, o_ref, mix_ref):
    i = pl.program_id(0)

    @pl.when(i == 0)
    def _():
        p1_ref[...] = jnp.zeros_like(p1_ref)
        x1_ref[...] = jnp.zeros_like(x1_ref)

    @pl.when(jnp.logical_or(i == 0, (i - 1) % steps_per_row == 0))
    def _():
        sf_ref[...] = jnp.zeros_like(sf_ref)

    def step(p_write_ref, p_read_ref, x_write_ref, x_read_ref):
        emit = _emitter(_projection_pieces(xn_ref, npre_ref, win_ref, p_write_ref, x_write_ref))
        for j in range(MAIN_SUB):
            _mix(j, p_read_ref, x_read_ref, klr_ref, vs_ref, wsp_ref, bsp_ref, gva_ref, wgk_ref,
                 bgk_ref, ltri_ref, utri_ref, gnb_ref, wout_ref, npost_ref, out_ref, sf_ref, o_ref,
                 mix_ref, emit)
        emit(MAIN_SUB * P_COLS // PIECE_COLS)

    @pl.when(i % 2 == 0)
    def _():
        step(p0_ref, p1_ref, x0_ref, x1_ref)

    @pl.when(i % 2 == 1)
    def _():
        step(p1_ref, p0_ref, x1_ref, x0_ref)


def _const_spec(shape):
    return pl.BlockSpec(shape, lambda *_: (0,) * len(shape))


def _hybrid_layer(x, wts):
    bsz, seq, _ = x.shape
    n_tiles = seq // TILE
    n_total = bsz * n_tiles

    params = pltpu.CompilerParams(dimension_semantics=("arbitrary",), vmem_limit_bytes=VMEM_LIMIT)

    n_steps = n_total // PRE_SUB

    def recurred_block(i):
        return (jnp.minimum(n_steps - i, n_steps - 1), 0, 0)

    klr_tiles, vs_tiles = pl.pallas_call(
        functools.partial(_bwd_state_kernel, n_tiles // PRE_SUB, n_steps),
        grid=(n_steps + 1,),
        in_specs=[
            pl.BlockSpec((1, PRE_SUB * TILE, D_MODEL),
                         lambda i: (jnp.maximum(n_steps - 1 - i, 0), 0, 0)),
            _const_spec((1, D_MODEL)),
            _const_spec((D_MODEL, PRE_COLS + PIECE_COLS)),
            _const_spec((LANES, D_K)),
            _const_spec((1, D_K)),
            _const_spec((TILE, TILE)),
        ],
        out_specs=[
            pl.BlockSpec((PRE_SUB, TILE, D_K + LANES), recurred_block),
            pl.BlockSpec((PRE_SUB, TILE + N_HEADS_B * HEAD_K, D_B), recurred_block),
        ],
        out_shape=[
            jax.ShapeDtypeStruct((n_total, TILE, D_K + LANES), F32),
            jax.ShapeDtypeStruct((n_total, TILE + N_HEADS_B * HEAD_K, D_B), BF16),
        ],
        scratch_shapes=[
            pltpu.VMEM((PRE_SUB * TILE, PRE_COLS), F32),
            pltpu.VMEM((PRE_SUB * TILE, PRE_COLS), F32),
            pltpu.VMEM((N_HEADS_B * HEAD_K, HEAD_V), F32),
        ],
        compiler_params=params,
    )(x.reshape(n_steps, PRE_SUB * TILE, D_MODEL), wts["npre"], wts["wkv"], wts["wgk_b"],
      wts["bgk_b"], wts["utri"])

    m_steps = n_total // MAIN_SUB

    def finished_block(i):
        return (jnp.maximum(i - 1, 0), 0, 0)

    y = pl.pallas_call(
        functools.partial(_layer_kernel, n_tiles // MAIN_SUB),
        grid=(m_steps + 1,),
        in_specs=[
            pl.BlockSpec((1, MAIN_SUB * TILE, D_MODEL),
                         lambda i: (jnp.minimum(i, m_steps - 1), 0, 0)),
            pl.BlockSpec((MAIN_SUB, TILE, D_K + LANES), finished_block),
            pl.BlockSpec((MAIN_SUB, TILE + N_HEADS_B * HEAD_K, D_B), finished_block),
            _const_spec((1, D_MODEL)),
            _const_spec((D_MODEL, P_COLS)),
            _const_spec((N_HEADS_A, CHUNK_A, CHUNK_A)),
            _const_spec((N_HEADS_A, CHUNK_A, HEAD_A)),
            _const_spec((1, D_A)),
            _const_spec((LANES, 2 * D_K)),
            _const_spec((1, 2 * D_K)),
            _const_spec((TILE, TILE)),
            _const_spec((TILE, TILE)),
            _const_spec((1, HEAD_V)),
            _const_spec((D_A + D_B, D_MODEL + PIECE_COLS)),
            _const_spec((1, D_MODEL)),
        ],
        out_specs=pl.BlockSpec((1, MAIN_SUB * TILE, D_MODEL), finished_block),
        out_shape=jax.ShapeDtypeStruct((m_steps, MAIN_SUB * TILE, D_MODEL), x.dtype),
        scratch_shapes=[
            pltpu.VMEM((MAIN_SUB * TILE, P_COLS), F32),
            pltpu.VMEM((MAIN_SUB * TILE, P_COLS), F32),
            pltpu.VMEM((MAIN_SUB * TILE, D_MODEL), F32),
            pltpu.VMEM((MAIN_SUB * TILE, D_MODEL), F32),
            pltpu.VMEM((N_HEADS_B * HEAD_K, HEAD_V), F32),
            pltpu.VMEM((MAIN_SUB, TILE, D_B), F32),
            pltpu.VMEM((MAIN_SUB, TILE, D_A + D_B), BF16),
        ],
        compiler_params=params,
    )(x.reshape(m_steps, MAIN_SUB * TILE, D_MODEL), klr_tiles, vs_tiles, wts["npre"], wts["win"],
      wts["wsp"], wts["bsp"], wts["gva"], wts["wgk"], wts["bgk"], wts["ltri"], wts["utri"],
      wts["gnb"], wts["wout"], wts["npost"])
    return y.reshape(x.shape)


def _prepare_weights(norm_pre, w_in, w_sp, b_sp, g_v_a, w_gk_fwd, b_gk_fwd, w_gk_bwd, b_gk_bwd,
                     g_norm_b, w_out, norm_post):
    def w_in_cols(lo, hi):
        return w_in[:, lo:hi].astype(BF16)

    wlr = jnp.pad(w_in_cols(OFF_LRF, D_IN), ((0, 0), (0, PIECE_COLS - 2 * GATE_RANK)))
    wgk = jnp.zeros((LANES, 2 * D_K), BF16)
    wgk = wgk.at[0:GATE_RANK, 0:D_K].set(w_gk_fwd.astype(BF16))
    wgk = wgk.at[GATE_RANK:2 * GATE_RANK, D_K:2 * D_K].set(w_gk_bwd.astype(BF16))
    pad_cols = jnp.zeros((D_MODEL, PIECE_COLS), BF16)
    pos = jnp.arange(TILE)
    same_chunk = (pos[:, None] // CHUNK_B) == (pos[None, :] // CHUNK_B)
    ltri = (same_chunk & (pos[None, :] <= pos[:, None])).astype(BF16)
    utri = (same_chunk & (pos[None, :] >= pos[:, None])).astype(BF16)
    return {
        "npre": norm_pre.reshape(1, D_MODEL),
        "win": jnp.concatenate([w_in_cols(OFF_U, OFF_K), w_in_cols(OFF_ZB, OFF_LRF)], axis=1),
        "wkv": jnp.concatenate([w_in_cols(OFF_K, OFF_ZB), wlr, pad_cols], axis=1),
        "wsp": w_sp.astype(BF16),
        "bsp": jnp.broadcast_to(b_sp[:, :, None], (N_HEADS_A, CHUNK_A, HEAD_A)),
        "gva": g_v_a.reshape(1, D_A),
        "wgk": wgk,
        "wgk_b": wgk[:, D_K:2 * D_K],
        "bgk": jnp.concatenate([b_gk_fwd, b_gk_bwd]).reshape(1, 2 * D_K),
        "bgk_b": b_gk_bwd.reshape(1, D_K),
        "ltri": ltri,
        "utri": utri,
        "gnb": g_norm_b.reshape(1, HEAD_V),
        "wout": jnp.concatenate([w_out.astype(BF16), pad_cols], axis=1),
        "npost": norm_post.reshape(1, D_MODEL),
    }


def kernel(x_prompt, x_sample, norm_pre, w_in, w_sp, b_sp, g_v_a, w_gk_fwd, b_gk_fwd,
           w_gk_bwd, b_gk_bwd, g_norm_b, w_out, norm_post):
    y_prompt, y_sample = x_prompt, x_sample
    for l in range(norm_pre.shape[0]):
        wts = _prepare_weights(norm_pre[l], w_in[l], w_sp[l], b_sp[l], g_v_a[l], w_gk_fwd[l],
                               b_gk_fwd[l], w_gk_bwd[l], b_gk_bwd[l], g_norm_b[l], w_out[l],
                               norm_post[l])
        y_prompt = _hybrid_layer(y_prompt, wts)
        y_sample = _hybrid_layer(y_sample, wts)
    return (y_prompt, y_sample)
```

```python
import functools

import jax
import jax.numpy as jnp
from jax import lax
from jax.experimental import pallas as pl
from jax.experimental.pallas import tpu as pltpu

F32 = jnp.float32
BF16 = jnp.bfloat16

D_MODEL = 1024
D_A = 512
N_HEADS_A = 4
HEAD_A = 128
CHUNK_A = 128
D_B = 512
N_HEADS_B = 4
HEAD_V = 128
D_K = 256
HEAD_K = 64
GATE_RANK = 16
GATE_NORMALIZER = 16.0
CHUNK_B = 64
EPS = 1e-6

OFF_U, OFF_V, OFF_ZA, OFF_Q, OFF_K, OFF_VB, OFF_ZB, OFF_LRF, OFF_LRB, D_IN = (
    0, 512, 1024, 1536, 1792, 2048, 2560, 3072, 3088, 3104)

LANES = 128
TILE = 256
N_CHUNK = TILE // CHUNK_B
N_PAIR = N_HEADS_B // 2
PIECE_COLS = 256
PRE_COLS = D_K + D_B + PIECE_COLS
PRE_SUB = 4
MAIN_SUB = 2
M_U, M_V, M_ZA, M_Q, M_ZB, P_COLS = 0, 512, 1024, 1536, 1792, 2304
VMEM_LIMIT = 56 * 1024 * 1024


def _dot(a, b):
    return jnp.dot(a, b, preferred_element_type=F32)


def _dot_nt(a, b):
    return lax.dot_general(a, b, (((1,), (1,)), ((), ())), preferred_element_type=F32)


def _gelu(x):
    return 0.5 * x * (1.0 + jnp.tanh(0.7978845608028654 * (x + 0.044715 * (x * x * x))))


def _silu(x):
    return x * (0.5 * jnp.tanh(0.5 * x) + 0.5)


def _log_sigmoid(x):
    return jnp.minimum(x, 0.0) - jnp.log1p(jnp.exp(-jnp.abs(x)))


def _rms_scale(x):
    return lax.rsqrt(jnp.mean(x * x, axis=-1, keepdims=True) + EPS)


def _chunk_cumsum(tri_ref, g):
    hi = g.astype(BF16)
    lo = (g - hi.astype(F32)).astype(BF16)
    tri = tri_ref[...]
    return _dot(tri, hi) + _dot(tri, lo)


def _decayed_keys(k, bcum, backward):
    kd, dec = [], []
    for c in range(N_CHUNK):
        r0 = c * CHUNK_B
        last = r0 if backward else r0 + CHUNK_B - 1
        b_last = bcum[last:last + 1, :]
        kd.append(k[r0:r0 + CHUNK_B, :] * jnp.exp(b_last - bcum[r0:r0 + CHUNK_B, :]))
        dec.append(jnp.exp(b_last))
    return jnp.concatenate(kd, axis=0), dec


def _chunk_kv(kt, v_par, c):
    cp, par = divmod(c, 2)
    parts = []
    for p in range(N_PAIR):
        full = _dot(kt[p * LANES:(p + 1) * LANES, cp * LANES:(cp + 1) * LANES],
                    v_par[cp][par][:, p * 2 * HEAD_V:(p + 1) * 2 * HEAD_V])
        parts.append(full[0:HEAD_K, 0:HEAD_V])
        parts.append(full[HEAD_K:2 * HEAD_K, HEAD_V:2 * HEAD_V])
    return jnp.concatenate(parts, axis=0)


def _decay_column(dec_row):
    return jnp.transpose(jnp.broadcast_to(dec_row, (LANES, D_K)))


def _parity_masked_values(vb):
    out = []
    row = lax.broadcasted_iota(jnp.int32, (2 * CHUNK_B, D_B), 0)
    for cp in range(N_CHUNK // 2):
        vp = vb[cp * 2 * CHUNK_B:(cp + 1) * 2 * CHUNK_B, :]
        zero = jnp.zeros_like(vp)
        out.append((jnp.where(row < CHUNK_B, vp, zero), jnp.where(row >= CHUNK_B, vp, zero)))
    return out


def _projection_pieces(x_ref, npre_ref, w_ref, p_ref, x_keep_ref=None):
    normed = {}

    def piece(r0, c0):
        def run():
            if r0 not in normed:
                x = x_ref[0, r0:r0 + TILE, :]
                if x_keep_ref is not None:
                    x_keep_ref[r0:r0 + TILE, :] = x
                normed[r0] = (x * _rms_scale(x) * npre_ref[...]).astype(BF16)
            p_ref[r0:r0 + TILE, c0:c0 + PIECE_COLS] = _dot(normed[r0], w_ref[:, c0:c0 + PIECE_COLS])
        return run

    return [piece(r0, c0) for r0 in range(0, p_ref.shape[0], TILE)
            for c0 in range(0, p_ref.shape[1], PIECE_COLS)]


def _emitter(pieces):
    def emit(n):
        for _ in range(min(n, len(pieces))):
            pieces.pop(0)()
    return emit


def _ping_pong_steps(i, n, step, even_refs, odd_refs):
    first, last = i == 0, i == n
    middle = jnp.logical_not(jnp.logical_or(first, last))

    @pl.when(first)
    def _():
        step(even_refs, None)

    @pl.when(jnp.logical_and(middle, i % 2 == 0))
    def _():
        step(even_refs, odd_refs)

    @pl.when(jnp.logical_and(middle, i % 2 == 1))
    def _():
        step(odd_refs, even_refs)

    @pl.when(last)
    def _():
        step(None, even_refs if n % 2 == 1 else odd_refs)


def _bwd_state_kernel(steps_per_row, n_steps, xn_ref, npre_ref, wkv_ref, wgk_ref, bgk_ref, utri_ref,
                      klr_ref, vs_ref, p0_ref, p1_ref, s_ref):
    i = pl.program_id(0)

    @pl.when(jnp.logical_or(i == 0, (n_steps - i) % steps_per_row == steps_per_row - 1))
    def _():
        s_ref[...] = jnp.zeros_like(s_ref)

    def step(write_refs, read_refs):
        emit = _emitter(_projection_pieces(xn_ref, npre_ref, wkv_ref, *write_refs)
                        if write_refs is not None else [])
        s = s_ref[...]
        for j in reversed(range(PRE_SUB) if read_refs is not None else ()):
            p_ref, r0 = read_refs[0], j * TILE
            emit(1)
            k = p_ref[r0:r0 + TILE, 0:D_K]
            vb = p_ref[r0:r0 + TILE, D_K:D_K + D_B].astype(BF16)
            plr = p_ref[r0:r0 + TILE, D_K + D_B:D_K + D_B + LANES]
            klr_ref[j, :, 0:D_K] = k
            klr_ref[j, :, D_K:D_K + LANES] = plr
            vs_ref[j, 0:TILE, :] = vb
            g = _log_sigmoid(_dot(plr.astype(BF16), wgk_ref[...]) + bgk_ref[...]) * (
                1.0 / GATE_NORMALIZER)
            emit(1)
            bcum = _chunk_cumsum(utri_ref, g)
            kdec, dec = _decayed_keys(k, bcum, backward=True)
            kt = jnp.transpose(kdec).astype(BF16)
            emit(1)
            v_par = _parity_masked_values(vb)
            kv = [_chunk_kv(kt, v_par, c) for c in range(N_CHUNK)]
            dec_col = [_decay_column(dec[c]) for c in range(N_CHUNK)]
            emit(1)
            for c in reversed(range(N_CHUNK)):
                vs_ref[j, TILE:TILE + N_HEADS_B * HEAD_K, c * HEAD_V:(c + 1) * HEAD_V] = (
                    s.astype(BF16))
                s = dec_col[c] * s + kv[c]
        emit(PRE_SUB * PRE_COLS // PIECE_COLS)
        s_ref[...] = s

    _ping_pong_steps(i, n_steps, step, (p0_ref,), (p1_ref,))


def _mix(j, p_ref, x_ref, klr_ref, vs_ref, wsp_ref, bsp_ref, gva_ref, wgk_ref, bgk_ref,
         ltri_ref, utri_ref, gnb_ref, wout_ref, npost_ref, out_ref, sf_ref, o_ref, mix_ref, emit):
    t0 = j * TILE

    emit(2)
    vg = _gelu(p_ref[t0:t0 + TILE, M_V:M_ZA])
    vc = vg - jnp.mean(vg, axis=-1, keepdims=True)
    vn = (vc * _rms_scale(vc) * gva_ref[...]).astype(BF16)
    for c in range(TILE // CHUNK_A):
        r0 = c * CHUNK_A
        for hd in range(N_HEADS_A):
            c0 = hd * HEAD_A
            sv = _dot(wsp_ref[hd], vn[r0:r0 + CHUNK_A, c0:c0 + HEAD_A]) + bsp_ref[hd]
            out_a = _gelu(p_ref[t0 + r0:t0 + r0 + CHUNK_A, M_U + c0:M_U + c0 + HEAD_A]) * sv
            out_a = out_a * _silu(
                p_ref[t0 + r0:t0 + r0 + CHUNK_A, M_ZA + c0:M_ZA + c0 + HEAD_A])
            mix_ref[j, r0:r0 + CHUNK_A, c0:c0 + HEAD_A] = out_a.astype(BF16)
            emit(1 if hd == 1 else 0)

    q = p_ref[t0:t0 + TILE, M_Q:M_ZB] * (HEAD_K ** -0.5)
    k = klr_ref[j, :, 0:D_K]
    vb = vs_ref[j, 0:TILE, :]
    plr = klr_ref[j, :, D_K:D_K + LANES]
    emit(1)
    g = _log_sigmoid(_dot(plr.astype(BF16), wgk_ref[...]) + bgk_ref[...]) * (1.0 / GATE_NORMALIZER)
    v_par = _parity_masked_values(vb)

    lane_head = lax.broadcasted_iota(jnp.int32, (CHUNK_B, D_K), 1) // HEAD_K
    lane_pos = lax.broadcasted_iota(jnp.int32, (CHUNK_B, D_K), 1) % CHUNK_B
    row_pos = lax.broadcasted_iota(jnp.int32, (CHUNK_B, D_K), 0)
    zero_blk = jnp.zeros((CHUNK_B, HEAD_V), BF16)

    dirs = []
    for d in range(2):
        backward = d == 1
        emit(1)
        bcum = _chunk_cumsum(utri_ref if backward else ltri_ref, g[:, d * D_K:(d + 1) * D_K])
        q_in = (q * jnp.exp(bcum)).astype(BF16)
        k_in = (k * jnp.exp(-bcum)).astype(BF16)
        keep = (row_pos < lane_pos) if backward else (row_pos >= lane_pos)
        dirs.append((q_in, k_in, keep))
        if not backward:
            kdec, dec = _decayed_keys(k, bcum, backward)
            kt = jnp.transpose(kdec).astype(BF16)

    steps = [(t, d, N_CHUNK - 1 - t if d == 1 else t) for t in range(N_CHUNK) for d in range(2)]
    p_blk, kv_blk, dec_col = {}, {}, {}
    for t, d, c in steps:
        q_in, k_in, keep = dirs[d]
        k_c = k_in[c * CHUNK_B:(c + 1) * CHUNK_B, :]
        k_bd = jnp.concatenate(
            [jnp.where(lane_head == hd, k_c, jnp.zeros_like(k_c)) for hd in range(N_HEADS_B)],
            axis=0)
        scores = _dot_nt(q_in[c * CHUNK_B:(c + 1) * CHUNK_B, :], k_bd)
        p_blk[d, c] = jnp.where(keep, scores, 0.0).astype(BF16)
    for c in range(N_CHUNK):
        kv_blk[c] = _chunk_kv(kt, v_par, c)
        dec_col[c] = _decay_column(dec[c])

    for t, d, c in steps:
        q_in = dirs[d][0]
        r0 = c * CHUNK_B
        if d == 1:
            s_bf = vs_ref[j, TILE:TILE + N_HEADS_B * HEAD_K, c * HEAD_V:(c + 1) * HEAD_V]
        else:
            s_old = sf_ref[...]
            s_bf = s_old.astype(BF16)
        for p in range(N_PAIR):
            a0 = 2 * p * HEAD_V
            rhs = jnp.concatenate([
                jnp.concatenate([vb[r0:r0 + CHUNK_B, a0:a0 + HEAD_V], zero_blk], axis=1),
                jnp.concatenate([zero_blk, vb[r0:r0 + CHUNK_B, a0 + HEAD_V:a0 + 2 * HEAD_V]], axis=1),
                jnp.concatenate([s_bf[2 * p * HEAD_K:(2 * p + 1) * HEAD_K, :], zero_blk], axis=1),
                jnp.concatenate([zero_blk, s_bf[(2 * p + 1) * HEAD_K:(2 * p + 2) * HEAD_K, :]], axis=1),
            ], axis=0)
            lhs = jnp.concatenate([p_blk[d, c][:, p * LANES:(p + 1) * LANES],
                                   q_in[r0:r0 + CHUNK_B, p * LANES:(p + 1) * LANES]], axis=1)
            o_pair = _dot(lhs, rhs)
            if t < N_CHUNK // 2:
                o_ref[j, r0:r0 + CHUNK_B, a0:a0 + 2 * HEAD_V] = o_pair
            else:
                o_ref[j, r0:r0 + CHUNK_B, a0:a0 + 2 * HEAD_V] += o_pair
        if d == 0:
            sf_ref[...] = dec_col[c] * s_old + kv_blk[c]

    emit(1)
    for hd in range(N_HEADS_B):
        c0 = hd * HEAD_V
        oh = o_ref[j, :, c0:c0 + HEAD_V]
        out_b = oh * _rms_scale(oh) * gnb_ref[...] * _silu(
            p_ref[t0:t0 + TILE, M_ZB + c0:M_ZB + c0 + HEAD_V])
        mix_ref[j, :, D_A + c0:D_A + c0 + HEAD_V] = out_b.astype(BF16)

    mixed = jnp.concatenate(
        [_dot(mix_ref[j], wout_ref[:, c0:c0 + PIECE_COLS]) for c0 in range(0, D_MODEL, PIECE_COLS)],
        axis=1)
    emit(1)
    out_ref[0, t0:t0 + TILE, :] = (
        x_ref[t0:t0 + TILE, :] + mixed * _rms_scale(mixed) * npost_ref[...])


def _layer_kernel(steps_per_row, m_steps, xn_ref, klr_ref, vs_ref, npre_ref, win_ref,
                  wsp_ref, bsp_ref, gva_ref, wgk_ref, bgk_ref, ltri_ref, utri_ref, gnb_ref,
                  wout_ref, npost_ref, out_ref, p0_ref, p1_ref, x0_ref, x1_ref, sf_ref, o_ref, mix_ref):
    i = pl.program_id(0)

    @pl.when(jnp.logical_or(i == 0, (i - 1) % steps_per_row == 0))
    def _():
        sf_ref[...] = jnp.zeros_like(sf_ref)

    def step(write_refs, read_refs):
        emit = _emitter(_projection_pieces(xn_ref, npre_ref, win_ref, *write_refs)
                        if write_refs is not None else [])
        for j in range(MAIN_SUB) if read_refs is not None else ():
            _mix(j, *read_refs, klr_ref, vs_ref, wsp_ref, bsp_ref, gva_ref, wgk_ref, bgk_ref,
                 ltri_ref, utri_ref, gnb_ref, wout_ref, npost_ref, out_ref, sf_ref, o_ref, mix_ref,
                 emit)
        emit(MAIN_SUB * P_COLS // PIECE_COLS)

    _ping_pong_steps(i, m_steps, step, (p0_ref, x0_ref), (p1_ref, x1_ref))


def _const_spec(shape):
    return pl.BlockSpec(shape, lambda *_: (0,) * len(shape))


def _hybrid_layer(x, wts):
    bsz, seq, _ = x.shape
    n_tiles = seq // TILE
    n_total = bsz * n_tiles

    params = pltpu.CompilerParams(dimension_semantics=("arbitrary",), vmem_limit_bytes=VMEM_LIMIT)

    n_steps = n_total // PRE_SUB

    def recurred_block(i):
        return (jnp.minimum(n_steps - i, n_steps - 1), 0, 0)

    klr_tiles, vs_tiles = pl.pallas_call(
        functools.partial(_bwd_state_kernel, n_tiles // PRE_SUB, n_steps),
        grid=(n_steps + 1,),
        in_specs=[
            pl.BlockSpec((1, PRE_SUB * TILE, D_MODEL),
                         lambda i: (jnp.maximum(n_steps - 1 - i, 0), 0, 0)),
            _const_spec((1, D_MODEL)),
            _const_spec((D_MODEL, PRE_COLS + PIECE_COLS)),
            _const_spec((LANES, D_K)),
            _const_spec((1, D_K)),
            _const_spec((TILE, TILE)),
        ],
        out_specs=[
            pl.BlockSpec((PRE_SUB, TILE, D_K + LANES), recurred_block),
            pl.BlockSpec((PRE_SUB, TILE + N_HEADS_B * HEAD_K, D_B), recurred_block),
        ],
        out_shape=[
            jax.ShapeDtypeStruct((n_total, TILE, D_K + LANES), F32),
            jax.ShapeDtypeStruct((n_total, TILE + N_HEADS_B * HEAD_K, D_B), BF16),
        ],
        scratch_shapes=[
            pltpu.VMEM((PRE_SUB * TILE, PRE_COLS), F32),
            pltpu.VMEM((PRE_SUB * TILE, PRE_COLS), F32),
            pltpu.VMEM((N_HEADS_B * HEAD_K, HEAD_V), F32),
        ],
        compiler_params=params,
    )(x.reshape(n_steps, PRE_SUB * TILE, D_MODEL), wts["npre"], wts["wkv"], wts["wgk_b"],
      wts["bgk_b"], wts["utri"])

    m_steps = n_total // MAIN_SUB

    def finished_block(i):
        return (jnp.maximum(i - 1, 0), 0, 0)

    y = pl.pallas_call(
        functools.partial(_layer_kernel, n_tiles // MAIN_SUB, m_steps),
        grid=(m_steps + 1,),
        in_specs=[
            pl.BlockSpec((1, MAIN_SUB * TILE, D_MODEL),
                         lambda i: (jnp.minimum(i, m_steps - 1), 0, 0)),
            pl.BlockSpec((MAIN_SUB, TILE, D_K + LANES), finished_block),
            pl.BlockSpec((MAIN_SUB, TILE + N_HEADS_B * HEAD_K, D_B), finished_block),
            _const_spec((1, D_MODEL)),
            _const_spec((D_MODEL, P_COLS)),
            _const_spec((N_HEADS_A, CHUNK_A, CHUNK_A)),
            _const_spec((N_HEADS_A, CHUNK_A, HEAD_A)),
            _const_spec((1, D_A)),
            _const_spec((LANES, 2 * D_K)),
            _const_spec((1, 2 * D_K)),
            _const_spec((TILE, TILE)),
            _const_spec((TILE, TILE)),
            _const_spec((1, HEAD_V)),
            _const_spec((D_A + D_B, D_MODEL + PIECE_COLS)),
            _const_spec((1, D_MODEL)),
        ],
        out_specs=pl.BlockSpec((1, MAIN_SUB * TILE, D_MODEL), finished_block),
        out_shape=jax.ShapeDtypeStruct((m_steps, MAIN_SUB * TILE, D_MODEL), x.dtype),
        scratch_shapes=[
            pltpu.VMEM((MAIN_SUB * TILE, P_COLS), F32),
            pltpu.VMEM((MAIN_SUB * TILE, P_COLS), F32),
            pltpu.VMEM((MAIN_SUB * TILE, D_MODEL), F32),
            pltpu.VMEM((MAIN_SUB * TILE, D_MODEL), F32),
            pltpu.VMEM((N_HEADS_B * HEAD_K, HEAD_V), F32),
            pltpu.VMEM((MAIN_SUB, TILE, D_B), F32),
            pltpu.VMEM((MAIN_SUB, TILE, D_A + D_B), BF16),
        ],
        compiler_params=params,
    )(x.reshape(m_steps, MAIN_SUB * TILE, D_MODEL), klr_tiles, vs_tiles, wts["npre"], wts["win"],
      wts["wsp"], wts["bsp"], wts["gva"], wts["wgk"], wts["bgk"], wts["ltri"], wts["utri"],
      wts["gnb"], wts["wout"], wts["npost"])
    return y.reshape(x.shape)


def _prepare_weights(norm_pre, w_in, w_sp, b_sp, g_v_a, w_gk_fwd, b_gk_fwd, w_gk_bwd, b_gk_bwd,
                     g_norm_b, w_out, norm_post):
    w_in_bf = w_in.astype(BF16)
    wlr = jnp.zeros((D_MODEL, PIECE_COLS), BF16).at[:, 0:2 * GATE_RANK].set(w_in_bf[:, OFF_LRF:D_IN])
    wgk = jnp.zeros((LANES, 2 * D_K), BF16)
    wgk = wgk.at[0:GATE_RANK, 0:D_K].set(w_gk_fwd.astype(BF16))
    wgk = wgk.at[GATE_RANK:2 * GATE_RANK, D_K:2 * D_K].set(w_gk_bwd.astype(BF16))
    pad_cols = jnp.zeros((D_MODEL, PIECE_COLS), BF16)
    pos = jnp.arange(TILE)
    same_chunk = (pos[:, None] // CHUNK_B) == (pos[None, :] // CHUNK_B)
    ltri = (same_chunk & (pos[None, :] <= pos[:, None])).astype(BF16)
    utri = (same_chunk & (pos[None, :] >= pos[:, None])).astype(BF16)
    return {
        "npre": norm_pre.reshape(1, D_MODEL),
        "win": jnp.concatenate([w_in_bf[:, OFF_U:OFF_K], w_in_bf[:, OFF_ZB:OFF_LRF]], axis=1),
        "wkv": jnp.concatenate([w_in_bf[:, OFF_K:OFF_ZB], wlr, pad_cols], axis=1),
        "wsp": w_sp.astype(BF16),
        "bsp": jnp.broadcast_to(b_sp[:, :, None], (N_HEADS_A, CHUNK_A, HEAD_A)),
        "gva": g_v_a.reshape(1, D_A),
        "wgk": wgk,
        "wgk_b": wgk[:, D_K:2 * D_K],
        "bgk": jnp.concatenate([b_gk_fwd, b_gk_bwd]).reshape(1, 2 * D_K),
        "bgk_b": b_gk_bwd.reshape(1, D_K),
        "ltri": ltri,
        "utri": utri,
        "gnb": g_norm_b.reshape(1, HEAD_V),
        "wout": jnp.concatenate([w_out.astype(BF16), pad_cols], axis=1),
        "npost": norm_post.reshape(1, D_MODEL),
    }


def kernel(x_prompt, x_sample, norm_pre, w_in, w_sp, b_sp, g_v_a, w_gk_fwd, b_gk_fwd,
           w_gk_bwd, b_gk_bwd, g_norm_b, w_out, norm_post):
    y_prompt, y_sample = x_prompt, x_sample
    for l in range(norm_pre.shape[0]):
        wts = _prepare_weights(norm_pre[l], w_in[l], w_sp[l], b_sp[l], g_v_a[l], w_gk_fwd[l],
                               b_gk_fwd[l], w_gk_bwd[l], b_gk_bwd[l], g_norm_b[l], w_out[l],
                               norm_post[l])
        y_prompt = _hybrid_layer(y_prompt, wts)
        y_sample = _hybrid_layer(y_sample, wts)
    return (y_prompt, y_sample)
```

```python
import functools

import jax
import jax.numpy as jnp
from jax import lax
from jax.experimental import pallas as pl
from jax.experimental.pallas import tpu as pltpu

F32 = jnp.float32
BF16 = jnp.bfloat16

D_MODEL = 1024
D_A = 512
N_HEADS_A = 4
HEAD_A = 128
CHUNK_A = 128
D_B = 512
N_HEADS_B = 4
HEAD_V = 128
D_K = 256
HEAD_K = 64
GATE_RANK = 16
GATE_NORMALIZER = 16.0
CHUNK_B = 64
EPS = 1e-6

OFF_U, OFF_V, OFF_ZA, OFF_Q, OFF_K, OFF_VB, OFF_ZB, OFF_LRF, OFF_LRB, D_IN = (
    0, 512, 1024, 1536, 1792, 2048, 2560, 3072, 3088, 3104)

LANES = 128
TILE = 256
N_CHUNK = TILE // CHUNK_B
N_PAIR = N_HEADS_B // 2
PIECE_COLS = 256
PRE_COLS = D_K + D_B + PIECE_COLS
PRE_SUB = 8
MAIN_SUB = 2
M_U, M_V, M_ZA, M_Q, M_ZB, P_COLS = 0, 512, 1024, 1536, 1792, 2304
VMEM_LIMIT = 56 * 1024 * 1024


def _dot(a, b):
    return jnp.dot(a, b, preferred_element_type=F32)


def _dot_nt(a, b):
    return lax.dot_general(a, b, (((1,), (1,)), ((), ())), preferred_element_type=F32)


def _gelu(x):
    return 0.5 * x * (1.0 + jnp.tanh(0.7978845608028654 * (x + 0.044715 * (x * x * x))))


def _silu(x):
    return x * (0.5 * jnp.tanh(0.5 * x) + 0.5)


def _log_sigmoid(x):
    return jnp.minimum(x, 0.0) - jnp.log1p(jnp.exp(-jnp.abs(x)))


def _rms_scale(x):
    return lax.rsqrt(jnp.mean(x * x, axis=-1, keepdims=True) + EPS)


def _chunk_cumsum(tri_ref, g):
    hi = g.astype(BF16)
    lo = (g - hi.astype(F32)).astype(BF16)
    tri = tri_ref[...]
    return _dot(tri, hi) + _dot(tri, lo)


def _decayed_keys(k, bcum, backward):
    kd, dec = [], []
    for c in range(N_CHUNK):
        r0 = c * CHUNK_B
        last = r0 if backward else r0 + CHUNK_B - 1
        b_last = bcum[last:last + 1, :]
        kd.append(k[r0:r0 + CHUNK_B, :] * jnp.exp(b_last - bcum[r0:r0 + CHUNK_B, :]))
        dec.append(jnp.exp(b_last))
    return jnp.concatenate(kd, axis=0), dec


def _chunk_kv(kt, v_par, c):
    cp, par = divmod(c, 2)
    parts = []
    for p in range(N_PAIR):
        full = _dot(kt[p * LANES:(p + 1) * LANES, cp * LANES:(cp + 1) * LANES],
                    v_par[cp][par][:, p * 2 * HEAD_V:(p + 1) * 2 * HEAD_V])
        parts.append(full[0:HEAD_K, 0:HEAD_V])
        parts.append(full[HEAD_K:2 * HEAD_K, HEAD_V:2 * HEAD_V])
    return jnp.concatenate(parts, axis=0)


def _decay_column(dec_row):
    return jnp.transpose(jnp.broadcast_to(dec_row, (LANES, D_K)))


def _parity_masked_values(vb):
    out = []
    row = lax.broadcasted_iota(jnp.int32, (2 * CHUNK_B, D_B), 0)
    for cp in range(N_CHUNK // 2):
        vp = vb[cp * 2 * CHUNK_B:(cp + 1) * 2 * CHUNK_B, :]
        zero = jnp.zeros_like(vp)
        out.append((jnp.where(row < CHUNK_B, vp, zero), jnp.where(row >= CHUNK_B, vp, zero)))
    return out


def _projection_pieces(x_ref, npre_ref, w_ref, p_ref, x_keep_ref=None):
    normed = {}

    def piece(r0, c0):
        def run():
            if r0 not in normed:
                x = x_ref[0, r0:r0 + TILE, :]
                if x_keep_ref is not None:
                    x_keep_ref[r0:r0 + TILE, :] = x
                normed[r0] = (x * _rms_scale(x) * npre_ref[...]).astype(BF16)
            p_ref[r0:r0 + TILE, c0:c0 + PIECE_COLS] = _dot(normed[r0], w_ref[:, c0:c0 + PIECE_COLS])
        return run

    return [piece(r0, c0) for r0 in range(0, p_ref.shape[0], TILE)
            for c0 in range(0, p_ref.shape[1], PIECE_COLS)]


def _emitter(pieces):
    def emit(n):
        for _ in range(min(n, len(pieces))):
            pieces.pop(0)()
    return emit


def _ping_pong_steps(i, n, step, even_refs, odd_refs):
    first, last = i == 0, i == n
    middle = jnp.logical_not(jnp.logical_or(first, last))

    @pl.when(first)
    def _():
        step(even_refs, None)

    @pl.when(jnp.logical_and(middle, i % 2 == 0))
    def _():
        step(even_refs, odd_refs)

    @pl.when(jnp.logical_and(middle, i % 2 == 1))
    def _():
        step(odd_refs, even_refs)

    @pl.when(last)
    def _():
        step(None, even_refs if n % 2 == 1 else odd_refs)


def _bwd_state_kernel(steps_per_row, n_steps, xn_ref, npre_ref, wkv_ref, wgk_ref, bgk_ref, utri_ref,
                      klr_ref, vs_ref, p0_ref, p1_ref, s_ref):
    i = pl.program_id(0)

    @pl.when(jnp.logical_or(i == 0, (n_steps - i) % steps_per_row == steps_per_row - 1))
    def _():
        s_ref[...] = jnp.zeros_like(s_ref)

    def step(write_refs, read_refs):
        emit = _emitter(_projection_pieces(xn_ref, npre_ref, wkv_ref, *write_refs)
                        if write_refs is not None else [])
        s = s_ref[...]
        for j in reversed(range(PRE_SUB) if read_refs is not None else ()):
            p_ref, r0 = read_refs[0], j * TILE
            emit(1)
            k = p_ref[r0:r0 + TILE, 0:D_K]
            vb = p_ref[r0:r0 + TILE, D_K:D_K + D_B].astype(BF16)
            plr = p_ref[r0:r0 + TILE, D_K + D_B:D_K + D_B + LANES]
            klr_ref[j, :, 0:D_K] = k
            klr_ref[j, :, D_K:D_K + LANES] = plr
            vs_ref[j, 0:TILE, :] = vb
            g = _log_sigmoid(_dot(plr.astype(BF16), wgk_ref[...]) + bgk_ref[...]) * (
                1.0 / GATE_NORMALIZER)
            emit(1)
            bcum = _chunk_cumsum(utri_ref, g)
            kdec, dec = _decayed_keys(k, bcum, backward=True)
            kt = jnp.transpose(kdec).astype(BF16)
            emit(1)
            v_par = _parity_masked_values(vb)
            kv = [_chunk_kv(kt, v_par, c) for c in range(N_CHUNK)]
            dec_col = [_decay_column(dec[c]) for c in range(N_CHUNK)]
            emit(1)
            for c in reversed(range(N_CHUNK)):
                vs_ref[j, TILE:TILE + N_HEADS_B * HEAD_K, c * HEAD_V:(c + 1) * HEAD_V] = (
                    s.astype(BF16))
                s = dec_col[c] * s + kv[c]
        emit(PRE_SUB * PRE_COLS // PIECE_COLS)
        s_ref[...] = s

    _ping_pong_steps(i, n_steps, step, (p0_ref,), (p1_ref,))


def _mix(j, p_ref, x_ref, klr_ref, vs_ref, wsp_ref, bsp_ref, gva_ref, wgk_ref, bgk_ref,
         ltri_ref, utri_ref, gnb_ref, wout_ref, npost_ref, out_ref, sf_ref, o_ref, mix_ref, emit):
    t0 = j * TILE

    emit(2)
    vg = _gelu(p_ref[t0:t0 + TILE, M_V:M_ZA])
    vc = vg - jnp.mean(vg, axis=-1, keepdims=True)
    vn = (vc * _rms_scale(vc) * gva_ref[...]).astype(BF16)
    for c in range(TILE // CHUNK_A):
        r0 = c * CHUNK_A
        for hd in range(N_HEADS_A):
            c0 = hd * HEAD_A
            sv = _dot(wsp_ref[hd], vn[r0:r0 + CHUNK_A, c0:c0 + HEAD_A]) + bsp_ref[hd]
            out_a = _gelu(p_ref[t0 + r0:t0 + r0 + CHUNK_A, M_U + c0:M_U + c0 + HEAD_A]) * sv
            out_a = out_a * _silu(
                p_ref[t0 + r0:t0 + r0 + CHUNK_A, M_ZA + c0:M_ZA + c0 + HEAD_A])
            mix_ref[j, r0:r0 + CHUNK_A, c0:c0 + HEAD_A] = out_a.astype(BF16)
            emit(1 if hd == 1 else 0)

    q = p_ref[t0:t0 + TILE, M_Q:M_ZB] * (HEAD_K ** -0.5)
    k = klr_ref[j, :, 0:D_K]
    vb = vs_ref[j, 0:TILE, :]
    plr = klr_ref[j, :, D_K:D_K + LANES]
    emit(1)
    g = _log_sigmoid(_dot(plr.astype(BF16), wgk_ref[...]) + bgk_ref[...]) * (1.0 / GATE_NORMALIZER)
    v_par = _parity_masked_values(vb)

    lane_head = lax.broadcasted_iota(jnp.int32, (CHUNK_B, D_K), 1) // HEAD_K
    lane_pos = lax.broadcasted_iota(jnp.int32, (CHUNK_B, D_K), 1) % CHUNK_B
    row_pos = lax.broadcasted_iota(jnp.int32, (CHUNK_B, D_K), 0)
    zero_blk = jnp.zeros((CHUNK_B, HEAD_V), BF16)

    dirs = []
    for d in range(2):
        backward = d == 1
        emit(1)
        bcum = _chunk_cumsum(utri_ref if backward else ltri_ref, g[:, d * D_K:(d + 1) * D_K])
        q_in = (q * jnp.exp(bcum)).astype(BF16)
        k_in = (k * jnp.exp(-bcum)).astype(BF16)
        keep = (row_pos < lane_pos) if backward else (row_pos >= lane_pos)
        dirs.append((q_in, k_in, keep))
        if not backward:
            kdec, dec = _decayed_keys(k, bcum, backward)
            kt = jnp.transpose(kdec).astype(BF16)

    steps = [(t, d, N_CHUNK - 1 - t if d == 1 else t) for t in range(N_CHUNK) for d in range(2)]
    p_blk, kv_blk, dec_col = {}, {}, {}
    for t, d, c in steps:
        q_in, k_in, keep = dirs[d]
        k_c = k_in[c * CHUNK_B:(c + 1) * CHUNK_B, :]
        k_bd = jnp.concatenate(
            [jnp.where(lane_head == hd, k_c, jnp.zeros_like(k_c)) for hd in range(N_HEADS_B)],
            axis=0)
        scores = _dot_nt(q_in[c * CHUNK_B:(c + 1) * CHUNK_B, :], k_bd)
        p_blk[d, c] = jnp.where(keep, scores, 0.0).astype(BF16)
    for c in range(N_CHUNK):
        kv_blk[c] = _chunk_kv(kt, v_par, c)
        dec_col[c] = _decay_column(dec[c])

    for t, d, c in steps:
        q_in = dirs[d][0]
        r0 = c * CHUNK_B
        if d == 1:
            s_bf = vs_ref[j, TILE:TILE + N_HEADS_B * HEAD_K, c * HEAD_V:(c + 1) * HEAD_V]
        else:
            s_old = sf_ref[...]
            s_bf = s_old.astype(BF16)
        for p in range(N_PAIR):
            a0 = 2 * p * HEAD_V
            rhs = jnp.concatenate([
                jnp.concatenate([vb[r0:r0 + CHUNK_B, a0:a0 + HEAD_V], zero_blk], axis=1),
                jnp.concatenate([zero_blk, vb[r0:r0 + CHUNK_B, a0 + HEAD_V:a0 + 2 * HEAD_V]], axis=1),
                jnp.concatenate([s_bf[2 * p * HEAD_K:(2 * p + 1) * HEAD_K, :], zero_blk], axis=1),
                jnp.concatenate([zero_blk, s_bf[(2 * p + 1) * HEAD_K:(2 * p + 2) * HEAD_K, :]], axis=1),
            ], axis=0)
            lhs = jnp.concatenate([p_blk[d, c][:, p * LANES:(p + 1) * LANES],
                                   q_in[r0:r0 + CHUNK_B, p * LANES:(p + 1) * LANES]], axis=1)
            o_pair = _dot(lhs, rhs)
            if t < N_CHUNK // 2:
                o_ref[j, r0:r0 + CHUNK_B, a0:a0 + 2 * HEAD_V] = o_pair
            else:
                o_ref[j, r0:r0 + CHUNK_B, a0:a0 + 2 * HEAD_V] += o_pair
        if d == 0:
            sf_ref[...] = dec_col[c] * s_old + kv_blk[c]

    emit(1)
    for hd in range(N_HEADS_B):
        c0 = hd * HEAD_V
        oh = o_ref[j, :, c0:c0 + HEAD_V]
        out_b = oh * _rms_scale(oh) * gnb_ref[...] * _silu(
            p_ref[t0:t0 + TILE, M_ZB + c0:M_ZB + c0 + HEAD_V])
        mix_ref[j, :, D_A + c0:D_A + c0 + HEAD_V] = out_b.astype(BF16)

    mixed = jnp.concatenate(
        [_dot(mix_ref[j], wout_ref[:, c0:c0 + PIECE_COLS]) for c0 in range(0, D_MODEL, PIECE_COLS)],
        axis=1)
    emit(1)
    out_ref[0, t0:t0 + TILE, :] = (
        x_ref[t0:t0 + TILE, :] + mixed * _rms_scale(mixed) * npost_ref[...])


def _layer_kernel(steps_per_row, m_steps, xn_ref, klr_ref, vs_ref, npre_ref, win_ref,
                  wsp_ref, bsp_ref, gva_ref, wgk_ref, bgk_ref, ltri_ref, utri_ref, gnb_ref,
                  wout_ref, npost_ref, out_ref, p0_ref, p1_ref, x0_ref, x1_ref, sf_ref, o_ref, mix_ref):
    i = pl.program_id(0)

    @pl.when(jnp.logical_or(i == 0, (i - 1) % steps_per_row == 0))
    def _():
        sf_ref[...] = jnp.zeros_like(sf_ref)

    def step(write_refs, read_refs):
        emit = _emitter(_projection_pieces(xn_ref, npre_ref, win_ref, *write_refs)
                        if write_refs is not None else [])
        for j in range(MAIN_SUB) if read_refs is not None else ():
            _mix(j, *read_refs, klr_ref, vs_ref, wsp_ref, bsp_ref, gva_ref, wgk_ref, bgk_ref,
                 ltri_ref, utri_ref, gnb_ref, wout_ref, npost_ref, out_ref, sf_ref, o_ref, mix_ref,
                 emit)
        emit(MAIN_SUB * P_COLS // PIECE_COLS)

    _ping_pong_steps(i, m_steps, step, (p0_ref, x0_ref), (p1_ref, x1_ref))


def _const_spec(shape):
    return pl.BlockSpec(shape, lambda *_: (0,) * len(shape))


def _hybrid_layer(x, wts):
    bsz, seq, _ = x.shape
    n_tiles = seq // TILE
    n_total = bsz * n_tiles

    params = pltpu.CompilerParams(dimension_semantics=("arbitrary",), vmem_limit_bytes=VMEM_LIMIT)

    n_steps = n_total // PRE_SUB

    def recurred_block(i):
        return (jnp.minimum(n_steps - i, n_steps - 1), 0, 0)

    klr_tiles, vs_tiles = pl.pallas_call(
        functools.partial(_bwd_state_kernel, n_tiles // PRE_SUB, n_steps),
        grid=(n_steps + 1,),
        in_specs=[
            pl.BlockSpec((1, PRE_SUB * TILE, D_MODEL),
                         lambda i: (jnp.maximum(n_steps - 1 - i, 0), 0, 0)),
            _const_spec((1, D_MODEL)),
            _const_spec((D_MODEL, PRE_COLS + PIECE_COLS)),
            _const_spec((LANES, D_K)),
            _const_spec((1, D_K)),
            _const_spec((TILE, TILE)),
        ],
        out_specs=[
            pl.BlockSpec((PRE_SUB, TILE, D_K + LANES), recurred_block),
            pl.BlockSpec((PRE_SUB, TILE + N_HEADS_B * HEAD_K, D_B), recurred_block),
        ],
        out_shape=[
            jax.ShapeDtypeStruct((n_total, TILE, D_K + LANES), F32),
            jax.ShapeDtypeStruct((n_total, TILE + N_HEADS_B * HEAD_K, D_B), BF16),
        ],
        scratch_shapes=[
            pltpu.VMEM((PRE_SUB * TILE, PRE_COLS), F32),
            pltpu.VMEM((PRE_SUB * TILE, PRE_COLS), F32),
            pltpu.VMEM((N_HEADS_B * HEAD_K, HEAD_V), F32),
        ],
        compiler_params=params,
    )(x.reshape(n_steps, PRE_SUB * TILE, D_MODEL), wts["npre"], wts["wkv"], wts["wgk_b"],
      wts["bgk_b"], wts["utri"])

    m_steps = n_total // MAIN_SUB

    def finished_block(i):
        return (jnp.maximum(i - 1, 0), 0, 0)

    y = pl.pallas_call(
        functools.partial(_layer_kernel, n_tiles // MAIN_SUB, m_steps),
        grid=(m_steps + 1,),
        in_specs=[
            pl.BlockSpec((1, MAIN_SUB * TILE, D_MODEL),
                         lambda i: (jnp.minimum(i, m_steps - 1), 0, 0)),
            pl.BlockSpec((MAIN_SUB, TILE, D_K + LANES), finished_block),
            pl.BlockSpec((MAIN_SUB, TILE + N_HEADS_B * HEAD_K, D_B), finished_block),
            _const_spec((1, D_MODEL)),
            _const_spec((D_MODEL, P_COLS)),
            _const_spec((N_HEADS_A, CHUNK_A, CHUNK_A)),
            _const_spec((N_HEADS_A, CHUNK_A, HEAD_A)),
            _const_spec((1, D_A)),
            _const_spec((LANES, 2 * D_K)),
            _const_spec((1, 2 * D_K)),
            _const_spec((TILE, TILE)),
            _const_spec((TILE, TILE)),
            _const_spec((1, HEAD_V)),
            _const_spec((D_A + D_B, D_MODEL + PIECE_COLS)),
            _const_spec((1, D_MODEL)),
        ],
        out_specs=pl.BlockSpec((1, MAIN_SUB * TILE, D_MODEL), finished_block),
        out_shape=jax.ShapeDtypeStruct((m_steps, MAIN_SUB * TILE, D_MODEL), x.dtype),
        scratch_shapes=[
            pltpu.VMEM((MAIN_SUB * TILE, P_COLS), F32),
            pltpu.VMEM((MAIN_SUB * TILE, P_COLS), F32),
            pltpu.VMEM((MAIN_SUB * TILE, D_MODEL), F32),
            pltpu.VMEM((MAIN_SUB * TILE, D_MODEL), F32),
            pltpu.VMEM((N_HEADS_B * HEAD_K, HEAD_V), F32),
            pltpu.VMEM((MAIN_SUB, TILE, D_B), F32),
            pltpu.VMEM((MAIN_SUB, TILE, D_A + D_B), BF16),
        ],
        compiler_params=params,
    )(x.reshape(m_steps, MAIN_SUB * TILE, D_MODEL), klr_tiles, vs_tiles, wts["npre"], wts["win"],
      wts["wsp"], wts["bsp"], wts["gva"], wts["wgk"], wts["bgk"], wts["ltri"], wts["utri"],
      wts["gnb"], wts["wout"], wts["npost"])
    return y.reshape(x.shape)


def _prepare_weights(norm_pre, w_in, w_sp, b_sp, g_v_a, w_gk_fwd, b_gk_fwd, w_gk_bwd, b_gk_bwd,
                     g_norm_b, w_out, norm_post):
    w_in_bf = w_in.astype(BF16)
    wlr = jnp.zeros((D_MODEL, PIECE_COLS), BF16).at[:, 0:2 * GATE_RANK].set(w_in_bf[:, OFF_LRF:D_IN])
    wgk = jnp.zeros((LANES, 2 * D_K), BF16)
    wgk = wgk.at[0:GATE_RANK, 0:D_K].set(w_gk_fwd.astype(BF16))
    wgk = wgk.at[GATE_RANK:2 * GATE_RANK, D_K:2 * D_K].set(w_gk_bwd.astype(BF16))
    pad_cols = jnp.zeros((D_MODEL, PIECE_COLS), BF16)
    pos = jnp.arange(TILE)
    same_chunk = (pos[:, None] // CHUNK_B) == (pos[None, :] // CHUNK_B)
    ltri = (same_chunk & (pos[None, :] <= pos[:, None])).astype(BF16)
    utri = (same_chunk & (pos[None, :] >= pos[:, None])).astype(BF16)
    return {
        "npre": norm_pre.reshape(1, D_MODEL),
        "win": jnp.concatenate([w_in_bf[:, OFF_U:OFF_K], w_in_bf[:, OFF_ZB:OFF_LRF]], axis=1),
        "wkv": jnp.concatenate([w_in_bf[:, OFF_K:OFF_ZB], wlr, pad_cols], axis=1),
        "wsp": w_sp.astype(BF16),
        "bsp": jnp.broadcast_to(b_sp[:, :, None], (N_HEADS_A, CHUNK_A, HEAD_A)),
        "gva": g_v_a.reshape(1, D_A),
        "wgk": wgk,
        "wgk_b": wgk[:, D_K:2 * D_K],
        "bgk": jnp.concatenate([b_gk_fwd, b_gk_bwd]).reshape(1, 2 * D_K),
        "bgk_b": b_gk_bwd.reshape(1, D_K),
        "ltri": ltri,
        "utri": utri,
        "gnb": g_norm_b.reshape(1, HEAD_V),
        "wout": jnp.concatenate([w_out.astype(BF16), pad_cols], axis=1),
        "npost": norm_post.reshape(1, D_MODEL),
    }


def kernel(x_prompt, x_sample, norm_pre, w_in, w_sp, b_sp, g_v_a, w_gk_fwd, b_gk_fwd,
           w_gk_bwd, b_gk_bwd, g_norm_b, w_out, norm_post):
    y_prompt, y_sample = x_prompt, x_sample
    for l in range(norm_pre.shape[0]):
        wts = _prepare_weights(norm_pre[l], w_in[l], w_sp[l], b_sp[l], g_v_a[l], w_gk_fwd[l],
                               b_gk_fwd[l], w_gk_bwd[l], b_gk_bwd[l], g_norm_b[l], w_out[l],
                               norm_post[l])
        y_prompt = _hybrid_layer(y_prompt, wts)
        y_sample = _hybrid_layer(y_sample, wts)
    return (y_prompt, y_sample)
```

```python
import functools

import jax
import jax.numpy as jnp
from jax import lax
from jax.experimental import pallas as pl
from jax.experimental.pallas import tpu as pltpu

F32 = jnp.float32
BF16 = jnp.bfloat16

D_MODEL = 1024
D_A = 512
N_HEADS_A = 4
HEAD_A = 128
CHUNK_A = 128
D_B = 512
N_HEADS_B = 4
HEAD_V = 128
D_K = 256
HEAD_K = 64
GATE_RANK = 16
GATE_NORMALIZER = 16.0
CHUNK_B = 64
EPS = 1e-6

OFF_U, OFF_V, OFF_ZA, OFF_Q, OFF_K, OFF_VB, OFF_ZB, OFF_LRF, OFF_LRB, D_IN = (
    0, 512, 1024, 1536, 1792, 2048, 2560, 3072, 3088, 3104)

LANES = 128
TILE = 256
N_CHUNK = TILE // CHUNK_B
N_PAIR = N_HEADS_B // 2
PIECE_COLS = 256
PRE_COLS = D_K + D_B + PIECE_COLS
PRE_SUB = 4
MAIN_SUB = 2
M_U, M_V, M_ZA, M_Q, M_ZB, P_COLS = 0, 512, 1024, 1536, 1792, 2304
VMEM_LIMIT = 56 * 1024 * 1024


def _dot(a, b):
    return jnp.dot(a, b, preferred_element_type=F32)


def _dot_nt(a, b):
    return lax.dot_general(a, b, (((1,), (1,)), ((), ())), preferred_element_type=F32)


def _gelu(x):
    return 0.5 * x * (1.0 + jnp.tanh(0.7978845608028654 * (x + 0.044715 * (x * x * x))))


def _silu(x):
    return x * (0.5 * jnp.tanh(0.5 * x) + 0.5)


def _log_sigmoid(x):
    return jnp.minimum(x, 0.0) - jnp.log1p(jnp.exp(-jnp.abs(x)))


def _rms_scale(x):
    return lax.rsqrt(jnp.mean(x * x, axis=-1, keepdims=True) + EPS)


def _chunk_cumsum(tri_ref, g):
    hi = g.astype(BF16)
    lo = (g - hi.astype(F32)).astype(BF16)
    tri = tri_ref[...]
    return _dot(tri, hi) + _dot(tri, lo)


def _decayed_keys(k, bcum, backward):
    kd, dec = [], []
    for c in range(N_CHUNK):
        r0 = c * CHUNK_B
        last = r0 if backward else r0 + CHUNK_B - 1
        b_last = bcum[last:last + 1, :]
        kd.append(k[r0:r0 + CHUNK_B, :] * jnp.exp(b_last - bcum[r0:r0 + CHUNK_B, :]))
        dec.append(jnp.exp(b_last))
    return jnp.concatenate(kd, axis=0), dec


def _chunk_kv(kt, v_par, c):
    cp, par = divmod(c, 2)
    parts = []
    for p in range(N_PAIR):
        full = _dot(kt[p * LANES:(p + 1) * LANES, cp * LANES:(cp + 1) * LANES],
                    v_par[cp][par][:, p * 2 * HEAD_V:(p + 1) * 2 * HEAD_V])
        parts.append(full[0:HEAD_K, 0:HEAD_V])
        parts.append(full[HEAD_K:2 * HEAD_K, HEAD_V:2 * HEAD_V])
    return jnp.concatenate(parts, axis=0)


def _decay_column(dec_row):
    return jnp.transpose(jnp.broadcast_to(dec_row, (LANES, D_K)))


def _parity_masked_values(vb):
    out = []
    row = lax.broadcasted_iota(jnp.int32, (2 * CHUNK_B, D_B), 0)
    for cp in range(N_CHUNK // 2):
        vp = vb[cp * 2 * CHUNK_B:(cp + 1) * 2 * CHUNK_B, :]
        zero = jnp.zeros_like(vp)
        out.append((jnp.where(row < CHUNK_B, vp, zero), jnp.where(row >= CHUNK_B, vp, zero)))
    return out


def _projection_pieces(x_ref, npre_ref, w_ref, p_ref, x_keep_ref=None):
    normed = {}

    def piece(r0, c0):
        def run():
            if r0 not in normed:
                x = x_ref[0, r0:r0 + TILE, :]
                if x_keep_ref is not None:
                    x_keep_ref[r0:r0 + TILE, :] = x
                normed[r0] = (x * _rms_scale(x) * npre_ref[...]).astype(BF16)
            p_ref[r0:r0 + TILE, c0:c0 + PIECE_COLS] = _dot(normed[r0], w_ref[:, c0:c0 + PIECE_COLS])
        return run

    return [piece(r0, c0) for r0 in range(0, p_ref.shape[0], TILE)
            for c0 in range(0, p_ref.shape[1], PIECE_COLS)]


def _emitter(pieces):
    def emit(n):
        for _ in range(min(n, len(pieces))):
            pieces.pop(0)()
    return emit


def _ping_pong_steps(i, n, step, even_refs, odd_refs):
    first, last = i == 0, i == n
    middle = jnp.logical_not(jnp.logical_or(first, last))

    @pl.when(first)
    def _():
        step(even_refs, None)

    @pl.when(jnp.logical_and(middle, i % 2 == 0))
    def _():
        step(even_refs, odd_refs)

    @pl.when(jnp.logical_and(middle, i % 2 == 1))
    def _():
        step(odd_refs, even_refs)

    @pl.when(last)
    def _():
        step(None, even_refs if n % 2 == 1 else odd_refs)


def _bwd_state_kernel(steps_per_row, n_steps, xn_ref, npre_ref, wkv_ref, wgk_ref, bgk_ref, utri_ref,
                      klr_ref, vs_ref, p0_ref, p1_ref, s_ref):
    i = pl.program_id(0)

    @pl.when(jnp.logical_or(i == 0, (n_steps - i) % steps_per_row == steps_per_row - 1))
    def _():
        s_ref[...] = jnp.zeros_like(s_ref)

    def step(write_refs, read_refs):
        emit = _emitter(_projection_pieces(xn_ref, npre_ref, wkv_ref, *write_refs)
                        if write_refs is not None else [])
        s = s_ref[...]
        for j in reversed(range(PRE_SUB) if read_refs is not None else ()):
            p_ref, r0 = read_refs[0], j * TILE
            emit(1)
            k = p_ref[r0:r0 + TILE, 0:D_K]
            vb = p_ref[r0:r0 + TILE, D_K:D_K + D_B].astype(BF16)
            plr = p_ref[r0:r0 + TILE, D_K + D_B:D_K + D_B + LANES]
            klr_ref[j, :, 0:D_K] = k
            klr_ref[j, :, D_K:D_K + LANES] = plr
            vs_ref[j, 0:TILE, :] = vb
            g = _log_sigmoid(_dot(plr.astype(BF16), wgk_ref[...]) + bgk_ref[...]) * (
                1.0 / GATE_NORMALIZER)
            emit(1)
            bcum = _chunk_cumsum(utri_ref, g)
            kdec, dec = _decayed_keys(k, bcum, backward=True)
            kt = jnp.transpose(kdec).astype(BF16)
            emit(1)
            v_par = _parity_masked_values(vb)
            kv = [_chunk_kv(kt, v_par, c) for c in range(N_CHUNK)]
            dec_col = [_decay_column(dec[c]) for c in range(N_CHUNK)]
            emit(1)
            for c in reversed(range(N_CHUNK)):
                vs_ref[j, TILE:TILE + N_HEADS_B * HEAD_K, c * HEAD_V:(c + 1) * HEAD_V] = (
                    s.astype(BF16))
                s = dec_col[c] * s + kv[c]
        emit(PRE_SUB * PRE_COLS // PIECE_COLS)
        s_ref[...] = s

    _ping_pong_steps(i, n_steps, step, (p0_ref,), (p1_ref,))


def _mix(j, p_ref, x_ref, klr_ref, vs_ref, wsp_ref, bsp_ref, gva_ref, wgk_ref, bgk_ref,
         ltri_ref, utri_ref, gnb_ref, wout_ref, npost_ref, out_ref, sf_ref, o_ref, mix_ref, emit):
    t0 = j * TILE

    emit(2)
    vg = _gelu(p_ref[t0:t0 + TILE, M_V:M_ZA])
    vc = vg - jnp.mean(vg, axis=-1, keepdims=True)
    vn = (vc * _rms_scale(vc) * gva_ref[...]).astype(BF16)
    for c in range(TILE // CHUNK_A):
        r0 = c * CHUNK_A
        for hd in range(N_HEADS_A):
            c0 = hd * HEAD_A
            sv = _dot(wsp_ref[hd], vn[r0:r0 + CHUNK_A, c0:c0 + HEAD_A]) + bsp_ref[hd]
            out_a = _gelu(p_ref[t0 + r0:t0 + r0 + CHUNK_A, M_U + c0:M_U + c0 + HEAD_A]) * sv
            out_a = out_a * _silu(
                p_ref[t0 + r0:t0 + r0 + CHUNK_A, M_ZA + c0:M_ZA + c0 + HEAD_A])
            mix_ref[j, r0:r0 + CHUNK_A, c0:c0 + HEAD_A] = out_a.astype(BF16)
            emit(1 if hd == 1 else 0)

    q = p_ref[t0:t0 + TILE, M_Q:M_ZB] * (HEAD_K ** -0.5)
    k = klr_ref[j, :, 0:D_K]
    vb = vs_ref[j, 0:TILE, :]
    plr = klr_ref[j, :, D_K:D_K + LANES]
    emit(1)
    g = _log_sigmoid(_dot(plr.astype(BF16), wgk_ref[...]) + bgk_ref[...]) * (1.0 / GATE_NORMALIZER)
    v_par = _parity_masked_values(vb)

    lane_head = lax.broadcasted_iota(jnp.int32, (CHUNK_B, D_K), 1) // HEAD_K
    lane_pos = lax.broadcasted_iota(jnp.int32, (CHUNK_B, D_K), 1) % CHUNK_B
    row_pos = lax.broadcasted_iota(jnp.int32, (CHUNK_B, D_K), 0)
    zero_blk = jnp.zeros((CHUNK_B, HEAD_V), BF16)

    dirs = []
    for d in range(2):
        backward = d == 1
        emit(1)
        bcum = _chunk_cumsum(utri_ref if backward else ltri_ref, g[:, d * D_K:(d + 1) * D_K])
        q_in = (q * jnp.exp(bcum)).astype(BF16)
        k_in = (k * jnp.exp(-bcum)).astype(BF16)
        keep = (row_pos < lane_pos) if backward else (row_pos >= lane_pos)
        dirs.append((q_in, k_in, keep))
        if not backward:
            kdec, dec = _decayed_keys(k, bcum, backward)
            kt = jnp.transpose(kdec).astype(BF16)

    p_blk, kv_blk, dec_col = {}, {}, {}
    for c in range(N_CHUNK):
        scores = []
        for q_in, k_in, _ in dirs:
            k_c = k_in[c * CHUNK_B:(c + 1) * CHUNK_B, :]
            k_bd = jnp.concatenate(
                [jnp.where(lane_head == hd, k_c, jnp.zeros_like(k_c)) for hd in range(N_HEADS_B)],
                axis=0)
            scores.append(_dot_nt(q_in[c * CHUNK_B:(c + 1) * CHUNK_B, :], k_bd))
        p_blk[c] = jnp.where(dirs[0][2], scores[0], scores[1]).astype(BF16)
    for c in range(N_CHUNK):
        kv_blk[c] = _chunk_kv(kt, v_par, c)
        dec_col[c] = _decay_column(dec[c])

    for c in range(N_CHUNK):
        r0 = c * CHUNK_B
        s_fwd = sf_ref[...]
        s_f = s_fwd.astype(BF16)
        s_b = vs_ref[j, TILE:TILE + N_HEADS_B * HEAD_K, c * HEAD_V:(c + 1) * HEAD_V]
        for p in range(N_PAIR):
            a0 = 2 * p * HEAD_V
            rows_a = slice(2 * p * HEAD_K, (2 * p + 1) * HEAD_K)
            rows_b = slice((2 * p + 1) * HEAD_K, (2 * p + 2) * HEAD_K)
            rhs = jnp.concatenate([
                jnp.concatenate([vb[r0:r0 + CHUNK_B, a0:a0 + HEAD_V], zero_blk], axis=1),
                jnp.concatenate([zero_blk, vb[r0:r0 + CHUNK_B, a0 + HEAD_V:a0 + 2 * HEAD_V]], axis=1),
                jnp.concatenate([s_f[rows_a, :], zero_blk], axis=1),
                jnp.concatenate([zero_blk, s_f[rows_b, :]], axis=1),
                jnp.concatenate([s_b[rows_a, :], zero_blk], axis=1),
                jnp.concatenate([zero_blk, s_b[rows_b, :]], axis=1),
            ], axis=0)
            lhs = jnp.concatenate(
                [p_blk[c][:, p * LANES:(p + 1) * LANES]]
                + [q_in[r0:r0 + CHUNK_B, p * LANES:(p + 1) * LANES] for q_in, _, _ in dirs],
                axis=1)
            o_ref[j, r0:r0 + CHUNK_B, a0:a0 + 2 * HEAD_V] = _dot(lhs, rhs)
        sf_ref[...] = dec_col[c] * s_fwd + kv_blk[c]

    emit(1)
    for hd in range(N_HEADS_B):
        c0 = hd * HEAD_V
        oh = o_ref[j, :, c0:c0 + HEAD_V]
        out_b = oh * _rms_scale(oh) * gnb_ref[...] * _silu(
            p_ref[t0:t0 + TILE, M_ZB + c0:M_ZB + c0 + HEAD_V])
        mix_ref[j, :, D_A + c0:D_A + c0 + HEAD_V] = out_b.astype(BF16)

    mixed = jnp.concatenate(
        [_dot(mix_ref[j], wout_ref[:, c0:c0 + PIECE_COLS]) for c0 in range(0, D_MODEL, PIECE_COLS)],
        axis=1)
    emit(1)
    out_ref[0, t0:t0 + TILE, :] = (
        x_ref[t0:t0 + TILE, :] + mixed * _rms_scale(mixed) * npost_ref[...])


def _layer_kernel(steps_per_row, m_steps, xn_ref, klr_ref, vs_ref, npre_ref, win_ref,
                  wsp_ref, bsp_ref, gva_ref, wgk_ref, bgk_ref, ltri_ref, utri_ref, gnb_ref,
                  wout_ref, npost_ref, out_ref, p0_ref, p1_ref, x0_ref, x1_ref, sf_ref, o_ref, mix_ref):
    i = pl.program_id(0)

    @pl.when(jnp.logical_or(i == 0, (i - 1) % steps_per_row == 0))
    def _():
        sf_ref[...] = jnp.zeros_like(sf_ref)

    def step(write_refs, read_refs):
        emit = _emitter(_projection_pieces(xn_ref, npre_ref, win_ref, *write_refs)
                        if write_refs is not None else [])
        for j in range(MAIN_SUB) if read_refs is not None else ():
            _mix(j, *read_refs, klr_ref, vs_ref, wsp_ref, bsp_ref, gva_ref, wgk_ref, bgk_ref,
                 ltri_ref, utri_ref, gnb_ref, wout_ref, npost_ref, out_ref, sf_ref, o_ref, mix_ref,
                 emit)
        emit(MAIN_SUB * P_COLS // PIECE_COLS)

    _ping_pong_steps(i, m_steps, step, (p0_ref, x0_ref), (p1_ref, x1_ref))


def _const_spec(shape):
    return pl.BlockSpec(shape, lambda *_: (0,) * len(shape))


def _hybrid_layer(x, wts):
    bsz, seq, _ = x.shape
    n_tiles = seq // TILE
    n_total = bsz * n_tiles

    params = pltpu.CompilerParams(dimension_semantics=("arbitrary",), vmem_limit_bytes=VMEM_LIMIT)

    n_steps = n_total // PRE_SUB

    def recurred_block(i):
        return (jnp.minimum(n_steps - i, n_steps - 1), 0, 0)

    klr_tiles, vs_tiles = pl.pallas_call(
        functools.partial(_bwd_state_kernel, n_tiles // PRE_SUB, n_steps),
        grid=(n_steps + 1,),
        in_specs=[
            pl.BlockSpec((1, PRE_SUB * TILE, D_MODEL),
                         lambda i: (jnp.maximum(n_steps - 1 - i, 0), 0, 0)),
            _const_spec((1, D_MODEL)),
            _const_spec((D_MODEL, PRE_COLS + PIECE_COLS)),
            _const_spec((LANES, D_K)),
            _const_spec((1, D_K)),
            _const_spec((TILE, TILE)),
        ],
        out_specs=[
            pl.BlockSpec((PRE_SUB, TILE, D_K + LANES), recurred_block),
            pl.BlockSpec((PRE_SUB, TILE + N_HEADS_B * HEAD_K, D_B), recurred_block),
        ],
        out_shape=[
            jax.ShapeDtypeStruct((n_total, TILE, D_K + LANES), F32),
            jax.ShapeDtypeStruct((n_total, TILE + N_HEADS_B * HEAD_K, D_B), BF16),
        ],
        scratch_shapes=[
            pltpu.VMEM((PRE_SUB * TILE, PRE_COLS), F32),
            pltpu.VMEM((PRE_SUB * TILE, PRE_COLS), F32),
            pltpu.VMEM((N_HEADS_B * HEAD_K, HEAD_V), F32),
        ],
        compiler_params=params,
    )(x.reshape(n_steps, PRE_SUB * TILE, D_MODEL), wts["npre"], wts["wkv"], wts["wgk_b"],
      wts["bgk_b"], wts["utri"])

    m_steps = n_total // MAIN_SUB

    def finished_block(i):
        return (jnp.maximum(i - 1, 0), 0, 0)

    y = pl.pallas_call(
        functools.partial(_layer_kernel, n_tiles // MAIN_SUB, m_steps),
        grid=(m_steps + 1,),
        in_specs=[
            pl.BlockSpec((1, MAIN_SUB * TILE, D_MODEL),
                         lambda i: (jnp.minimum(i, m_steps - 1), 0, 0)),
            pl.BlockSpec((MAIN_SUB, TILE, D_K + LANES), finished_block),
            pl.BlockSpec((MAIN_SUB, TILE + N_HEADS_B * HEAD_K, D_B), finished_block),
            _const_spec((1, D_MODEL)),
            _const_spec((D_MODEL, P_COLS)),
            _const_spec((N_HEADS_A, CHUNK_A, CHUNK_A)),
            _const_spec((N_HEADS_A, CHUNK_A, HEAD_A)),
            _const_spec((1, D_A)),
            _const_spec((LANES, 2 * D_K)),
            _const_spec((1, 2 * D_K)),
            _const_spec((TILE, TILE)),
            _const_spec((TILE, TILE)),
            _const_spec((1, HEAD_V)),
            _const_spec((D_A + D_B, D_MODEL + PIECE_COLS)),
            _const_spec((1, D_MODEL)),
        ],
        out_specs=pl.BlockSpec((1, MAIN_SUB * TILE, D_MODEL), finished_block),
        out_shape=jax.ShapeDtypeStruct((m_steps, MAIN_SUB * TILE, D_MODEL), x.dtype),
        scratch_shapes=[
            pltpu.VMEM((MAIN_SUB * TILE, P_COLS), F32),
            pltpu.VMEM((MAIN_SUB * TILE, P_COLS), F32),
            pltpu.VMEM((MAIN_SUB * TILE, D_MODEL), F32),
            pltpu.VMEM((MAIN_SUB * TILE, D_MODEL), F32),
            pltpu.VMEM((N_HEADS_B * HEAD_K, HEAD_V), F32),
            pltpu.VMEM((MAIN_SUB, TILE, D_B), F32),
            pltpu.VMEM((MAIN_SUB, TILE, D_A + D_B), BF16),
        ],
        compiler_params=params,
    )(x.reshape(m_steps, MAIN_SUB * TILE, D_MODEL), klr_tiles, vs_tiles, wts["npre"], wts["win"],
      wts["wsp"], wts["bsp"], wts["gva"], wts["wgk"], wts["bgk"], wts["ltri"], wts["utri"],
      wts["gnb"], wts["wout"], wts["npost"])
    return y.reshape(x.shape)


def _prepare_weights(norm_pre, w_in, w_sp, b_sp, g_v_a, w_gk_fwd, b_gk_fwd, w_gk_bwd, b_gk_bwd,
                     g_norm_b, w_out, norm_post):
    w_in_bf = w_in.astype(BF16)
    wlr = jnp.zeros((D_MODEL, PIECE_COLS), BF16).at[:, 0:2 * GATE_RANK].set(w_in_bf[:, OFF_LRF:D_IN])
    wgk = jnp.zeros((LANES, 2 * D_K), BF16)
    wgk = wgk.at[0:GATE_RANK, 0:D_K].set(w_gk_fwd.astype(BF16))
    wgk = wgk.at[GATE_RANK:2 * GATE_RANK, D_K:2 * D_K].set(w_gk_bwd.astype(BF16))
    pad_cols = jnp.zeros((D_MODEL, PIECE_COLS), BF16)
    pos = jnp.arange(TILE)
    same_chunk = (pos[:, None] // CHUNK_B) == (pos[None, :] // CHUNK_B)
    ltri = (same_chunk & (pos[None, :] <= pos[:, None])).astype(BF16)
    utri = (same_chunk & (pos[None, :] >= pos[:, None])).astype(BF16)
    return {
        "npre": norm_pre.reshape(1, D_MODEL),
        "win": jnp.concatenate([w_in_bf[:, OFF_U:OFF_K], w_in_bf[:, OFF_ZB:OFF_LRF]], axis=1),
        "wkv": jnp.concatenate([w_in_bf[:, OFF_K:OFF_ZB], wlr, pad_cols], axis=1),
        "wsp": w_sp.astype(BF16),
        "bsp": jnp.broadcast_to(b_sp[:, :, None], (N_HEADS_A, CHUNK_A, HEAD_A)),
        "gva": g_v_a.reshape(1, D_A),
        "wgk": wgk,
        "wgk_b": wgk[:, D_K:2 * D_K],
        "bgk": jnp.concatenate([b_gk_fwd, b_gk_bwd]).reshape(1, 2 * D_K),
        "bgk_b": b_gk_bwd.reshape(1, D_K),
        "ltri": ltri,
        "utri": utri,
        "gnb": g_norm_b.reshape(1, HEAD_V),
        "wout": jnp.concatenate([w_out.astype(BF16), pad_cols], axis=1),
        "npost": norm_post.reshape(1, D_MODEL),
    }


def kernel(x_prompt, x_sample, norm_pre, w_in, w_sp, b_sp, g_v_a, w_gk_fwd, b_gk_fwd,
           w_gk_bwd, b_gk_bwd, g_norm_b, w_out, norm_post):
    y_prompt, y_sample = x_prompt, x_sample
    for l in range(norm_pre.shape[0]):
        wts = _prepare_weights(norm_pre[l], w_in[l], w_sp[l], b_sp[l], g_v_a[l], w_gk_fwd[l],
                               b_gk_fwd[l], w_gk_bwd[l], b_gk_bwd[l], g_norm_b[l], w_out[l],
                               norm_post[l])
        y_prompt = _hybrid_layer(y_prompt, wts)
        y_sample = _hybrid_layer(y_sample, wts)
    return (y_prompt, y_sample)
```

```python
import functools

import jax
import jax.numpy as jnp
from jax import lax
from jax.experimental import pallas as pl
from jax.experimental.pallas import tpu as pltpu

F32 = jnp.float32
BF16 = jnp.bfloat16

D_MODEL = 1024
D_A = 512
N_HEADS_A = 4
HEAD_A = 128
CHUNK_A = 128
D_B = 512
N_HEADS_B = 4
HEAD_V = 128
D_K = 256
HEAD_K = 64
GATE_RANK = 16
GATE_NORMALIZER = 16.0
CHUNK_B = 64
EPS = 1e-6

OFF_U, OFF_V, OFF_ZA, OFF_Q, OFF_K, OFF_VB, OFF_ZB, OFF_LRF, OFF_LRB, D_IN = (
    0, 512, 1024, 1536, 1792, 2048, 2560, 3072, 3088, 3104)

LANES = 128
TILE = 256
N_CHUNK = TILE // CHUNK_B
N_PAIR = N_HEADS_B // 2
PIECE_COLS = 256
PRE_COLS = D_K + D_B + PIECE_COLS
PRE_SUB = 4
MAIN_SUB = 2
KLB_COLS = D_K + LANES + D_K
M_U, M_V, M_ZA, M_Q, M_ZB, P_COLS = 0, 512, 1024, 1536, 1792, 2304
VMEM_LIMIT = 56 * 1024 * 1024


def _dot(a, b):
    return jnp.dot(a, b, preferred_element_type=F32)


def _dot_nt(a, b):
    return lax.dot_general(a, b, (((1,), (1,)), ((), ())), preferred_element_type=F32)


def _gelu(x):
    return 0.5 * x * (1.0 + jnp.tanh(0.7978845608028654 * (x + 0.044715 * (x * x * x))))


def _silu(x):
    return x * (0.5 * jnp.tanh(0.5 * x) + 0.5)


def _log_sigmoid(x):
    return jnp.minimum(x, 0.0) - jnp.log1p(jnp.exp(-jnp.abs(x)))


def _rms_scale(x):
    return lax.rsqrt(jnp.mean(x * x, axis=-1, keepdims=True) + EPS)


def _chunk_cumsum(tri_ref, g):
    hi = g.astype(BF16)
    lo = (g - hi.astype(F32)).astype(BF16)
    tri = tri_ref[...]
    return _dot(tri, hi) + _dot(tri, lo)


def _decayed_keys(k, bcum, backward):
    kd, dec = [], []
    for c in range(N_CHUNK):
        r0 = c * CHUNK_B
        last = r0 if backward else r0 + CHUNK_B - 1
        b_last = bcum[last:last + 1, :]
        kd.append(k[r0:r0 + CHUNK_B, :] * jnp.exp(b_last - bcum[r0:r0 + CHUNK_B, :]))
        dec.append(jnp.exp(b_last))
    return jnp.concatenate(kd, axis=0), dec


def _chunk_kv(kt, v_par, c):
    cp, par = divmod(c, 2)
    parts = []
    for p in range(N_PAIR):
        full = _dot(kt[p * LANES:(p + 1) * LANES, cp * LANES:(cp + 1) * LANES],
                    v_par[cp][par][:, p * 2 * HEAD_V:(p + 1) * 2 * HEAD_V])
        parts.append(full[0:HEAD_K, 0:HEAD_V])
        parts.append(full[HEAD_K:2 * HEAD_K, HEAD_V:2 * HEAD_V])
    return jnp.concatenate(parts, axis=0)


def _decay_column(dec_row):
    return jnp.transpose(jnp.broadcast_to(dec_row, (LANES, D_K)))


def _parity_masked_values(vb):
    out = []
    row = lax.broadcasted_iota(jnp.int32, (2 * CHUNK_B, D_B), 0)
    for cp in range(N_CHUNK // 2):
        vp = vb[cp * 2 * CHUNK_B:(cp + 1) * 2 * CHUNK_B, :]
        zero = jnp.zeros_like(vp)
        out.append((jnp.where(row < CHUNK_B, vp, zero), jnp.where(row >= CHUNK_B, vp, zero)))
    return out


def _projection_pieces(x_ref, npre_ref, w_ref, p_ref, x_keep_ref=None):
    normed = {}

    def piece(r0, c0):
        def run():
            if r0 not in normed:
                x = x_ref[0, r0:r0 + TILE, :]
                if x_keep_ref is not None:
                    x_keep_ref[r0:r0 + TILE, :] = x
                normed[r0] = (x * _rms_scale(x) * npre_ref[...]).astype(BF16)
            p_ref[r0:r0 + TILE, c0:c0 + PIECE_COLS] = _dot(normed[r0], w_ref[:, c0:c0 + PIECE_COLS])
        return run

    return [piece(r0, c0) for r0 in range(0, p_ref.shape[0], TILE)
            for c0 in range(0, p_ref.shape[1], PIECE_COLS)]


def _emitter(pieces):
    def emit(n):
        for _ in range(min(n, len(pieces))):
            pieces.pop(0)()
    return emit


def _ping_pong_steps(i, n, step, even_refs, odd_refs):
    first, last = i == 0, i == n
    middle = jnp.logical_not(jnp.logical_or(first, last))

    @pl.when(first)
    def _():
        step(even_refs, None)

    @pl.when(jnp.logical_and(middle, i % 2 == 0))
    def _():
        step(even_refs, odd_refs)

    @pl.when(jnp.logical_and(middle, i % 2 == 1))
    def _():
        step(odd_refs, even_refs)

    @pl.when(last)
    def _():
        step(None, even_refs if n % 2 == 1 else odd_refs)


def _bwd_state_kernel(steps_per_row, n_steps, xn_ref, npre_ref, wkv_ref, wgk_ref, bgk_ref, utri_ref,
                      klr_ref, vs_ref, p0_ref, p1_ref, s_ref):
    i = pl.program_id(0)

    @pl.when(jnp.logical_or(i == 0, (n_steps - i) % steps_per_row == steps_per_row - 1))
    def _():
        s_ref[...] = jnp.zeros_like(s_ref)

    def step(write_refs, read_refs):
        emit = _emitter(_projection_pieces(xn_ref, npre_ref, wkv_ref, *write_refs)
                        if write_refs is not None else [])
        s = s_ref[...]
        for j in reversed(range(PRE_SUB) if read_refs is not None else ()):
            p_ref, r0 = read_refs[0], j * TILE
            emit(1)
            k = p_ref[r0:r0 + TILE, 0:D_K]
            vb = p_ref[r0:r0 + TILE, D_K:D_K + D_B].astype(BF16)
            plr = p_ref[r0:r0 + TILE, D_K + D_B:D_K + D_B + LANES]
            klr_ref[j, :, 0:D_K] = k
            klr_ref[j, :, D_K:D_K + LANES] = plr
            vs_ref[j, 0:TILE, :] = vb
            g = _log_sigmoid(_dot(plr.astype(BF16), wgk_ref[...]) + bgk_ref[...]) * (
                1.0 / GATE_NORMALIZER)
            emit(1)
            bcum = _chunk_cumsum(utri_ref, g)
            klr_ref[j, :, D_K + LANES:KLB_COLS] = bcum
            kdec, dec = _decayed_keys(k, bcum, backward=True)
            kt = jnp.transpose(kdec).astype(BF16)
            emit(1)
            v_par = _parity_masked_values(vb)
            kv = [_chunk_kv(kt, v_par, c) for c in range(N_CHUNK)]
            dec_col = [_decay_column(dec[c]) for c in range(N_CHUNK)]
            emit(1)
            for c in reversed(range(N_CHUNK)):
                vs_ref[j, TILE:TILE + N_HEADS_B * HEAD_K, c * HEAD_V:(c + 1) * HEAD_V] = (
                    s.astype(BF16))
                s = dec_col[c] * s + kv[c]
        emit(PRE_SUB * PRE_COLS // PIECE_COLS)
        s_ref[...] = s

    _ping_pong_steps(i, n_steps, step, (p0_ref,), (p1_ref,))


def _mix(j, p_ref, x_ref, klr_ref, vs_ref, wsp_ref, bsp_ref, gva_ref, wgk_ref, bgk_ref,
         ltri_ref, gnb_ref, wout_ref, npost_ref, out_ref, sf_ref, o_ref, mix_ref, emit):
    t0 = j * TILE

    emit(2)
    vg = _gelu(p_ref[t0:t0 + TILE, M_V:M_ZA])
    vc = vg - jnp.mean(vg, axis=-1, keepdims=True)
    vn = (vc * _rms_scale(vc) * gva_ref[...]).astype(BF16)
    for c in range(TILE // CHUNK_A):
        r0 = c * CHUNK_A
        for hd in range(N_HEADS_A):
            c0 = hd * HEAD_A
            sv = _dot(wsp_ref[hd], vn[r0:r0 + CHUNK_A, c0:c0 + HEAD_A]) + bsp_ref[hd]
            out_a = _gelu(p_ref[t0 + r0:t0 + r0 + CHUNK_A, M_U + c0:M_U + c0 + HEAD_A]) * sv
            out_a = out_a * _silu(
                p_ref[t0 + r0:t0 + r0 + CHUNK_A, M_ZA + c0:M_ZA + c0 + HEAD_A])
            mix_ref[j, r0:r0 + CHUNK_A, c0:c0 + HEAD_A] = out_a.astype(BF16)
            emit(1 if hd == 1 else 0)

    q = p_ref[t0:t0 + TILE, M_Q:M_ZB] * (HEAD_K ** -0.5)
    k = klr_ref[j, :, 0:D_K]
    vb = vs_ref[j, 0:TILE, :]
    plr = klr_ref[j, :, D_K:D_K + LANES]
    emit(1)
    g_fwd = _log_sigmoid(_dot(plr.astype(BF16), wgk_ref[...]) + bgk_ref[...]) * (
        1.0 / GATE_NORMALIZER)
    v_par = _parity_masked_values(vb)

    lane_head = lax.broadcasted_iota(jnp.int32, (CHUNK_B, D_K), 1) // HEAD_K
    lane_pos = lax.broadcasted_iota(jnp.int32, (CHUNK_B, D_K), 1) % CHUNK_B
    row_pos = lax.broadcasted_iota(jnp.int32, (CHUNK_B, D_K), 0)
    zero_blk = jnp.zeros((CHUNK_B, HEAD_V), BF16)

    dirs = []
    for d in range(2):
        backward = d == 1
        emit(1)
        bcum = klr_ref[j, :, D_K + LANES:KLB_COLS] if backward else _chunk_cumsum(ltri_ref, g_fwd)
        q_in = (q * jnp.exp(bcum)).astype(BF16)
        k_in = (k * jnp.exp(-bcum)).astype(BF16)
        keep = (row_pos < lane_pos) if backward else (row_pos >= lane_pos)
        dirs.append((q_in, k_in, keep))
        if not backward:
            kdec, dec = _decayed_keys(k, bcum, backward)
            kt = jnp.transpose(kdec).astype(BF16)

    p_blk, kv_blk, dec_col = {}, {}, {}
    for c in range(N_CHUNK):
        scores = []
        for q_in, k_in, _ in dirs:
            k_c = k_in[c * CHUNK_B:(c + 1) * CHUNK_B, :]
            k_bd = jnp.concatenate(
                [jnp.where(lane_head == hd, k_c, jnp.zeros_like(k_c)) for hd in range(N_HEADS_B)],
                axis=0)
            scores.append(_dot_nt(q_in[c * CHUNK_B:(c + 1) * CHUNK_B, :], k_bd))
        p_blk[c] = jnp.where(dirs[0][2], scores[0], scores[1]).astype(BF16)
    for c in range(N_CHUNK):
        kv_blk[c] = _chunk_kv(kt, v_par, c)
        dec_col[c] = _decay_column(dec[c])

    for c in range(N_CHUNK):
        r0 = c * CHUNK_B
        s_fwd = sf_ref[...]
        s_f = s_fwd.astype(BF16)
        s_b = vs_ref[j, TILE:TILE + N_HEADS_B * HEAD_K, c * HEAD_V:(c + 1) * HEAD_V]
        for p in range(N_PAIR):
            a0 = 2 * p * HEAD_V
            rows_a = slice(2 * p * HEAD_K, (2 * p + 1) * HEAD_K)
            rows_b = slice((2 * p + 1) * HEAD_K, (2 * p + 2) * HEAD_K)
            rhs = jnp.concatenate([
                jnp.concatenate([vb[r0:r0 + CHUNK_B, a0:a0 + HEAD_V], zero_blk], axis=1),
                jnp.concatenate([zero_blk, vb[r0:r0 + CHUNK_B, a0 + HEAD_V:a0 + 2 * HEAD_V]], axis=1),
                jnp.concatenate([s_f[rows_a, :], zero_blk], axis=1),
                jnp.concatenate([zero_blk, s_f[rows_b, :]], axis=1),
                jnp.concatenate([s_b[rows_a, :], zero_blk], axis=1),
                jnp.concatenate([zero_blk, s_b[rows_b, :]], axis=1),
            ], axis=0)
            lhs = jnp.concatenate(
                [p_blk[c][:, p * LANES:(p + 1) * LANES]]
                + [q_in[r0:r0 + CHUNK_B, p * LANES:(p + 1) * LANES] for q_in, _, _ in dirs],
                axis=1)
            o_ref[j, r0:r0 + CHUNK_B, a0:a0 + 2 * HEAD_V] = _dot(lhs, rhs)
        sf_ref[...] = dec_col[c] * s_fwd + kv_blk[c]

    emit(1)
    for hd in range(N_HEADS_B):
        c0 = hd * HEAD_V
        oh = o_ref[j, :, c0:c0 + HEAD_V]
        out_b = oh * _rms_scale(oh) * gnb_ref[...] * _silu(
            p_ref[t0:t0 + TILE, M_ZB + c0:M_ZB + c0 + HEAD_V])
        mix_ref[j, :, D_A + c0:D_A + c0 + HEAD_V] = out_b.astype(BF16)

    mixed = jnp.concatenate(
        [_dot(mix_ref[j], wout_ref[:, c0:c0 + PIECE_COLS]) for c0 in range(0, D_MODEL, PIECE_COLS)],
        axis=1)
    emit(1)
    out_ref[0, t0:t0 + TILE, :] = (
        x_ref[t0:t0 + TILE, :] + mixed * _rms_scale(mixed) * npost_ref[...])


def _layer_kernel(steps_per_row, m_steps, xn_ref, klr_ref, vs_ref, npre_ref, win_ref,
                  wsp_ref, bsp_ref, gva_ref, wgk_ref, bgk_ref, ltri_ref, gnb_ref,
                  wout_ref, npost_ref, out_ref, p0_ref, p1_ref, x0_ref, x1_ref, sf_ref, o_ref, mix_ref):
    i = pl.program_id(0)

    @pl.when(jnp.logical_or(i == 0, (i - 1) % steps_per_row == 0))
    def _():
        sf_ref[...] = jnp.zeros_like(sf_ref)

    def step(write_refs, read_refs):
        emit = _emitter(_projection_pieces(xn_ref, npre_ref, win_ref, *write_refs)
                        if write_refs is not None else [])
        for j in range(MAIN_SUB) if read_refs is not None else ():
            _mix(j, *read_refs, klr_ref, vs_ref, wsp_ref, bsp_ref, gva_ref, wgk_ref, bgk_ref,
                 ltri_ref, gnb_ref, wout_ref, npost_ref, out_ref, sf_ref, o_ref, mix_ref,
                 emit)
        emit(MAIN_SUB * P_COLS // PIECE_COLS)

    _ping_pong_steps(i, m_steps, step, (p0_ref, x0_ref), (p1_ref, x1_ref))


def _const_spec(shape):
    return pl.BlockSpec(shape, lambda *_: (0,) * len(shape))


def _hybrid_layer(x, wts):
    bsz, seq, _ = x.shape
    n_tiles = seq // TILE
    n_total = bsz * n_tiles

    params = pltpu.CompilerParams(dimension_semantics=("arbitrary",), vmem_limit_bytes=VMEM_LIMIT)

    n_steps = n_total // PRE_SUB

    def recurred_block(i):
        return (jnp.minimum(n_steps - i, n_steps - 1), 0, 0)

    klr_tiles, vs_tiles = pl.pallas_call(
        functools.partial(_bwd_state_kernel, n_tiles // PRE_SUB, n_steps),
        grid=(n_steps + 1,),
        in_specs=[
            pl.BlockSpec((1, PRE_SUB * TILE, D_MODEL),
                         lambda i: (jnp.maximum(n_steps - 1 - i, 0), 0, 0)),
            _const_spec((1, D_MODEL)),
            _const_spec((D_MODEL, PRE_COLS + PIECE_COLS)),
            _const_spec((LANES, D_K)),
            _const_spec((1, D_K)),
            _const_spec((TILE, TILE)),
        ],
        out_specs=[
            pl.BlockSpec((PRE_SUB, TILE, KLB_COLS), recurred_block),
            pl.BlockSpec((PRE_SUB, TILE + N_HEADS_B * HEAD_K, D_B), recurred_block),
        ],
        out_shape=[
            jax.ShapeDtypeStruct((n_total, TILE, KLB_COLS), F32),
            jax.ShapeDtypeStruct((n_total, TILE + N_HEADS_B * HEAD_K, D_B), BF16),
        ],
        scratch_shapes=[
            pltpu.VMEM((PRE_SUB * TILE, PRE_COLS), F32),
            pltpu.VMEM((PRE_SUB * TILE, PRE_COLS), F32),
            pltpu.VMEM((N_HEADS_B * HEAD_K, HEAD_V), F32),
        ],
        compiler_params=params,
    )(x.reshape(n_steps, PRE_SUB * TILE, D_MODEL), wts["npre"], wts["wkv"], wts["wgk_b"],
      wts["bgk_b"], wts["utri"])

    m_steps = n_total // MAIN_SUB

    def finished_block(i):
        return (jnp.maximum(i - 1, 0), 0, 0)

    y = pl.pallas_call(
        functools.partial(_layer_kernel, n_tiles // MAIN_SUB, m_steps),
        grid=(m_steps + 1,),
        in_specs=[
            pl.BlockSpec((1, MAIN_SUB * TILE, D_MODEL),
                         lambda i: (jnp.minimum(i, m_steps - 1), 0, 0)),
            pl.BlockSpec((MAIN_SUB, TILE, KLB_COLS), finished_block),
            pl.BlockSpec((MAIN_SUB, TILE + N_HEADS_B * HEAD_K, D_B), finished_block),
            _const_spec((1, D_MODEL)),
            _const_spec((D_MODEL, P_COLS)),
            _const_spec((N_HEADS_A, CHUNK_A, CHUNK_A)),
            _const_spec((N_HEADS_A, CHUNK_A, HEAD_A)),
            _const_spec((1, D_A)),
            _const_spec((LANES, D_K)),
            _const_spec((1, D_K)),
            _const_spec((TILE, TILE)),
            _const_spec((1, HEAD_V)),
            _const_spec((D_A + D_B, D_MODEL + PIECE_COLS)),
            _const_spec((1, D_MODEL)),
        ],
        out_specs=pl.BlockSpec((1, MAIN_SUB * TILE, D_MODEL), finished_block),
        out_shape=jax.ShapeDtypeStruct((m_steps, MAIN_SUB * TILE, D_MODEL), x.dtype),
        scratch_shapes=[
            pltpu.VMEM((MAIN_SUB * TILE, P_COLS), F32),
            pltpu.VMEM((MAIN_SUB * TILE, P_COLS), F32),
            pltpu.VMEM((MAIN_SUB * TILE, D_MODEL), F32),
            pltpu.VMEM((MAIN_SUB * TILE, D_MODEL), F32),
            pltpu.VMEM((N_HEADS_B * HEAD_K, HEAD_V), F32),
            pltpu.VMEM((MAIN_SUB, TILE, D_B), F32),
            pltpu.VMEM((MAIN_SUB, TILE, D_A + D_B), BF16),
        ],
        compiler_params=params,
    )(x.reshape(m_steps, MAIN_SUB * TILE, D_MODEL), klr_tiles, vs_tiles, wts["npre"], wts["win"],
      wts["wsp"], wts["bsp"], wts["gva"], wts["wgk_f"], wts["bgk_f"], wts["ltri"], wts["gnb"],
      wts["wout"], wts["npost"])
    return y.reshape(x.shape)


def _prepare_weights(norm_pre, w_in, w_sp, b_sp, g_v_a, w_gk_fwd, b_gk_fwd, w_gk_bwd, b_gk_bwd,
                     g_norm_b, w_out, norm_post):
    w_in_bf = w_in.astype(BF16)
    wlr = jnp.zeros((D_MODEL, PIECE_COLS), BF16).at[:, 0:2 * GATE_RANK].set(w_in_bf[:, OFF_LRF:D_IN])
    wgk = jnp.zeros((LANES, 2 * D_K), BF16)
    wgk = wgk.at[0:GATE_RANK, 0:D_K].set(w_gk_fwd.astype(BF16))
    wgk = wgk.at[GATE_RANK:2 * GATE_RANK, D_K:2 * D_K].set(w_gk_bwd.astype(BF16))
    pad_cols = jnp.zeros((D_MODEL, PIECE_COLS), BF16)
    pos = jnp.arange(TILE)
    same_chunk = (pos[:, None] // CHUNK_B) == (pos[None, :] // CHUNK_B)
    ltri = (same_chunk & (pos[None, :] <= pos[:, None])).astype(BF16)
    utri = (same_chunk & (pos[None, :] >= pos[:, None])).astype(BF16)
    return {
        "npre": norm_pre.reshape(1, D_MODEL),
        "win": jnp.concatenate([w_in_bf[:, OFF_U:OFF_K], w_in_bf[:, OFF_ZB:OFF_LRF]], axis=1),
        "wkv": jnp.concatenate([w_in_bf[:, OFF_K:OFF_ZB], wlr, pad_cols], axis=1),
        "wsp": w_sp.astype(BF16),
        "bsp": jnp.broadcast_to(b_sp[:, :, None], (N_HEADS_A, CHUNK_A, HEAD_A)),
        "gva": g_v_a.reshape(1, D_A),
        "wgk_f": wgk[:, 0:D_K],
        "wgk_b": wgk[:, D_K:2 * D_K],
        "bgk_f": b_gk_fwd.reshape(1, D_K),
        "bgk_b": b_gk_bwd.reshape(1, D_K),
        "ltri": ltri,
        "utri": utri,
        "gnb": g_norm_b.reshape(1, HEAD_V),
        "wout": jnp.concatenate([w_out.astype(BF16), pad_cols], axis=1),
        "npost": norm_post.reshape(1, D_MODEL),
    }


def kernel(x_prompt, x_sample, norm_pre, w_in, w_sp, b_sp, g_v_a, w_gk_fwd, b_gk_fwd,
           w_gk_bwd, b_gk_bwd, g_norm_b, w_out, norm_post):
    y_prompt, y_sample = x_prompt, x_sample
    for l in range(norm_pre.shape[0]):
        wts = _prepare_weights(norm_pre[l], w_in[l], w_sp[l], b_sp[l], g_v_a[l], w_gk_fwd[l],
                               b_gk_fwd[l], w_gk_bwd[l], b_gk_bwd[l], g_norm_b[l], w_out[l],
                               norm_post[l])
        y_prompt = _hybrid_layer(y_prompt, wts)
        y_sample = _hybrid_layer(y_sample, wts)
    return (y_prompt, y_sample)
```

```python
import functools

import jax
import jax.numpy as jnp
from jax import lax
from jax.experimental import pallas as pl
from jax.experimental.pallas import tpu as pltpu

F32 = jnp.float32
BF16 = jnp.bfloat16

D_MODEL = 1024
D_A = 512
N_HEADS_A = 4
HEAD_A = 128
CHUNK_A = 128
D_B = 512
N_HEADS_B = 4
HEAD_V = 128
D_K = 256
HEAD_K = 64
GATE_RANK = 16
GATE_NORMALIZER = 16.0
CHUNK_B = 64
EPS = 1e-6

OFF_U, OFF_V, OFF_ZA, OFF_Q, OFF_K, OFF_VB, OFF_ZB, OFF_LRF, OFF_LRB, D_IN = (
    0, 512, 1024, 1536, 1792, 2048, 2560, 3072, 3088, 3104)

LANES = 128
TILE = 256
N_CHUNK = TILE // CHUNK_B
N_PAIR = N_HEADS_B // 2
PIECE_COLS = 256
PRE_COLS = D_K + D_B + PIECE_COLS
PRE_SUB = 4
MAIN_SUB = 2
KLB_COLS = D_K + LANES + D_K
M_U, M_V, M_ZA, M_Q, M_ZB, P_COLS = 0, 512, 1024, 1536, 1792, 2304
VMEM_LIMIT = 56 * 1024 * 1024


def _dot(a, b):
    return jnp.dot(a, b, preferred_element_type=F32)


def _dot_nt(a, b):
    return lax.dot_general(a, b, (((1,), (1,)), ((), ())), preferred_element_type=F32)


def _gelu(x):
    return 0.5 * x * (1.0 + jnp.tanh(0.7978845608028654 * (x + 0.044715 * (x * x * x))))


def _silu(x):
    return x * (0.5 * jnp.tanh(0.5 * x) + 0.5)


def _log_sigmoid(x):
    return jnp.minimum(x, 0.0) - jnp.log1p(jnp.exp(-jnp.abs(x)))


def _rms_scale(x):
    return lax.rsqrt(jnp.mean(x * x, axis=-1, keepdims=True) + EPS)


def _chunk_cumsum(tri_ref, g):
    hi = g.astype(BF16)
    lo = (g - hi.astype(F32)).astype(BF16)
    tri = tri_ref[...]
    return _dot(tri, hi) + _dot(tri, lo)


def _decayed_keys(k, bcum, backward):
    kd, dec = [], []
    for c in range(N_CHUNK):
        r0 = c * CHUNK_B
        last = r0 if backward else r0 + CHUNK_B - 1
        b_last = bcum[last:last + 1, :]
        kd.append(k[r0:r0 + CHUNK_B, :] * jnp.exp(b_last - bcum[r0:r0 + CHUNK_B, :]))
        dec.append(jnp.exp(b_last))
    return jnp.concatenate(kd, axis=0), dec


def _chunk_kv(kt, v_par, c):
    cp, par = divmod(c, 2)
    parts = []
    for p in range(N_PAIR):
        full = _dot(kt[p * LANES:(p + 1) * LANES, cp * LANES:(cp + 1) * LANES],
                    v_par[cp][par][:, p * 2 * HEAD_V:(p + 1) * 2 * HEAD_V])
        parts.append(full[0:HEAD_K, 0:HEAD_V])
        parts.append(full[HEAD_K:2 * HEAD_K, HEAD_V:2 * HEAD_V])
    return jnp.concatenate(parts, axis=0)


def _decay_column(dec_row):
    return jnp.transpose(jnp.broadcast_to(dec_row, (LANES, D_K)))


def _parity_masked_values(vb):
    out = []
    row = lax.broadcasted_iota(jnp.int32, (2 * CHUNK_B, D_B), 0)
    for cp in range(N_CHUNK // 2):
        vp = vb[cp * 2 * CHUNK_B:(cp + 1) * 2 * CHUNK_B, :]
        zero = jnp.zeros_like(vp)
        out.append((jnp.where(row < CHUNK_B, vp, zero), jnp.where(row >= CHUNK_B, vp, zero)))
    return out


def _projection_pieces(x_ref, npre_ref, w_ref, p_ref, x_keep_ref=None):
    normed = {}

    def piece(r0, c0):
        def run():
            if r0 not in normed:
                x = x_ref[0, r0:r0 + TILE, :]
                if x_keep_ref is not None:
                    x_keep_ref[r0:r0 + TILE, :] = x
                normed[r0] = (x * _rms_scale(x) * npre_ref[...]).astype(BF16)
            p_ref[r0:r0 + TILE, c0:c0 + PIECE_COLS] = _dot(normed[r0], w_ref[:, c0:c0 + PIECE_COLS])
        return run

    return [piece(r0, c0) for r0 in range(0, p_ref.shape[0], TILE)
            for c0 in range(0, p_ref.shape[1], PIECE_COLS)]


def _emitter(pieces):
    def emit(n):
        for _ in range(min(n, len(pieces))):
            pieces.pop(0)()
    return emit


def _ping_pong_steps(i, n, step, even_refs, odd_refs):
    first, last = i == 0, i == n
    middle = jnp.logical_not(jnp.logical_or(first, last))

    @pl.when(first)
    def _():
        step(even_refs, None)

    @pl.when(jnp.logical_and(middle, i % 2 == 0))
    def _():
        step(even_refs, odd_refs)

    @pl.when(jnp.logical_and(middle, i % 2 == 1))
    def _():
        step(odd_refs, even_refs)

    @pl.when(last)
    def _():
        step(None, even_refs if n % 2 == 1 else odd_refs)


def _bwd_state_kernel(steps_per_row, n_steps, xn_ref, npre_ref, wkv_ref, wgk_ref, bgk_ref, utri_ref,
                      klr_ref, vs_ref, p0_ref, p1_ref, s_ref):
    i = pl.program_id(0)

    @pl.when(jnp.logical_or(i == 0, (n_steps - i) % steps_per_row == steps_per_row - 1))
    def _():
        s_ref[...] = jnp.zeros_like(s_ref)

    def step(write_refs, read_refs):
        emit = _emitter(_projection_pieces(xn_ref, npre_ref, wkv_ref, *write_refs)
                        if write_refs is not None else [])
        s = s_ref[...]
        for j in reversed(range(PRE_SUB) if read_refs is not None else ()):
            p_ref, r0 = read_refs[0], j * TILE
            emit(1)
            k = p_ref[r0:r0 + TILE, 0:D_K]
            vb = p_ref[r0:r0 + TILE, D_K:D_K + D_B].astype(BF16)
            plr = p_ref[r0:r0 + TILE, D_K + D_B:D_K + D_B + LANES]
            klr_ref[j, :, 0:D_K] = k
            klr_ref[j, :, D_K:D_K + LANES] = plr
            vs_ref[j, 0:TILE, :] = vb
            g = _log_sigmoid(_dot(plr.astype(BF16), wgk_ref[...]) + bgk_ref[...]) * (
                1.0 / GATE_NORMALIZER)
            emit(1)
            bcum = _chunk_cumsum(utri_ref, g)
            klr_ref[j, :, D_K + LANES:KLB_COLS] = bcum
            kdec, dec = _decayed_keys(k, bcum, backward=True)
            kt = jnp.transpose(kdec).astype(BF16)
            emit(1)
            v_par = _parity_masked_values(vb)
            kv = [_chunk_kv(kt, v_par, c) for c in range(N_CHUNK)]
            dec_col = [_decay_column(dec[c]) for c in range(N_CHUNK)]
            emit(1)
            for c in reversed(range(N_CHUNK)):
                vs_ref[j, TILE:TILE + N_HEADS_B * HEAD_K, c * HEAD_V:(c + 1) * HEAD_V] = (
                    s.astype(BF16))
                s = dec_col[c] * s + kv[c]
        emit(PRE_SUB * PRE_COLS // PIECE_COLS)
        s_ref[...] = s

    _ping_pong_steps(i, n_steps, step, (p0_ref,), (p1_ref,))


def _mix(j, p_ref, x_ref, klr_ref, vs_ref, wsp_ref, bsp_ref, gva_ref, wgk_ref, bgk_ref,
         ltri_ref, gnb_ref, wout_ref, npost_ref, out_ref, sf_ref, o_ref, mix_ref, emit):
    t0 = j * TILE

    emit(2)
    vg = _gelu(p_ref[t0:t0 + TILE, M_V:M_ZA])
    vc = vg - jnp.mean(vg, axis=-1, keepdims=True)
    vn = (vc * _rms_scale(vc) * gva_ref[...]).astype(BF16)
    zero_a = jnp.zeros((CHUNK_A, HEAD_A), BF16)
    for c in range(TILE // CHUNK_A):
        r0 = c * CHUNK_A
        for pp in range(N_HEADS_A // 2):
            ca, cb = 2 * pp * HEAD_A, (2 * pp + 1) * HEAD_A
            sv_pair = _dot(wsp_ref[pp], jnp.concatenate([
                jnp.concatenate([vn[r0:r0 + CHUNK_A, ca:ca + HEAD_A], zero_a], axis=1),
                jnp.concatenate([zero_a, vn[r0:r0 + CHUNK_A, cb:cb + HEAD_A]], axis=1)], axis=0))
            for hd, c0 in ((2 * pp, ca), (2 * pp + 1, cb)):
                sv = sv_pair[:, c0 - ca:c0 - ca + HEAD_A] + bsp_ref[hd]
                out_a = _gelu(p_ref[t0 + r0:t0 + r0 + CHUNK_A, M_U + c0:M_U + c0 + HEAD_A]) * sv
                out_a = out_a * _silu(
                    p_ref[t0 + r0:t0 + r0 + CHUNK_A, M_ZA + c0:M_ZA + c0 + HEAD_A])
                mix_ref[j, r0:r0 + CHUNK_A, c0:c0 + HEAD_A] = out_a.astype(BF16)
            emit(1 if pp == 0 else 0)

    q = p_ref[t0:t0 + TILE, M_Q:M_ZB] * (HEAD_K ** -0.5)
    k = klr_ref[j, :, 0:D_K]
    vb = vs_ref[j, 0:TILE, :]
    plr = klr_ref[j, :, D_K:D_K + LANES]
    emit(1)
    g_fwd = _log_sigmoid(_dot(plr.astype(BF16), wgk_ref[...]) + bgk_ref[...]) * (
        1.0 / GATE_NORMALIZER)
    v_par = _parity_masked_values(vb)

    lane_head = lax.broadcasted_iota(jnp.int32, (CHUNK_B, D_K), 1) // HEAD_K
    lane_pos = lax.broadcasted_iota(jnp.int32, (CHUNK_B, D_K), 1) % CHUNK_B
    row_pos = lax.broadcasted_iota(jnp.int32, (CHUNK_B, D_K), 0)
    zero_blk = jnp.zeros((CHUNK_B, HEAD_V), BF16)

    dirs = []
    for d in range(2):
        backward = d == 1
        emit(1)
        bcum = klr_ref[j, :, D_K + LANES:KLB_COLS] if backward else _chunk_cumsum(ltri_ref, g_fwd)
        q_in = (q * jnp.exp(bcum)).astype(BF16)
        k_in = (k * jnp.exp(-bcum)).astype(BF16)
        keep = (row_pos < lane_pos) if backward else (row_pos >= lane_pos)
        dirs.append((q_in, k_in, keep))
        if not backward:
            kdec, dec = _decayed_keys(k, bcum, backward)
            kt = jnp.transpose(kdec).astype(BF16)

    p_blk, kv_blk, dec_col = {}, {}, {}
    for c in range(N_CHUNK):
        scores = []
        for q_in, k_in, _ in dirs:
            k_c = k_in[c * CHUNK_B:(c + 1) * CHUNK_B, :]
            k_bd = jnp.concatenate(
                [jnp.where(lane_head == hd, k_c, jnp.zeros_like(k_c)) for hd in range(N_HEADS_B)],
                axis=0)
            scores.append(_dot_nt(q_in[c * CHUNK_B:(c + 1) * CHUNK_B, :], k_bd))
        p_blk[c] = jnp.where(dirs[0][2], scores[0], scores[1]).astype(BF16)
    for c in range(N_CHUNK):
        kv_blk[c] = _chunk_kv(kt, v_par, c)
        dec_col[c] = _decay_column(dec[c])

    for c in range(N_CHUNK):
        r0 = c * CHUNK_B
        s_fwd = sf_ref[...]
        s_f = s_fwd.astype(BF16)
        s_b = vs_ref[j, TILE:TILE + N_HEADS_B * HEAD_K, c * HEAD_V:(c + 1) * HEAD_V]
        for p in range(N_PAIR):
            a0 = 2 * p * HEAD_V
            rows_a = slice(2 * p * HEAD_K, (2 * p + 1) * HEAD_K)
            rows_b = slice((2 * p + 1) * HEAD_K, (2 * p + 2) * HEAD_K)
            rhs = jnp.concatenate([
                jnp.concatenate([vb[r0:r0 + CHUNK_B, a0:a0 + HEAD_V], zero_blk], axis=1),
                jnp.concatenate([zero_blk, vb[r0:r0 + CHUNK_B, a0 + HEAD_V:a0 + 2 * HEAD_V]], axis=1),
                jnp.concatenate([s_f[rows_a, :], zero_blk], axis=1),
                jnp.concatenate([zero_blk, s_f[rows_b, :]], axis=1),
                jnp.concatenate([s_b[rows_a, :], zero_blk], axis=1),
                jnp.concatenate([zero_blk, s_b[rows_b, :]], axis=1),
            ], axis=0)
            lhs = jnp.concatenate(
                [p_blk[c][:, p * LANES:(p + 1) * LANES]]
                + [q_in[r0:r0 + CHUNK_B, p * LANES:(p + 1) * LANES] for q_in, _, _ in dirs],
                axis=1)
            o_ref[j, r0:r0 + CHUNK_B, a0:a0 + 2 * HEAD_V] = _dot(lhs, rhs)
        sf_ref[...] = dec_col[c] * s_fwd + kv_blk[c]

    emit(1)
    for hd in range(N_HEADS_B):
        c0 = hd * HEAD_V
        oh = o_ref[j, :, c0:c0 + HEAD_V]
        out_b = oh * _rms_scale(oh) * gnb_ref[...] * _silu(
            p_ref[t0:t0 + TILE, M_ZB + c0:M_ZB + c0 + HEAD_V])
        mix_ref[j, :, D_A + c0:D_A + c0 + HEAD_V] = out_b.astype(BF16)

    mixed = jnp.concatenate(
        [_dot(mix_ref[j], wout_ref[:, c0:c0 + PIECE_COLS]) for c0 in range(0, D_MODEL, PIECE_COLS)],
        axis=1)
    emit(1)
    out_ref[0, t0:t0 + TILE, :] = (
        x_ref[t0:t0 + TILE, :] + mixed * _rms_scale(mixed) * npost_ref[...])


def _layer_kernel(steps_per_row, m_steps, xn_ref, klr_ref, vs_ref, npre_ref, win_ref,
                  wsp_ref, bsp_ref, gva_ref, wgk_ref, bgk_ref, ltri_ref, gnb_ref,
                  wout_ref, npost_ref, out_ref, p0_ref, p1_ref, x0_ref, x1_ref, sf_ref, o_ref, mix_ref):
    i = pl.program_id(0)

    @pl.when(jnp.logical_or(i == 0, (i - 1) % steps_per_row == 0))
    def _():
        sf_ref[...] = jnp.zeros_like(sf_ref)

    def step(write_refs, read_refs):
        emit = _emitter(_projection_pieces(xn_ref, npre_ref, win_ref, *write_refs)
                        if write_refs is not None else [])
        for j in range(MAIN_SUB) if read_refs is not None else ():
            _mix(j, *read_refs, klr_ref, vs_ref, wsp_ref, bsp_ref, gva_ref, wgk_ref, bgk_ref,
                 ltri_ref, gnb_ref, wout_ref, npost_ref, out_ref, sf_ref, o_ref, mix_ref,
                 emit)
        emit(MAIN_SUB * P_COLS // PIECE_COLS)

    _ping_pong_steps(i, m_steps, step, (p0_ref, x0_ref), (p1_ref, x1_ref))


def _const_spec(shape):
    return pl.BlockSpec(shape, lambda *_: (0,) * len(shape))


def _hybrid_layer(x, wts):
    bsz, seq, _ = x.shape
    n_tiles = seq // TILE
    n_total = bsz * n_tiles

    params = pltpu.CompilerParams(dimension_semantics=("arbitrary",), vmem_limit_bytes=VMEM_LIMIT)

    n_steps = n_total // PRE_SUB

    def recurred_block(i):
        return (jnp.minimum(n_steps - i, n_steps - 1), 0, 0)

    klr_tiles, vs_tiles = pl.pallas_call(
        functools.partial(_bwd_state_kernel, n_tiles // PRE_SUB, n_steps),
        grid=(n_steps + 1,),
        in_specs=[
            pl.BlockSpec((1, PRE_SUB * TILE, D_MODEL),
                         lambda i: (jnp.maximum(n_steps - 1 - i, 0), 0, 0)),
            _const_spec((1, D_MODEL)),
            _const_spec((D_MODEL, PRE_COLS + PIECE_COLS)),
            _const_spec((LANES, D_K)),
            _const_spec((1, D_K)),
            _const_spec((TILE, TILE)),
        ],
        out_specs=[
            pl.BlockSpec((PRE_SUB, TILE, KLB_COLS), recurred_block),
            pl.BlockSpec((PRE_SUB, TILE + N_HEADS_B * HEAD_K, D_B), recurred_block),
        ],
        out_shape=[
            jax.ShapeDtypeStruct((n_total, TILE, KLB_COLS), F32),
            jax.ShapeDtypeStruct((n_total, TILE + N_HEADS_B * HEAD_K, D_B), BF16),
        ],
        scratch_shapes=[
            pltpu.VMEM((PRE_SUB * TILE, PRE_COLS), F32),
            pltpu.VMEM((PRE_SUB * TILE, PRE_COLS), F32),
            pltpu.VMEM((N_HEADS_B * HEAD_K, HEAD_V), F32),
        ],
        compiler_params=params,
    )(x.reshape(n_steps, PRE_SUB * TILE, D_MODEL), wts["npre"], wts["wkv"], wts["wgk_b"],
      wts["bgk_b"], wts["utri"])

    m_steps = n_total // MAIN_SUB

    def finished_block(i):
        return (jnp.maximum(i - 1, 0), 0, 0)

    y = pl.pallas_call(
        functools.partial(_layer_kernel, n_tiles // MAIN_SUB, m_steps),
        grid=(m_steps + 1,),
        in_specs=[
            pl.BlockSpec((1, MAIN_SUB * TILE, D_MODEL),
                         lambda i: (jnp.minimum(i, m_steps - 1), 0, 0)),
            pl.BlockSpec((MAIN_SUB, TILE, KLB_COLS), finished_block),
            pl.BlockSpec((MAIN_SUB, TILE + N_HEADS_B * HEAD_K, D_B), finished_block),
            _const_spec((1, D_MODEL)),
            _const_spec((D_MODEL, P_COLS)),
            _const_spec((N_HEADS_A // 2, CHUNK_A, 2 * CHUNK_A)),
            _const_spec((N_HEADS_A, CHUNK_A, HEAD_A)),
            _const_spec((1, D_A)),
            _const_spec((LANES, D_K)),
            _const_spec((1, D_K)),
            _const_spec((TILE, TILE)),
            _const_spec((1, HEAD_V)),
            _const_spec((D_A + D_B, D_MODEL + PIECE_COLS)),
            _const_spec((1, D_MODEL)),
        ],
        out_specs=pl.BlockSpec((1, MAIN_SUB * TILE, D_MODEL), finished_block),
        out_shape=jax.ShapeDtypeStruct((m_steps, MAIN_SUB * TILE, D_MODEL), x.dtype),
        scratch_shapes=[
            pltpu.VMEM((MAIN_SUB * TILE, P_COLS), F32),
            pltpu.VMEM((MAIN_SUB * TILE, P_COLS), F32),
            pltpu.VMEM((MAIN_SUB * TILE, D_MODEL), F32),
            pltpu.VMEM((MAIN_SUB * TILE, D_MODEL), F32),
            pltpu.VMEM((N_HEADS_B * HEAD_K, HEAD_V), F32),
            pltpu.VMEM((MAIN_SUB, TILE, D_B), F32),
            pltpu.VMEM((MAIN_SUB, TILE, D_A + D_B), BF16),
        ],
        compiler_params=params,
    )(x.reshape(m_steps, MAIN_SUB * TILE, D_MODEL), klr_tiles, vs_tiles, wts["npre"], wts["win"],
      wts["wsp"], wts["bsp"], wts["gva"], wts["wgk_f"], wts["bgk_f"], wts["ltri"], wts["gnb"],
      wts["wout"], wts["npost"])
    return y.reshape(x.shape)


def _prepare_weights(norm_pre, w_in, w_sp, b_sp, g_v_a, w_gk_fwd, b_gk_fwd, w_gk_bwd, b_gk_bwd,
                     g_norm_b, w_out, norm_post):
    w_in_bf = w_in.astype(BF16)
    wlr = jnp.zeros((D_MODEL, PIECE_COLS), BF16).at[:, 0:2 * GATE_RANK].set(w_in_bf[:, OFF_LRF:D_IN])
    wgk = jnp.zeros((LANES, 2 * D_K), BF16)
    wgk = wgk.at[0:GATE_RANK, 0:D_K].set(w_gk_fwd.astype(BF16))
    wgk = wgk.at[GATE_RANK:2 * GATE_RANK, D_K:2 * D_K].set(w_gk_bwd.astype(BF16))
    pad_cols = jnp.zeros((D_MODEL, PIECE_COLS), BF16)
    pos = jnp.arange(TILE)
    same_chunk = (pos[:, None] // CHUNK_B) == (pos[None, :] // CHUNK_B)
    ltri = (same_chunk & (pos[None, :] <= pos[:, None])).astype(BF16)
    utri = (same_chunk & (pos[None, :] >= pos[:, None])).astype(BF16)
    return {
        "npre": norm_pre.reshape(1, D_MODEL),
        "win": jnp.concatenate([w_in_bf[:, OFF_U:OFF_K], w_in_bf[:, OFF_ZB:OFF_LRF]], axis=1),
        "wkv": jnp.concatenate([w_in_bf[:, OFF_K:OFF_ZB], wlr, pad_cols], axis=1),
        "wsp": jnp.concatenate([w_sp[0::2], w_sp[1::2]], axis=-1).astype(BF16),
        "bsp": jnp.broadcast_to(b_sp[:, :, None], (N_HEADS_A, CHUNK_A, HEAD_A)),
        "gva": g_v_a.reshape(1, D_A),
        "wgk_f": wgk[:, 0:D_K],
        "wgk_b": wgk[:, D_K:2 * D_K],
        "bgk_f": b_gk_fwd.reshape(1, D_K),
        "bgk_b": b_gk_bwd.reshape(1, D_K),
        "ltri": ltri,
        "utri": utri,
        "gnb": g_norm_b.reshape(1, HEAD_V),
        "wout": jnp.concatenate([w_out.astype(BF16), pad_cols], axis=1),
        "npost": norm_post.reshape(1, D_MODEL),
    }


def kernel(x_prompt, x_sample, norm_pre, w_in, w_sp, b_sp, g_v_a, w_gk_fwd, b_gk_fwd,
           w_gk_bwd, b_gk_bwd, g_norm_b, w_out, norm_post):
    y_prompt, y_sample = x_prompt, x_sample
    for l in range(norm_pre.shape[0]):
        wts = _prepare_weights(norm_pre[l], w_in[l], w_sp[l], b_sp[l], g_v_a[l], w_gk_fwd[l],
                               b_gk_fwd[l], w_gk_bwd[l], b_gk_bwd[l], g_norm_b[l], w_out[l],
                               norm_post[l])
        y_prompt = _hybrid_layer(y_prompt, wts)
        y_sample = _hybrid_layer(y_sample, wts)
    return (y_prompt, y_sample)
```

```python
import functools

import jax
import jax.numpy as jnp
from jax import lax
from jax.experimental import pallas as pl
from jax.experimental.pallas import tpu as pltpu

F32 = jnp.float32
BF16 = jnp.bfloat16

D_MODEL = 1024
D_A = 512
N_HEADS_A = 4
HEAD_A = 128
CHUNK_A = 128
D_B = 512
N_HEADS_B = 4
HEAD_V = 128
D_K = 256
HEAD_K = 64
GATE_RANK = 16
GATE_NORMALIZER = 16.0
CHUNK_B = 64
EPS = 1e-6

OFF_U, OFF_V, OFF_ZA, OFF_Q, OFF_K, OFF_VB, OFF_ZB, OFF_LRF, OFF_LRB, D_IN = (
    0, 512, 1024, 1536, 1792, 2048, 2560, 3072, 3088, 3104)

LANES = 128
TILE = 256
N_CHUNK = TILE // CHUNK_B
N_PAIR = N_HEADS_B // 2
PIECE_COLS = 256
PRE_COLS = D_K + D_B + PIECE_COLS
PRE_SUB = 4
MAIN_SUB = 2
KLB_COLS = D_K + LANES + D_K
M_U, M_V, M_ZA, M_Q, M_ZB, P_COLS = 0, 512, 1024, 1536, 1792, 2304
VMEM_LIMIT = 56 * 1024 * 1024


def _dot(a, b):
    return jnp.dot(a, b, preferred_element_type=F32)


def _dot_nt(a, b):
    return lax.dot_general(a, b, (((1,), (1,)), ((), ())), preferred_element_type=F32)


def _gelu(x):
    return 0.5 * x * (1.0 + jnp.tanh(0.7978845608028654 * (x + 0.044715 * (x * x * x))))


def _silu(x):
    return x * (0.5 * jnp.tanh(0.5 * x) + 0.5)


def _log_sigmoid(x):
    return jnp.minimum(x, 0.0) - jnp.log1p(jnp.exp(-jnp.abs(x)))


def _rms_scale(x):
    return lax.rsqrt(jnp.mean(x * x, axis=-1, keepdims=True) + EPS)


def _chunk_cumsum(tri_ref, g):
    hi = g.astype(BF16)
    lo = (g - hi.astype(F32)).astype(BF16)
    tri = tri_ref[...]
    return _dot(tri, hi) + _dot(tri, lo)


def _decayed_keys(k, bcum, backward):
    kd, dec = [], []
    for c in range(N_CHUNK):
        r0 = c * CHUNK_B
        last = r0 if backward else r0 + CHUNK_B - 1
        b_last = bcum[last:last + 1, :]
        kd.append(k[r0:r0 + CHUNK_B, :] * jnp.exp(b_last - bcum[r0:r0 + CHUNK_B, :]))
        dec.append(jnp.exp(b_last))
    return jnp.concatenate(kd, axis=0), dec


def _all_chunk_kv(kt, v_par):
    out = [[None] * N_HEADS_B for _ in range(N_CHUNK)]
    for cp in range(N_CHUNK // 2):
        for p in range(N_PAIR):
            cols = slice(p * 2 * HEAD_V, (p + 1) * 2 * HEAD_V)
            full = _dot(kt[p * LANES:(p + 1) * LANES, cp * LANES:(cp + 1) * LANES],
                        jnp.concatenate([v_par[cp][0][:, cols], v_par[cp][1][:, cols]], axis=1))
            for par in range(2):
                blk = full[:, par * 2 * HEAD_V:(par + 1) * 2 * HEAD_V]
                out[2 * cp + par][2 * p] = blk[0:HEAD_K, 0:HEAD_V]
                out[2 * cp + par][2 * p + 1] = blk[HEAD_K:2 * HEAD_K, HEAD_V:2 * HEAD_V]
    return [jnp.concatenate(parts, axis=0) for parts in out]


def _decay_column(dec_row):
    return jnp.transpose(jnp.broadcast_to(dec_row, (LANES, D_K)))


def _parity_masked_values(vb):
    out = []
    row = lax.broadcasted_iota(jnp.int32, (2 * CHUNK_B, D_B), 0)
    for cp in range(N_CHUNK // 2):
        vp = vb[cp * 2 * CHUNK_B:(cp + 1) * 2 * CHUNK_B, :]
        zero = jnp.zeros_like(vp)
        out.append((jnp.where(row < CHUNK_B, vp, zero), jnp.where(row >= CHUNK_B, vp, zero)))
    return out


def _projection_pieces(x_ref, npre_ref, w_ref, p_ref, x_keep_ref=None):
    normed = {}

    def piece(r0, c0):
        def run():
            if r0 not in normed:
                x = x_ref[0, r0:r0 + TILE, :]
                if x_keep_ref is not None:
                    x_keep_ref[r0:r0 + TILE, :] = x
                normed[r0] = (x * _rms_scale(x) * npre_ref[...]).astype(BF16)
            p_ref[r0:r0 + TILE, c0:c0 + PIECE_COLS] = _dot(normed[r0], w_ref[:, c0:c0 + PIECE_COLS])
        return run

    return [piece(r0, c0) for r0 in range(0, p_ref.shape[0], TILE)
            for c0 in range(0, p_ref.shape[1], PIECE_COLS)]


def _emitter(pieces):
    def emit(n):
        for _ in range(min(n, len(pieces))):
            pieces.pop(0)()
    return emit


def _ping_pong_steps(i, n, step, even_refs, odd_refs):
    first, last = i == 0, i == n
    middle = jnp.logical_not(jnp.logical_or(first, last))

    @pl.when(first)
    def _():
        step(even_refs, None)

    @pl.when(jnp.logical_and(middle, i % 2 == 0))
    def _():
        step(even_refs, odd_refs)

    @pl.when(jnp.logical_and(middle, i % 2 == 1))
    def _():
        step(odd_refs, even_refs)

    @pl.when(last)
    def _():
        step(None, even_refs if n % 2 == 1 else odd_refs)


def _bwd_state_kernel(steps_per_row, n_steps, xn_ref, npre_ref, wkv_ref, wgk_ref, bgk_ref, utri_ref,
                      klr_ref, vs_ref, p0_ref, p1_ref, s_ref):
    i = pl.program_id(0)

    @pl.when(jnp.logical_or(i == 0, (n_steps - i) % steps_per_row == steps_per_row - 1))
    def _():
        s_ref[...] = jnp.zeros_like(s_ref)

    def step(write_refs, read_refs):
        emit = _emitter(_projection_pieces(xn_ref, npre_ref, wkv_ref, *write_refs)
                        if write_refs is not None else [])
        s = s_ref[...]
        for j in reversed(range(PRE_SUB) if read_refs is not None else ()):
            p_ref, r0 = read_refs[0], j * TILE
            emit(1)
            k = p_ref[r0:r0 + TILE, 0:D_K]
            vb = p_ref[r0:r0 + TILE, D_K:D_K + D_B].astype(BF16)
            plr = p_ref[r0:r0 + TILE, D_K + D_B:D_K + D_B + LANES]
            klr_ref[j, :, 0:D_K] = k
            klr_ref[j, :, D_K:D_K + LANES] = plr
            vs_ref[j, 0:TILE, :] = vb
            g = _log_sigmoid(_dot(plr.astype(BF16), wgk_ref[...]) + bgk_ref[...]) * (
                1.0 / GATE_NORMALIZER)
            emit(1)
            bcum = _chunk_cumsum(utri_ref, g)
            klr_ref[j, :, D_K + LANES:KLB_COLS] = bcum
            kdec, dec = _decayed_keys(k, bcum, backward=True)
            kt = jnp.transpose(kdec).astype(BF16)
            emit(1)
            v_par = _parity_masked_values(vb)
            kv = _all_chunk_kv(kt, v_par)
            dec_col = [_decay_column(dec[c]) for c in range(N_CHUNK)]
            emit(1)
            for c in reversed(range(N_CHUNK)):
                vs_ref[j, TILE:TILE + N_HEADS_B * HEAD_K, c * HEAD_V:(c + 1) * HEAD_V] = (
                    s.astype(BF16))
                s = dec_col[c] * s + kv[c]
        emit(PRE_SUB * PRE_COLS // PIECE_COLS)
        s_ref[...] = s

    _ping_pong_steps(i, n_steps, step, (p0_ref,), (p1_ref,))


def _mix(j, p_ref, x_ref, klr_ref, vs_ref, wsp_ref, bsp_ref, gva_ref, wgk_ref, bgk_ref,
         ltri_ref, gnb_ref, wout_ref, npost_ref, out_ref, sf_ref, o_ref, mix_ref, emit):
    t0 = j * TILE

    emit(2)
    vg = _gelu(p_ref[t0:t0 + TILE, M_V:M_ZA])
    vc = vg - jnp.mean(vg, axis=-1, keepdims=True)
    vn = (vc * _rms_scale(vc) * gva_ref[...]).astype(BF16)
    zero_a = jnp.zeros((CHUNK_A, HEAD_A), BF16)
    for c in range(TILE // CHUNK_A):
        r0 = c * CHUNK_A
        for pp in range(N_HEADS_A // 2):
            ca, cb = 2 * pp * HEAD_A, (2 * pp + 1) * HEAD_A
            sv_pair = _dot(wsp_ref[pp], jnp.concatenate([
                jnp.concatenate([vn[r0:r0 + CHUNK_A, ca:ca + HEAD_A], zero_a], axis=1),
                jnp.concatenate([zero_a, vn[r0:r0 + CHUNK_A, cb:cb + HEAD_A]], axis=1)], axis=0))
            for hd, c0 in ((2 * pp, ca), (2 * pp + 1, cb)):
                sv = sv_pair[:, c0 - ca:c0 - ca + HEAD_A] + bsp_ref[hd]
                out_a = _gelu(p_ref[t0 + r0:t0 + r0 + CHUNK_A, M_U + c0:M_U + c0 + HEAD_A]) * sv
                out_a = out_a * _silu(
                    p_ref[t0 + r0:t0 + r0 + CHUNK_A, M_ZA + c0:M_ZA + c0 + HEAD_A])
                mix_ref[j, r0:r0 + CHUNK_A, c0:c0 + HEAD_A] = out_a.astype(BF16)
            emit(1 if pp == 0 else 0)

    q = p_ref[t0:t0 + TILE, M_Q:M_ZB] * (HEAD_K ** -0.5)
    k = klr_ref[j, :, 0:D_K]
    vb = vs_ref[j, 0:TILE, :]
    plr = klr_ref[j, :, D_K:D_K + LANES]
    emit(1)
    g_fwd = _log_sigmoid(_dot(plr.astype(BF16), wgk_ref[...]) + bgk_ref[...]) * (
        1.0 / GATE_NORMALIZER)
    v_par = _parity_masked_values(vb)

    lane_head = lax.broadcasted_iota(jnp.int32, (CHUNK_B, D_K), 1) // HEAD_K
    lane_pos = lax.broadcasted_iota(jnp.int32, (CHUNK_B, D_K), 1) % CHUNK_B
    row_pos = lax.broadcasted_iota(jnp.int32, (CHUNK_B, D_K), 0)
    zero_blk = jnp.zeros((CHUNK_B, HEAD_V), BF16)

    dirs = []
    for d in range(2):
        backward = d == 1
        emit(1)
        bcum = klr_ref[j, :, D_K + LANES:KLB_COLS] if backward else _chunk_cumsum(ltri_ref, g_fwd)
        q_in = (q * jnp.exp(bcum)).astype(BF16)
        k_in = (k * jnp.exp(-bcum)).astype(BF16)
        keep = (row_pos < lane_pos) if backward else (row_pos >= lane_pos)
        dirs.append((q_in, k_in, keep))
        if not backward:
            kdec, dec = _decayed_keys(k, bcum, backward)
            kt = jnp.transpose(kdec).astype(BF16)

    p_blk, dec_col = {}, {}
    for c in range(N_CHUNK):
        scores = []
        for q_in, k_in, _ in dirs:
            k_c = k_in[c * CHUNK_B:(c + 1) * CHUNK_B, :]
            k_bd = jnp.concatenate(
                [jnp.where(lane_head == hd, k_c, jnp.zeros_like(k_c)) for hd in range(N_HEADS_B)],
                axis=0)
            scores.append(_dot_nt(q_in[c * CHUNK_B:(c + 1) * CHUNK_B, :], k_bd))
        p_blk[c] = jnp.where(dirs[0][2], scores[0], scores[1]).astype(BF16)
    kv_blk = _all_chunk_kv(kt, v_par)
    for c in range(N_CHUNK):
        dec_col[c] = _decay_column(dec[c])

    for c in range(N_CHUNK):
        r0 = c * CHUNK_B
        s_fwd = sf_ref[...]
        s_f = s_fwd.astype(BF16)
        s_b = vs_ref[j, TILE:TILE + N_HEADS_B * HEAD_K, c * HEAD_V:(c + 1) * HEAD_V]
        for p in range(N_PAIR):
            a0 = 2 * p * HEAD_V
            rows_a = slice(2 * p * HEAD_K, (2 * p + 1) * HEAD_K)
            rows_b = slice((2 * p + 1) * HEAD_K, (2 * p + 2) * HEAD_K)
            rhs = jnp.concatenate([
                jnp.concatenate([vb[r0:r0 + CHUNK_B, a0:a0 + HEAD_V], zero_blk], axis=1),
                jnp.concatenate([zero_blk, vb[r0:r0 + CHUNK_B, a0 + HEAD_V:a0 + 2 * HEAD_V]], axis=1),
                jnp.concatenate([s_f[rows_a, :], zero_blk], axis=1),
                jnp.concatenate([zero_blk, s_f[rows_b, :]], axis=1),
                jnp.concatenate([s_b[rows_a, :], zero_blk], axis=1),
                jnp.concatenate([zero_blk, s_b[rows_b, :]], axis=1),
            ], axis=0)
            lhs = jnp.concatenate(
                [p_blk[c][:, p * LANES:(p + 1) * LANES]]
                + [q_in[r0:r0 + CHUNK_B, p * LANES:(p + 1) * LANES] for q_in, _, _ in dirs],
                axis=1)
            o_ref[j, r0:r0 + CHUNK_B, a0:a0 + 2 * HEAD_V] = _dot(lhs, rhs)
        sf_ref[...] = dec_col[c] * s_fwd + kv_blk[c]

    emit(1)
    for hd in range(N_HEADS_B):
        c0 = hd * HEAD_V
        oh = o_ref[j, :, c0:c0 + HEAD_V]
        out_b = oh * _rms_scale(oh) * gnb_ref[...] * _silu(
            p_ref[t0:t0 + TILE, M_ZB + c0:M_ZB + c0 + HEAD_V])
        mix_ref[j, :, D_A + c0:D_A + c0 + HEAD_V] = out_b.astype(BF16)

    mixed = jnp.concatenate(
        [_dot(mix_ref[j], wout_ref[:, c0:c0 + PIECE_COLS]) for c0 in range(0, D_MODEL, PIECE_COLS)],
        axis=1)
    emit(1)
    out_ref[0, t0:t0 + TILE, :] = (
        x_ref[t0:t0 + TILE, :] + mixed * _rms_scale(mixed) * npost_ref[...])


def _layer_kernel(steps_per_row, m_steps, xn_ref, klr_ref, vs_ref, npre_ref, win_ref,
                  wsp_ref, bsp_ref, gva_ref, wgk_ref, bgk_ref, ltri_ref, gnb_ref,
                  wout_ref, npost_ref, out_ref, p0_ref, p1_ref, x0_ref, x1_ref, sf_ref, o_ref, mix_ref):
    i = pl.program_id(0)

    @pl.when(jnp.logical_or(i == 0, (i - 1) % steps_per_row == 0))
    def _():
        sf_ref[...] = jnp.zeros_like(sf_ref)

    def step(write_refs, read_refs):
        emit = _emitter(_projection_pieces(xn_ref, npre_ref, win_ref, *write_refs)
                        if write_refs is not None else [])
        for j in range(MAIN_SUB) if read_refs is not None else ():
            _mix(j, *read_refs, klr_ref, vs_ref, wsp_ref, bsp_ref, gva_ref, wgk_ref, bgk_ref,
                 ltri_ref, gnb_ref, wout_ref, npost_ref, out_ref, sf_ref, o_ref, mix_ref,
                 emit)
        emit(MAIN_SUB * P_COLS // PIECE_COLS)

    _ping_pong_steps(i, m_steps, step, (p0_ref, x0_ref), (p1_ref, x1_ref))


def _const_spec(shape):
    return pl.BlockSpec(shape, lambda *_: (0,) * len(shape))


def _hybrid_layer(x, wts):
    bsz, seq, _ = x.shape
    n_tiles = seq // TILE
    n_total = bsz * n_tiles

    params = pltpu.CompilerParams(dimension_semantics=("arbitrary",), vmem_limit_bytes=VMEM_LIMIT)

    n_steps = n_total // PRE_SUB

    def recurred_block(i):
        return (jnp.minimum(n_steps - i, n_steps - 1), 0, 0)

    klr_tiles, vs_tiles = pl.pallas_call(
        functools.partial(_bwd_state_kernel, n_tiles // PRE_SUB, n_steps),
        grid=(n_steps + 1,),
        in_specs=[
            pl.BlockSpec((1, PRE_SUB * TILE, D_MODEL),
                         lambda i: (jnp.maximum(n_steps - 1 - i, 0), 0, 0)),
            _const_spec((1, D_MODEL)),
            _const_spec((D_MODEL, PRE_COLS + PIECE_COLS)),
            _const_spec((LANES, D_K)),
            _const_spec((1, D_K)),
            _const_spec((TILE, TILE)),
        ],
        out_specs=[
            pl.BlockSpec((PRE_SUB, TILE, KLB_COLS), recurred_block),
            pl.BlockSpec((PRE_SUB, TILE + N_HEADS_B * HEAD_K, D_B), recurred_block),
        ],
        out_shape=[
            jax.ShapeDtypeStruct((n_total, TILE, KLB_COLS), F32),
            jax.ShapeDtypeStruct((n_total, TILE + N_HEADS_B * HEAD_K, D_B), BF16),
        ],
        scratch_shapes=[
            pltpu.VMEM((PRE_SUB * TILE, PRE_COLS), F32),
            pltpu.VMEM((PRE_SUB * TILE, PRE_COLS), F32),
            pltpu.VMEM((N_HEADS_B * HEAD_K, HEAD_V), F32),
        ],
        compiler_params=params,
    )(x.reshape(n_steps, PRE_SUB * TILE, D_MODEL), wts["npre"], wts["wkv"], wts["wgk_b"],
      wts["bgk_b"], wts["utri"])

    m_steps = n_total // MAIN_SUB

    def finished_block(i):
        return (jnp.maximum(i - 1, 0), 0, 0)

    y = pl.pallas_call(
        functools.partial(_layer_kernel, n_tiles // MAIN_SUB, m_steps),
        grid=(m_steps + 1,),
        in_specs=[
            pl.BlockSpec((1, MAIN_SUB * TILE, D_MODEL),
                         lambda i: (jnp.minimum(i, m_steps - 1), 0, 0)),
            pl.BlockSpec((MAIN_SUB, TILE, KLB_COLS), finished_block),
            pl.BlockSpec((MAIN_SUB, TILE + N_HEADS_B * HEAD_K, D_B), finished_block),
            _const_spec((1, D_MODEL)),
            _const_spec((D_MODEL, P_COLS)),
            _const_spec((N_HEADS_A // 2, CHUNK_A, 2 * CHUNK_A)),
            _const_spec((N_HEADS_A, CHUNK_A, HEAD_A)),
            _const_spec((1, D_A)),
            _const_spec((LANES, D_K)),
            _const_spec((1, D_K)),
            _const_spec((TILE, TILE)),
            _const_spec((1, HEAD_V)),
            _const_spec((D_A + D_B, D_MODEL + PIECE_COLS)),
            _const_spec((1, D_MODEL)),
        ],
        out_specs=pl.BlockSpec((1, MAIN_SUB * TILE, D_MODEL), finished_block),
        out_shape=jax.ShapeDtypeStruct((m_steps, MAIN_SUB * TILE, D_MODEL), x.dtype),
        scratch_shapes=[
            pltpu.VMEM((MAIN_SUB * TILE, P_COLS), F32),
            pltpu.VMEM((MAIN_SUB * TILE, P_COLS), F32),
            pltpu.VMEM((MAIN_SUB * TILE, D_MODEL), F32),
            pltpu.VMEM((MAIN_SUB * TILE, D_MODEL), F32),
            pltpu.VMEM((N_HEADS_B * HEAD_K, HEAD_V), F32),
            pltpu.VMEM((MAIN_SUB, TILE, D_B), F32),
            pltpu.VMEM((MAIN_SUB, TILE, D_A + D_B), BF16),
        ],
        compiler_params=params,
    )(x.reshape(m_steps, MAIN_SUB * TILE, D_MODEL), klr_tiles, vs_tiles, wts["npre"], wts["win"],
      wts["wsp"], wts["bsp"], wts["gva"], wts["wgk_f"], wts["bgk_f"], wts["ltri"], wts["gnb"],
      wts["wout"], wts["npost"])
    return y.reshape(x.shape)


def _prepare_weights(norm_pre, w_in, w_sp, b_sp, g_v_a, w_gk_fwd, b_gk_fwd, w_gk_bwd, b_gk_bwd,
                     g_norm_b, w_out, norm_post):
    w_in_bf = w_in.astype(BF16)
    wlr = jnp.zeros((D_MODEL, PIECE_COLS), BF16).at[:, 0:2 * GATE_RANK].set(w_in_bf[:, OFF_LRF:D_IN])
    wgk = jnp.zeros((LANES, 2 * D_K), BF16)
    wgk = wgk.at[0:GATE_RANK, 0:D_K].set(w_gk_fwd.astype(BF16))
    wgk = wgk.at[GATE_RANK:2 * GATE_RANK, D_K:2 * D_K].set(w_gk_bwd.astype(BF16))
    pad_cols = jnp.zeros((D_MODEL, PIECE_COLS), BF16)
    pos = jnp.arange(TILE)
    same_chunk = (pos[:, None] // CHUNK_B) == (pos[None, :] // CHUNK_B)
    ltri = (same_chunk & (pos[None, :] <= pos[:, None])).astype(BF16)
    utri = (same_chunk & (pos[None, :] >= pos[:, None])).astype(BF16)
    return {
        "npre": norm_pre.reshape(1, D_MODEL),
        "win": jnp.concatenate([w_in_bf[:, OFF_U:OFF_K], w_in_bf[:, OFF_ZB:OFF_LRF]], axis=1),
        "wkv": jnp.concatenate([w_in_bf[:, OFF_K:OFF_ZB], wlr, pad_cols], axis=1),
        "wsp": jnp.concatenate([w_sp[0::2], w_sp[1::2]], axis=-1).astype(BF16),
        "bsp": jnp.broadcast_to(b_sp[:, :, None], (N_HEADS_A, CHUNK_A, HEAD_A)),
        "gva": g_v_a.reshape(1, D_A),
        "wgk_f": wgk[:, 0:D_K],
        "wgk_b": wgk[:, D_K:2 * D_K],
        "bgk_f": b_gk_fwd.reshape(1, D_K),
        "bgk_b": b_gk_bwd.reshape(1, D_K),
        "ltri": ltri,
        "utri": utri,
        "gnb": g_norm_b.reshape(1, HEAD_V),
        "wout": jnp.concatenate([w_out.astype(BF16), pad_cols], axis=1),
        "npost": norm_post.reshape(1, D_MODEL),
    }


def kernel(x_prompt, x_sample, norm_pre, w_in, w_sp, b_sp, g_v_a, w_gk_fwd, b_gk_fwd,
           w_gk_bwd, b_gk_bwd, g_norm_b, w_out, norm_post):
    y_prompt, y_sample = x_prompt, x_sample
    for l in range(norm_pre.shape[0]):
        wts = _prepare_weights(norm_pre[l], w_in[l], w_sp[l], b_sp[l], g_v_a[l], w_gk_fwd[l],
                               b_gk_fwd[l], w_gk_bwd[l], b_gk_bwd[l], g_norm_b[l], w_out[l],
                               norm_post[l])
        y_prompt = _hybrid_layer(y_prompt, wts)
        y_sample = _hybrid_layer(y_sample, wts)
    return (y_prompt, y_sample)
```

```python
import functools

import jax
import jax.numpy as jnp
from jax import lax
from jax.experimental import pallas as pl
from jax.experimental.pallas import tpu as pltpu

F32 = jnp.float32
BF16 = jnp.bfloat16

D_MODEL = 1024
D_A = 512
N_HEADS_A = 4
HEAD_A = 128
CHUNK_A = 128
D_B = 512
N_HEADS_B = 4
HEAD_V = 128
D_K = 256
HEAD_K = 64
GATE_RANK = 16
GATE_NORMALIZER = 16.0
CHUNK_B = 64
EPS = 1e-6

OFF_U, OFF_V, OFF_ZA, OFF_Q, OFF_K, OFF_VB, OFF_ZB, OFF_LRF, OFF_LRB, D_IN = (
    0, 512, 1024, 1536, 1792, 2048, 2560, 3072, 3088, 3104)

LANES = 128
TILE = 256
N_CHUNK = TILE // CHUNK_B
N_PAIR = N_HEADS_B // 2
PIECE_COLS = 256
PRE_COLS = D_K + D_B + PIECE_COLS
PRE_SUB = 4
MAIN_SUB = 2
KLB_COLS = D_K + LANES + D_K
M_U, M_V, M_ZA, M_Q, M_ZB, P_COLS = 0, 512, 1024, 1536, 1792, 2304
VMEM_LIMIT = 56 * 1024 * 1024


def _dot(a, b):
    return jnp.dot(a, b, preferred_element_type=F32)


def _dot_nt(a, b):
    return lax.dot_general(a, b, (((1,), (1,)), ((), ())), preferred_element_type=F32)


def _gelu(x):
    return 0.5 * x * (1.0 + jnp.tanh(0.7978845608028654 * (x + 0.044715 * (x * x * x))))


def _silu(x):
    return x * (0.5 * jnp.tanh(0.5 * x) + 0.5)


def _log_sigmoid(x):
    return jnp.minimum(x, 0.0) - jnp.log1p(jnp.exp(-jnp.abs(x)))


def _rms_scale(x):
    return lax.rsqrt(jnp.mean(x * x, axis=-1, keepdims=True) + EPS)


def _chunk_cumsum(tri_ref, g):
    hi = g.astype(BF16)
    lo = (g - hi.astype(F32)).astype(BF16)
    n = g.shape[1]
    both = _dot(tri_ref[...], jnp.concatenate([hi, lo], axis=1))
    return both[:, 0:n] + both[:, n:2 * n]


def _decayed_keys(k, bcum, backward):
    kd, dec = [], []
    for c in range(N_CHUNK):
        r0 = c * CHUNK_B
        last = r0 if backward else r0 + CHUNK_B - 1
        b_last = bcum[last:last + 1, :]
        kd.append(k[r0:r0 + CHUNK_B, :] * jnp.exp(b_last - bcum[r0:r0 + CHUNK_B, :]))
        dec.append(jnp.exp(b_last))
    return jnp.concatenate(kd, axis=0), dec


def _chunk_kv(kt, v_par, c):
    cp, par = divmod(c, 2)
    parts = []
    for p in range(N_PAIR):
        full = _dot(kt[p * LANES:(p + 1) * LANES, cp * LANES:(cp + 1) * LANES],
                    v_par[cp][par][:, p * 2 * HEAD_V:(p + 1) * 2 * HEAD_V])
        parts.append(full[0:HEAD_K, 0:HEAD_V])
        parts.append(full[HEAD_K:2 * HEAD_K, HEAD_V:2 * HEAD_V])
    return jnp.concatenate(parts, axis=0)


def _decay_column(dec_row):
    return jnp.transpose(jnp.broadcast_to(dec_row, (LANES, D_K)))


def _parity_masked_values(vb):
    out = []
    row = lax.broadcasted_iota(jnp.int32, (2 * CHUNK_B, D_B), 0)
    for cp in range(N_CHUNK // 2):
        vp = vb[cp * 2 * CHUNK_B:(cp + 1) * 2 * CHUNK_B, :]
        zero = jnp.zeros_like(vp)
        out.append((jnp.where(row < CHUNK_B, vp, zero), jnp.where(row >= CHUNK_B, vp, zero)))
    return out


def _projection_pieces(x_ref, npre_ref, w_ref, p_ref, x_keep_ref=None):
    normed = {}

    def piece(r0, c0):
        def run():
            if r0 not in normed:
                x = x_ref[0, r0:r0 + TILE, :]
                if x_keep_ref is not None:
                    x_keep_ref[r0:r0 + TILE, :] = x
                normed[r0] = (x * _rms_scale(x) * npre_ref[...]).astype(BF16)
            p_ref[r0:r0 + TILE, c0:c0 + PIECE_COLS] = _dot(normed[r0], w_ref[:, c0:c0 + PIECE_COLS])
        return run

    return [piece(r0, c0) for r0 in range(0, p_ref.shape[0], TILE)
            for c0 in range(0, p_ref.shape[1], PIECE_COLS)]


def _emitter(pieces):
    def emit(n):
        for _ in range(min(n, len(pieces))):
            pieces.pop(0)()
    return emit


def _ping_pong_steps(i, n, step, even_refs, odd_refs):
    first, last = i == 0, i == n
    middle = jnp.logical_not(jnp.logical_or(first, last))

    @pl.when(first)
    def _():
        step(even_refs, None)

    @pl.when(jnp.logical_and(middle, i % 2 == 0))
    def _():
        step(even_refs, odd_refs)

    @pl.when(jnp.logical_and(middle, i % 2 == 1))
    def _():
        step(odd_refs, even_refs)

    @pl.when(last)
    def _():
        step(None, even_refs if n % 2 == 1 else odd_refs)


def _bwd_state_kernel(steps_per_row, n_steps, xn_ref, npre_ref, wkv_ref, wgk_ref, bgk_ref, utri_ref,
                      klr_ref, vs_ref, p0_ref, p1_ref, s_ref):
    i = pl.program_id(0)

    @pl.when(jnp.logical_or(i == 0, (n_steps - i) % steps_per_row == steps_per_row - 1))
    def _():
        s_ref[...] = jnp.zeros_like(s_ref)

    def step(write_refs, read_refs):
        emit = _emitter(_projection_pieces(xn_ref, npre_ref, wkv_ref, *write_refs)
                        if write_refs is not None else [])
        s = s_ref[...]
        for j in reversed(range(PRE_SUB) if read_refs is not None else ()):
            p_ref, r0 = read_refs[0], j * TILE
            emit(1)
            k = p_ref[r0:r0 + TILE, 0:D_K]
            vb = p_ref[r0:r0 + TILE, D_K:D_K + D_B].astype(BF16)
            plr = p_ref[r0:r0 + TILE, D_K + D_B:D_K + D_B + LANES]
            klr_ref[j, :, 0:D_K] = k
            klr_ref[j, :, D_K:D_K + LANES] = plr
            vs_ref[j, 0:TILE, :] = vb
            g = _log_sigmoid(_dot(plr.astype(BF16), wgk_ref[...]) + bgk_ref[...]) * (
                1.0 / GATE_NORMALIZER)
            emit(1)
            bcum = _chunk_cumsum(utri_ref, g)
            klr_ref[j, :, D_K + LANES:KLB_COLS] = bcum
            kdec, dec = _decayed_keys(k, bcum, backward=True)
            kt = jnp.transpose(kdec).astype(BF16)
            emit(1)
            v_par = _parity_masked_values(vb)
            kv = [_chunk_kv(kt, v_par, c) for c in range(N_CHUNK)]
            dec_col = [_decay_column(dec[c]) for c in range(N_CHUNK)]
            emit(1)
            for c in reversed(range(N_CHUNK)):
                vs_ref[j, TILE:TILE + N_HEADS_B * HEAD_K, c * HEAD_V:(c + 1) * HEAD_V] = (
                    s.astype(BF16))
                s = dec_col[c] * s + kv[c]
        emit(PRE_SUB * PRE_COLS // PIECE_COLS)
        s_ref[...] = s

    _ping_pong_steps(i, n_steps, step, (p0_ref,), (p1_ref,))


def _mix(j, p_ref, x_ref, klr_ref, vs_ref, wsp_ref, bsp_ref, gva_ref, wgk_ref, bgk_ref,
         ltri_ref, gnb_ref, wout_ref, npost_ref, out_ref, sf_ref, o_ref, mix_ref, emit):
    t0 = j * TILE

    emit(2)
    vg = _gelu(p_ref[t0:t0 + TILE, M_V:M_ZA])
    vc = vg - jnp.mean(vg, axis=-1, keepdims=True)
    vn = (vc * _rms_scale(vc) * gva_ref[...]).astype(BF16)
    zero_a = jnp.zeros((CHUNK_A, HEAD_A), BF16)
    for c in range(TILE // CHUNK_A):
        r0 = c * CHUNK_A
        for pp in range(N_HEADS_A // 2):
            ca, cb = 2 * pp * HEAD_A, (2 * pp + 1) * HEAD_A
            sv_pair = _dot(wsp_ref[pp], jnp.concatenate([
                jnp.concatenate([vn[r0:r0 + CHUNK_A, ca:ca + HEAD_A], zero_a], axis=1),
                jnp.concatenate([zero_a, vn[r0:r0 + CHUNK_A, cb:cb + HEAD_A]], axis=1)], axis=0))
            for hd, c0 in ((2 * pp, ca), (2 * pp + 1, cb)):
                sv = sv_pair[:, c0 - ca:c0 - ca + HEAD_A] + bsp_ref[hd]
                out_a = _gelu(p_ref[t0 + r0:t0 + r0 + CHUNK_A, M_U + c0:M_U + c0 + HEAD_A]) * sv
                out_a = out_a * _silu(
                    p_ref[t0 + r0:t0 + r0 + CHUNK_A, M_ZA + c0:M_ZA + c0 + HEAD_A])
                mix_ref[j, r0:r0 + CHUNK_A, c0:c0 + HEAD_A] = out_a.astype(BF16)
            emit(1 if pp == 0 else 0)

    q = p_ref[t0:t0 + TILE, M_Q:M_ZB] * (HEAD_K ** -0.5)
    k = klr_ref[j, :, 0:D_K]
    vb = vs_ref[j, 0:TILE, :]
    plr = klr_ref[j, :, D_K:D_K + LANES]
    emit(1)
    g_fwd = _log_sigmoid(_dot(plr.astype(BF16), wgk_ref[...]) + bgk_ref[...]) * (
        1.0 / GATE_NORMALIZER)
    v_par = _parity_masked_values(vb)

    lane_head = lax.broadcasted_iota(jnp.int32, (CHUNK_B, D_K), 1) // HEAD_K
    lane_pos = lax.broadcasted_iota(jnp.int32, (CHUNK_B, D_K), 1) % CHUNK_B
    row_pos = lax.broadcasted_iota(jnp.int32, (CHUNK_B, D_K), 0)
    zero_blk = jnp.zeros((CHUNK_B, HEAD_V), BF16)

    dirs = []
    for d in range(2):
        backward = d == 1
        emit(1)
        bcum = klr_ref[j, :, D_K + LANES:KLB_COLS] if backward else _chunk_cumsum(ltri_ref, g_fwd)
        q_in = (q * jnp.exp(bcum)).astype(BF16)
        k_in = (k * jnp.exp(-bcum)).astype(BF16)
        keep = (row_pos < lane_pos) if backward else (row_pos >= lane_pos)
        dirs.append((q_in, k_in, keep))
        if not backward:
            kdec, dec = _decayed_keys(k, bcum, backward)
            kt = jnp.transpose(kdec).astype(BF16)

    p_blk, kv_blk, dec_col = {}, {}, {}
    for c in range(N_CHUNK):
        scores = []
        for q_in, k_in, _ in dirs:
            k_c = k_in[c * CHUNK_B:(c + 1) * CHUNK_B, :]
            k_bd = jnp.concatenate(
                [jnp.where(lane_head == hd, k_c, jnp.zeros_like(k_c)) for hd in range(N_HEADS_B)],
                axis=0)
            scores.append(_dot_nt(q_in[c * CHUNK_B:(c + 1) * CHUNK_B, :], k_bd))
        p_blk[c] = jnp.where(dirs[0][2], scores[0], scores[1]).astype(BF16)
    for c in range(N_CHUNK):
        kv_blk[c] = _chunk_kv(kt, v_par, c)
        dec_col[c] = _decay_column(dec[c])

    for c in range(N_CHUNK):
        r0 = c * CHUNK_B
        s_fwd = sf_ref[...]
        s_f = s_fwd.astype(BF16)
        s_b = vs_ref[j, TILE:TILE + N_HEADS_B * HEAD_K, c * HEAD_V:(c + 1) * HEAD_V]
        for p in range(N_PAIR):
            a0 = 2 * p * HEAD_V
            rows_a = slice(2 * p * HEAD_K, (2 * p + 1) * HEAD_K)
            rows_b = slice((2 * p + 1) * HEAD_K, (2 * p + 2) * HEAD_K)
            rhs = jnp.concatenate([
                jnp.concatenate([vb[r0:r0 + CHUNK_B, a0:a0 + HEAD_V], zero_blk], axis=1),
                jnp.concatenate([zero_blk, vb[r0:r0 + CHUNK_B, a0 + HEAD_V:a0 + 2 * HEAD_V]], axis=1),
                jnp.concatenate([s_f[rows_a, :], zero_blk], axis=1),
                jnp.concatenate([zero_blk, s_f[rows_b, :]], axis=1),
                jnp.concatenate([s_b[rows_a, :], zero_blk], axis=1),
                jnp.concatenate([zero_blk, s_b[rows_b, :]], axis=1),
            ], axis=0)
            lhs = jnp.concatenate(
                [p_blk[c][:, p * LANES:(p + 1) * LANES]]
                + [q_in[r0:r0 + CHUNK_B, p * LANES:(p + 1) * LANES] for q_in, _, _ in dirs],
                axis=1)
            o_ref[j, r0:r0 + CHUNK_B, a0:a0 + 2 * HEAD_V] = _dot(lhs, rhs)
        sf_ref[...] = dec_col[c] * s_fwd + kv_blk[c]

    emit(1)
    for hd in range(N_HEADS_B):
        c0 = hd * HEAD_V
        oh = o_ref[j, :, c0:c0 + HEAD_V]
        out_b = oh * _rms_scale(oh) * gnb_ref[...] * _silu(
            p_ref[t0:t0 + TILE, M_ZB + c0:M_ZB + c0 + HEAD_V])
        mix_ref[j, :, D_A + c0:D_A + c0 + HEAD_V] = out_b.astype(BF16)

    mixed = jnp.concatenate(
        [_dot(mix_ref[j], wout_ref[:, c0:c0 + PIECE_COLS]) for c0 in range(0, D_MODEL, PIECE_COLS)],
        axis=1)
    emit(1)
    out_ref[0, t0:t0 + TILE, :] = (
        x_ref[t0:t0 + TILE, :] + mixed * _rms_scale(mixed) * npost_ref[...])


def _layer_kernel(steps_per_row, m_steps, xn_ref, klr_ref, vs_ref, npre_ref, win_ref,
                  wsp_ref, bsp_ref, gva_ref, wgk_ref, bgk_ref, ltri_ref, gnb_ref,
                  wout_ref, npost_ref, out_ref, p0_ref, p1_ref, x0_ref, x1_ref, sf_ref, o_ref, mix_ref):
    i = pl.program_id(0)

    @pl.when(jnp.logical_or(i == 0, (i - 1) % steps_per_row == 0))
    def _():
        sf_ref[...] = jnp.zeros_like(sf_ref)

    def step(write_refs, read_refs):
        emit = _emitter(_projection_pieces(xn_ref, npre_ref, win_ref, *write_refs)
                        if write_refs is not None else [])
        for j in range(MAIN_SUB) if read_refs is not None else ():
            _mix(j, *read_refs, klr_ref, vs_ref, wsp_ref, bsp_ref, gva_ref, wgk_ref, bgk_ref,
                 ltri_ref, gnb_ref, wout_ref, npost_ref, out_ref, sf_ref, o_ref, mix_ref,
                 emit)
        emit(MAIN_SUB * P_COLS // PIECE_COLS)

    _ping_pong_steps(i, m_steps, step, (p0_ref, x0_ref), (p1_ref, x1_ref))


def _const_spec(shape):
    return pl.BlockSpec(shape, lambda *_: (0,) * len(shape))


def _hybrid_layer(x, wts):
    bsz, seq, _ = x.shape
    n_tiles = seq // TILE
    n_total = bsz * n_tiles

    params = pltpu.CompilerParams(dimension_semantics=("arbitrary",), vmem_limit_bytes=VMEM_LIMIT)

    n_steps = n_total // PRE_SUB

    def recurred_block(i):
        return (jnp.minimum(n_steps - i, n_steps - 1), 0, 0)

    klr_tiles, vs_tiles = pl.pallas_call(
        functools.partial(_bwd_state_kernel, n_tiles // PRE_SUB, n_steps),
        grid=(n_steps + 1,),
        in_specs=[
            pl.BlockSpec((1, PRE_SUB * TILE, D_MODEL),
                         lambda i: (jnp.maximum(n_steps - 1 - i, 0), 0, 0)),
            _const_spec((1, D_MODEL)),
            _const_spec((D_MODEL, PRE_COLS + PIECE_COLS)),
            _const_spec((LANES, D_K)),
            _const_spec((1, D_K)),
            _const_spec((TILE, TILE)),
        ],
        out_specs=[
            pl.BlockSpec((PRE_SUB, TILE, KLB_COLS), recurred_block),
            pl.BlockSpec((PRE_SUB, TILE + N_HEADS_B * HEAD_K, D_B), recurred_block),
        ],
        out_shape=[
            jax.ShapeDtypeStruct((n_total, TILE, KLB_COLS), F32),
            jax.ShapeDtypeStruct((n_total, TILE + N_HEADS_B * HEAD_K, D_B), BF16),
        ],
        scratch_shapes=[
            pltpu.VMEM((PRE_SUB * TILE, PRE_COLS), F32),
            pltpu.VMEM((PRE_SUB * TILE, PRE_COLS), F32),
            pltpu.VMEM((N_HEADS_B * HEAD_K, HEAD_V), F32),
        ],
        compiler_params=params,
    )(x.reshape(n_steps, PRE_SUB * TILE, D_MODEL), wts["npre"], wts["wkv"], wts["wgk_b"],
      wts["bgk_b"], wts["utri"])

    m_steps = n_total // MAIN_SUB

    def finished_block(i):
        return (jnp.maximum(i - 1, 0), 0, 0)

    y = pl.pallas_call(
        functools.partial(_layer_kernel, n_tiles // MAIN_SUB, m_steps),
        grid=(m_steps + 1,),
        in_specs=[
            pl.BlockSpec((1, MAIN_SUB * TILE, D_MODEL),
                         lambda i: (jnp.minimum(i, m_steps - 1), 0, 0)),
            pl.BlockSpec((MAIN_SUB, TILE, KLB_COLS), finished_block),
            pl.BlockSpec((MAIN_SUB, TILE + N_HEADS_B * HEAD_K, D_B), finished_block),
            _const_spec((1, D_MODEL)),
            _const_spec((D_MODEL, P_COLS)),
            _const_spec((N_HEADS_A // 2, CHUNK_A, 2 * CHUNK_A)),
            _const_spec((N_HEADS_A, CHUNK_A, HEAD_A)),
            _const_spec((1, D_A)),
            _const_spec((LANES, D_K)),
            _const_spec((1, D_K)),
            _const_spec((TILE, TILE)),
            _const_spec((1, HEAD_V)),
            _const_spec((D_A + D_B, D_MODEL + PIECE_COLS)),
            _const_spec((1, D_MODEL)),
        ],
        out_specs=pl.BlockSpec((1, MAIN_SUB * TILE, D_MODEL), finished_block),
        out_shape=jax.ShapeDtypeStruct((m_steps, MAIN_SUB * TILE, D_MODEL), x.dtype),
        scratch_shapes=[
            pltpu.VMEM((MAIN_SUB * TILE, P_COLS), F32),
            pltpu.VMEM((MAIN_SUB * TILE, P_COLS), F32),
            pltpu.VMEM((MAIN_SUB * TILE, D_MODEL), F32),
            pltpu.VMEM((MAIN_SUB * TILE, D_MODEL), F32),
            pltpu.VMEM((N_HEADS_B * HEAD_K, HEAD_V), F32),
            pltpu.VMEM((MAIN_SUB, TILE, D_B), F32),
            pltpu.VMEM((MAIN_SUB, TILE, D_A + D_B), BF16),
        ],
        compiler_params=params,
    )(x.reshape(m_steps, MAIN_SUB * TILE, D_MODEL), klr_tiles, vs_tiles, wts["npre"], wts["win"],
      wts["wsp"], wts["bsp"], wts["gva"], wts["wgk_f"], wts["bgk_f"], wts["ltri"], wts["gnb"],
      wts["wout"], wts["npost"])
    return y.reshape(x.shape)


def _prepare_weights(norm_pre, w_in, w_sp, b_sp, g_v_a, w_gk_fwd, b_gk_fwd, w_gk_bwd, b_gk_bwd,
                     g_norm_b, w_out, norm_post):
    w_in_bf = w_in.astype(BF16)
    wlr = jnp.zeros((D_MODEL, PIECE_COLS), BF16).at[:, 0:2 * GATE_RANK].set(w_in_bf[:, OFF_LRF:D_IN])
    wgk = jnp.zeros((LANES, 2 * D_K), BF16)
    wgk = wgk.at[0:GATE_RANK, 0:D_K].set(w_gk_fwd.astype(BF16))
    wgk = wgk.at[GATE_RANK:2 * GATE_RANK, D_K:2 * D_K].set(w_gk_bwd.astype(BF16))
    pad_cols = jnp.zeros((D_MODEL, PIECE_COLS), BF16)
    pos = jnp.arange(TILE)
    same_chunk = (pos[:, None] // CHUNK_B) == (pos[None, :] // CHUNK_B)
    ltri = (same_chunk & (pos[None, :] <= pos[:, None])).astype(BF16)
    utri = (same_chunk & (pos[None, :] >= pos[:, None])).astype(BF16)
    return {
        "npre": norm_pre.reshape(1, D_MODEL),
        "win": jnp.concatenate([w_in_bf[:, OFF_U:OFF_K], w_in_bf[:, OFF_ZB:OFF_LRF]], axis=1),
        "wkv": jnp.concatenate([w_in_bf[:, OFF_K:OFF_ZB], wlr, pad_cols], axis=1),
        "wsp": jnp.concatenate([w_sp[0::2], w_sp[1::2]], axis=-1).astype(BF16),
        "bsp": jnp.broadcast_to(b_sp[:, :, None], (N_HEADS_A, CHUNK_A, HEAD_A)),
        "gva": g_v_a.reshape(1, D_A),
        "wgk_f": wgk[:, 0:D_K],
        "wgk_b": wgk[:, D_K:2 * D_K],
        "bgk_f": b_gk_fwd.reshape(1, D_K),
        "bgk_b": b_gk_bwd.reshape(1, D_K),
        "ltri": ltri,
        "utri": utri,
        "gnb": g_norm_b.reshape(1, HEAD_V),
        "wout": jnp.concatenate([w_out.astype(BF16), pad_cols], axis=1),
        "npost": norm_post.reshape(1, D_MODEL),
    }


def kernel(x_prompt, x_sample, norm_pre, w_in, w_sp, b_sp, g_v_a, w_gk_fwd, b_gk_fwd,
           w_gk_bwd, b_gk_bwd, g_norm_b, w_out, norm_post):
    y_prompt, y_sample = x_prompt, x_sample
    for l in range(norm_pre.shape[0]):
        wts = _prepare_weights(norm_pre[l], w_in[l], w_sp[l], b_sp[l], g_v_a[l], w_gk_fwd[l],
                               b_gk_fwd[l], w_gk_bwd[l], b_gk_bwd[l], g_norm_b[l], w_out[l],
                               norm_post[l])
        y_prompt = _hybrid_layer(y_prompt, wts)
        y_sample = _hybrid_layer(y_sample, wts)
    return (y_prompt, y_sample)
```

```python
import functools

import jax
import jax.numpy as jnp
from jax import lax
from jax.experimental import pallas as pl
from jax.experimental.pallas import tpu as pltpu

F32 = jnp.float32
BF16 = jnp.bfloat16

D_MODEL = 1024
D_A = 512
N_HEADS_A = 4
HEAD_A = 128
CHUNK_A = 128
D_B = 512
N_HEADS_B = 4
HEAD_V = 128
D_K = 256
HEAD_K = 64
GATE_RANK = 16
GATE_NORMALIZER = 16.0
CHUNK_B = 64
EPS = 1e-6

OFF_U, OFF_V, OFF_ZA, OFF_Q, OFF_K, OFF_VB, OFF_ZB, OFF_LRF, OFF_LRB, D_IN = (
    0, 512, 1024, 1536, 1792, 2048, 2560, 3072, 3088, 3104)

LANES = 128
TILE = 256
N_CHUNK = TILE // CHUNK_B
N_PAIR = N_HEADS_B // 2
PIECE_COLS = 256
PRE_COLS = D_K + D_B + PIECE_COLS
PRE_SUB = 4
MAIN_SUB = 2
KLB_COLS = D_K + LANES + D_K
M_U, M_V, M_ZA, M_Q, M_ZB, P_COLS = 0, 512, 1024, 1536, 1792, 2304
VMEM_LIMIT = 56 * 1024 * 1024


def _dot(a, b):
    return jnp.dot(a, b, preferred_element_type=F32)


def _dot_nt(a, b):
    return lax.dot_general(a, b, (((1,), (1,)), ((), ())), preferred_element_type=F32)


def _gelu(x):
    return 0.5 * x * (1.0 + jnp.tanh(0.7978845608028654 * (x + 0.044715 * (x * x * x))))


def _silu(x):
    return x * (0.5 * jnp.tanh(0.5 * x) + 0.5)


def _log_sigmoid(x):
    return jnp.minimum(x, 0.0) - jnp.log1p(jnp.exp(-jnp.abs(x)))


def _rms_scale(x):
    return lax.rsqrt(jnp.mean(x * x, axis=-1, keepdims=True) + EPS)


def _chunk_cumsum(tri_ref, g):
    hi = g.astype(BF16)
    lo = (g - hi.astype(F32)).astype(BF16)
    n = g.shape[1]
    both = _dot(tri_ref[...], jnp.concatenate([hi, lo], axis=1))
    return both[:, 0:n] + both[:, n:2 * n]


def _decayed_keys(k, bcum, backward):
    kd, dec = [], []
    for c in range(N_CHUNK):
        r0 = c * CHUNK_B
        last = r0 if backward else r0 + CHUNK_B - 1
        b_last = bcum[last:last + 1, :]
        kd.append(k[r0:r0 + CHUNK_B, :] * jnp.exp(b_last - bcum[r0:r0 + CHUNK_B, :]))
        dec.append(jnp.exp(b_last))
    return jnp.concatenate(kd, axis=0), dec


def _chunk_kv(kt, vb, c):
    cp, par = divmod(c, 2)
    own = lax.broadcasted_iota(jnp.int32, (LANES, LANES), 1) // CHUNK_B == par
    parts = []
    for p in range(N_PAIR):
        kt_c = kt[p * LANES:(p + 1) * LANES, cp * LANES:(cp + 1) * LANES]
        full = _dot(jnp.where(own, kt_c, jnp.zeros_like(kt_c)),
                    vb[cp * LANES:(cp + 1) * LANES, p * 2 * HEAD_V:(p + 1) * 2 * HEAD_V])
        parts.append(full[0:HEAD_K, 0:HEAD_V])
        parts.append(full[HEAD_K:2 * HEAD_K, HEAD_V:2 * HEAD_V])
    return jnp.concatenate(parts, axis=0)


def _decay_column(dec_row):
    return jnp.transpose(jnp.broadcast_to(dec_row, (LANES, D_K)))


def _projection_pieces(x_ref, npre_ref, w_ref, p_ref, x_keep_ref=None):
    normed = {}

    def piece(r0, c0):
        def run():
            if r0 not in normed:
                x = x_ref[0, r0:r0 + TILE, :]
                if x_keep_ref is not None:
                    x_keep_ref[r0:r0 + TILE, :] = x
                normed[r0] = (x * _rms_scale(x) * npre_ref[...]).astype(BF16)
            p_ref[r0:r0 + TILE, c0:c0 + PIECE_COLS] = _dot(normed[r0], w_ref[:, c0:c0 + PIECE_COLS])
        return run

    return [piece(r0, c0) for r0 in range(0, p_ref.shape[0], TILE)
            for c0 in range(0, p_ref.shape[1], PIECE_COLS)]


def _emitter(pieces):
    def emit(n):
        for _ in range(min(n, len(pieces))):
            pieces.pop(0)()
    return emit


def _ping_pong_steps(i, n, step, even_refs, odd_refs):
    first, last = i == 0, i == n
    middle = jnp.logical_not(jnp.logical_or(first, last))

    @pl.when(first)
    def _():
        step(even_refs, None)

    @pl.when(jnp.logical_and(middle, i % 2 == 0))
    def _():
        step(even_refs, odd_refs)

    @pl.when(jnp.logical_and(middle, i % 2 == 1))
    def _():
        step(odd_refs, even_refs)

    @pl.when(last)
    def _():
        step(None, even_refs if n % 2 == 1 else odd_refs)


def _bwd_state_kernel(steps_per_row, n_steps, xn_ref, npre_ref, wkv_ref, wgk_ref, bgk_ref, utri_ref,
                      klr_ref, vs_ref, p0_ref, p1_ref, s_ref):
    i = pl.program_id(0)

    @pl.when(jnp.logical_or(i == 0, (n_steps - i) % steps_per_row == steps_per_row - 1))
    def _():
        s_ref[...] = jnp.zeros_like(s_ref)

    def step(write_refs, read_refs):
        emit = _emitter(_projection_pieces(xn_ref, npre_ref, wkv_ref, *write_refs)
                        if write_refs is not None else [])
        s = s_ref[...]
        for j in reversed(range(PRE_SUB) if read_refs is not None else ()):
            p_ref, r0 = read_refs[0], j * TILE
            emit(1)
            k = p_ref[r0:r0 + TILE, 0:D_K]
            vb = p_ref[r0:r0 + TILE, D_K:D_K + D_B].astype(BF16)
            plr = p_ref[r0:r0 + TILE, D_K + D_B:D_K + D_B + LANES]
            klr_ref[j, :, 0:D_K] = k
            klr_ref[j, :, D_K:D_K + LANES] = plr
            vs_ref[j, 0:TILE, :] = vb
            g = _log_sigmoid(_dot(plr.astype(BF16), wgk_ref[...]) + bgk_ref[...]) * (
                1.0 / GATE_NORMALIZER)
            emit(1)
            bcum = _chunk_cumsum(utri_ref, g)
            klr_ref[j, :, D_K + LANES:KLB_COLS] = bcum
            kdec, dec = _decayed_keys(k, bcum, backward=True)
            kt = jnp.transpose(kdec).astype(BF16)
            emit(1)
            kv = [_chunk_kv(kt, vb, c) for c in range(N_CHUNK)]
            dec_col = [_decay_column(dec[c]) for c in range(N_CHUNK)]
            emit(1)
            for c in reversed(range(N_CHUNK)):
                vs_ref[j, TILE:TILE + N_HEADS_B * HEAD_K, c * HEAD_V:(c + 1) * HEAD_V] = (
                    s.astype(BF16))
                s = dec_col[c] * s + kv[c]
        emit(PRE_SUB * PRE_COLS // PIECE_COLS)
        s_ref[...] = s

    _ping_pong_steps(i, n_steps, step, (p0_ref,), (p1_ref,))


def _mix(j, p_ref, x_ref, klr_ref, vs_ref, wsp_ref, bsp_ref, gva_ref, wgk_ref, bgk_ref,
         ltri_ref, gnb_ref, wout_ref, npost_ref, out_ref, sf_ref, o_ref, mix_ref, emit):
    t0 = j * TILE

    emit(2)
    vg = _gelu(p_ref[t0:t0 + TILE, M_V:M_ZA])
    vc = vg - jnp.mean(vg, axis=-1, keepdims=True)
    vn = (vc * _rms_scale(vc) * gva_ref[...]).astype(BF16)
    zero_a = jnp.zeros((CHUNK_A, HEAD_A), BF16)
    for c in range(TILE // CHUNK_A):
        r0 = c * CHUNK_A
        for pp in range(N_HEADS_A // 2):
            ca, cb = 2 * pp * HEAD_A, (2 * pp + 1) * HEAD_A
            sv_pair = _dot(wsp_ref[pp], jnp.concatenate([
                jnp.concatenate([vn[r0:r0 + CHUNK_A, ca:ca + HEAD_A], zero_a], axis=1),
                jnp.concatenate([zero_a, vn[r0:r0 + CHUNK_A, cb:cb + HEAD_A]], axis=1)], axis=0))
            for hd, c0 in ((2 * pp, ca), (2 * pp + 1, cb)):
                sv = sv_pair[:, c0 - ca:c0 - ca + HEAD_A] + bsp_ref[hd]
                out_a = _gelu(p_ref[t0 + r0:t0 + r0 + CHUNK_A, M_U + c0:M_U + c0 + HEAD_A]) * sv
                out_a = out_a * _silu(
                    p_ref[t0 + r0:t0 + r0 + CHUNK_A, M_ZA + c0:M_ZA + c0 + HEAD_A])
                mix_ref[j, r0:r0 + CHUNK_A, c0:c0 + HEAD_A] = out_a.astype(BF16)
            emit(1 if pp == 0 else 0)

    q = p_ref[t0:t0 + TILE, M_Q:M_ZB] * (HEAD_K ** -0.5)
    k = klr_ref[j, :, 0:D_K]
    vb = vs_ref[j, 0:TILE, :]
    plr = klr_ref[j, :, D_K:D_K + LANES]
    emit(1)
    g_fwd = _log_sigmoid(_dot(plr.astype(BF16), wgk_ref[...]) + bgk_ref[...]) * (
        1.0 / GATE_NORMALIZER)

    lane_head = lax.broadcasted_iota(jnp.int32, (CHUNK_B, D_K), 1) // HEAD_K
    lane_pos = lax.broadcasted_iota(jnp.int32, (CHUNK_B, D_K), 1) % CHUNK_B
    row_pos = lax.broadcasted_iota(jnp.int32, (CHUNK_B, D_K), 0)
    zero_blk = jnp.zeros((CHUNK_B, HEAD_V), BF16)

    dirs = []
    for d in range(2):
        backward = d == 1
        emit(1)
        bcum = klr_ref[j, :, D_K + LANES:KLB_COLS] if backward else _chunk_cumsum(ltri_ref, g_fwd)
        q_in = (q * jnp.exp(bcum)).astype(BF16)
        k_in = (k * jnp.exp(-bcum)).astype(BF16)
        keep = (row_pos < lane_pos) if backward else (row_pos >= lane_pos)
        dirs.append((q_in, k_in, keep))
        if not backward:
            kdec, dec = _decayed_keys(k, bcum, backward)
            kt = jnp.transpose(kdec).astype(BF16)

    p_blk, kv_blk, dec_col = {}, {}, {}
    for c in range(N_CHUNK):
        scores = []
        for q_in, k_in, _ in dirs:
            k_c = k_in[c * CHUNK_B:(c + 1) * CHUNK_B, :]
            k_bd = jnp.concatenate(
                [jnp.where(lane_head == hd, k_c, jnp.zeros_like(k_c)) for hd in range(N_HEADS_B)],
                axis=0)
            scores.append(_dot_nt(q_in[c * CHUNK_B:(c + 1) * CHUNK_B, :], k_bd))
        p_blk[c] = jnp.where(dirs[0][2], scores[0], scores[1]).astype(BF16)
    for c in range(N_CHUNK):
        kv_blk[c] = _chunk_kv(kt, vb, c)
        dec_col[c] = _decay_column(dec[c])

    for c in range(N_CHUNK):
        r0 = c * CHUNK_B
        s_fwd = sf_ref[...]
        s_f = s_fwd.astype(BF16)
        s_b = vs_ref[j, TILE:TILE + N_HEADS_B * HEAD_K, c * HEAD_V:(c + 1) * HEAD_V]
        for p in range(N_PAIR):
            a0 = 2 * p * HEAD_V
            rows_a = slice(2 * p * HEAD_K, (2 * p + 1) * HEAD_K)
            rows_b = slice((2 * p + 1) * HEAD_K, (2 * p + 2) * HEAD_K)
            rhs = jnp.concatenate([
                jnp.concatenate([vb[r0:r0 + CHUNK_B, a0:a0 + HEAD_V], zero_blk], axis=1),
                jnp.concatenate([zero_blk, vb[r0:r0 + CHUNK_B, a0 + HEAD_V:a0 + 2 * HEAD_V]], axis=1),
                jnp.concatenate([s_f[rows_a, :], zero_blk], axis=1),
                jnp.concatenate([zero_blk, s_f[rows_b, :]], axis=1),
                jnp.concatenate([s_b[rows_a, :], zero_blk], axis=1),
                jnp.concatenate([zero_blk, s_b[rows_b, :]], axis=1),
            ], axis=0)
            lhs = jnp.concatenate(
                [p_blk[c][:, p * LANES:(p + 1) * LANES]]
                + [q_in[r0:r0 + CHUNK_B, p * LANES:(p + 1) * LANES] for q_in, _, _ in dirs],
                axis=1)
            o_ref[j, r0:r0 + CHUNK_B, a0:a0 + 2 * HEAD_V] = _dot(lhs, rhs)
        sf_ref[...] = dec_col[c] * s_fwd + kv_blk[c]

    emit(1)
    for hd in range(N_HEADS_B):
        c0 = hd * HEAD_V
        oh = o_ref[j, :, c0:c0 + HEAD_V]
        out_b = oh * _rms_scale(oh) * gnb_ref[...] * _silu(
            p_ref[t0:t0 + TILE, M_ZB + c0:M_ZB + c0 + HEAD_V])
        mix_ref[j, :, D_A + c0:D_A + c0 + HEAD_V] = out_b.astype(BF16)

    mixed = jnp.concatenate(
        [_dot(mix_ref[j], wout_ref[:, c0:c0 + PIECE_COLS]) for c0 in range(0, D_MODEL, PIECE_COLS)],
        axis=1)
    emit(1)
    out_ref[0, t0:t0 + TILE, :] = (
        x_ref[t0:t0 + TILE, :] + mixed * _rms_scale(mixed) * npost_ref[...])


def _layer_kernel(steps_per_row, m_steps, xn_ref, klr_ref, vs_ref, npre_ref, win_ref,
                  wsp_ref, bsp_ref, gva_ref, wgk_ref, bgk_ref, ltri_ref, gnb_ref,
                  wout_ref, npost_ref, out_ref, p0_ref, p1_ref, x0_ref, x1_ref, sf_ref, o_ref, mix_ref):
    i = pl.program_id(0)

    @pl.when(jnp.logical_or(i == 0, (i - 1) % steps_per_row == 0))
    def _():
        sf_ref[...] = jnp.zeros_like(sf_ref)

    def step(write_refs, read_refs):
        emit = _emitter(_projection_pieces(xn_ref, npre_ref, win_ref, *write_refs)
                        if write_refs is not None else [])
        for j in range(MAIN_SUB) if read_refs is not None else ():
            _mix(j, *read_refs, klr_ref, vs_ref, wsp_ref, bsp_ref, gva_ref, wgk_ref, bgk_ref,
                 ltri_ref, gnb_ref, wout_ref, npost_ref, out_ref, sf_ref, o_ref, mix_ref,
                 emit)
        emit(MAIN_SUB * P_COLS // PIECE_COLS)

    _ping_pong_steps(i, m_steps, step, (p0_ref, x0_ref), (p1_ref, x1_ref))


def _const_spec(shape):
    return pl.BlockSpec(shape, lambda *_: (0,) * len(shape))


def _hybrid_layer(x, wts):
    bsz, seq, _ = x.shape
    n_tiles = seq // TILE
    n_total = bsz * n_tiles

    params = pltpu.CompilerParams(dimension_semantics=("arbitrary",), vmem_limit_bytes=VMEM_LIMIT)

    n_steps = n_total // PRE_SUB

    def recurred_block(i):
        return (jnp.minimum(n_steps - i, n_steps - 1), 0, 0)

    klr_tiles, vs_tiles = pl.pallas_call(
        functools.partial(_bwd_state_kernel, n_tiles // PRE_SUB, n_steps),
        grid=(n_steps + 1,),
        in_specs=[
            pl.BlockSpec((1, PRE_SUB * TILE, D_MODEL),
                         lambda i: (jnp.maximum(n_steps - 1 - i, 0), 0, 0)),
            _const_spec((1, D_MODEL)),
            _const_spec((D_MODEL, PRE_COLS + PIECE_COLS)),
            _const_spec((LANES, D_K)),
            _const_spec((1, D_K)),
            _const_spec((TILE, TILE)),
        ],
        out_specs=[
            pl.BlockSpec((PRE_SUB, TILE, KLB_COLS), recurred_block),
            pl.BlockSpec((PRE_SUB, TILE + N_HEADS_B * HEAD_K, D_B), recurred_block),
        ],
        out_shape=[
            jax.ShapeDtypeStruct((n_total, TILE, KLB_COLS), F32),
            jax.ShapeDtypeStruct((n_total, TILE + N_HEADS_B * HEAD_K, D_B), BF16),
        ],
        scratch_shapes=[
            pltpu.VMEM((PRE_SUB * TILE, PRE_COLS), F32),
            pltpu.VMEM((PRE_SUB * TILE, PRE_COLS), F32),
            pltpu.VMEM((N_HEADS_B * HEAD_K, HEAD_V), F32),
        ],
        compiler_params=params,
    )(x.reshape(n_steps, PRE_SUB * TILE, D_MODEL), wts["npre"], wts["wkv"], wts["wgk_b"],
      wts["bgk_b"], wts["utri"])

    m_steps = n_total // MAIN_SUB

    def finished_block(i):
        return (jnp.maximum(i - 1, 0), 0, 0)

    y = pl.pallas_call(
        functools.partial(_layer_kernel, n_tiles // MAIN_SUB, m_steps),
        grid=(m_steps + 1,),
        in_specs=[
            pl.BlockSpec((1, MAIN_SUB * TILE, D_MODEL),
                         lambda i: (jnp.minimum(i, m_steps - 1), 0, 0)),
            pl.BlockSpec((MAIN_SUB, TILE, KLB_COLS), finished_block),
            pl.BlockSpec((MAIN_SUB, TILE + N_HEADS_B * HEAD_K, D_B), finished_block),
            _const_spec((1, D_MODEL)),
            _const_spec((D_MODEL, P_COLS)),
            _const_spec((N_HEADS_A // 2, CHUNK_A, 2 * CHUNK_A)),
            _const_spec((N_HEADS_A, CHUNK_A, HEAD_A)),
            _const_spec((1, D_A)),
            _const_spec((LANES, D_K)),
            _const_spec((1, D_K)),
            _const_spec((TILE, TILE)),
            _const_spec((1, HEAD_V)),
            _const_spec((D_A + D_B, D_MODEL + PIECE_COLS)),
            _const_spec((1, D_MODEL)),
        ],
        out_specs=pl.BlockSpec((1, MAIN_SUB * TILE, D_MODEL), finished_block),
        out_shape=jax.ShapeDtypeStruct((m_steps, MAIN_SUB * TILE, D_MODEL), x.dtype),
        scratch_shapes=[
            pltpu.VMEM((MAIN_SUB * TILE, P_COLS), F32),
            pltpu.VMEM((MAIN_SUB * TILE, P_COLS), F32),
            pltpu.VMEM((MAIN_SUB * TILE, D_MODEL), F32),
            pltpu.VMEM((MAIN_SUB * TILE, D_MODEL), F32),
            pltpu.VMEM((N_HEADS_B * HEAD_K, HEAD_V), F32),
            pltpu.VMEM((MAIN_SUB, TILE, D_B), F32),
            pltpu.VMEM((MAIN_SUB, TILE, D_A + D_B), BF16),
        ],
        compiler_params=params,
    )(x.reshape(m_steps, MAIN_SUB * TILE, D_MODEL), klr_tiles, vs_tiles, wts["npre"], wts["win"],
      wts["wsp"], wts["bsp"], wts["gva"], wts["wgk_f"], wts["bgk_f"], wts["ltri"], wts["gnb"],
      wts["wout"], wts["npost"])
    return y.reshape(x.shape)


def _prepare_weights(norm_pre, w_in, w_sp, b_sp, g_v_a, w_gk_fwd, b_gk_fwd, w_gk_bwd, b_gk_bwd,
                     g_norm_b, w_out, norm_post):
    w_in_bf = w_in.astype(BF16)
    wlr = jnp.zeros((D_MODEL, PIECE_COLS), BF16).at[:, 0:2 * GATE_RANK].set(w_in_bf[:, OFF_LRF:D_IN])
    wgk = jnp.zeros((LANES, 2 * D_K), BF16)
    wgk = wgk.at[0:GATE_RANK, 0:D_K].set(w_gk_fwd.astype(BF16))
    wgk = wgk.at[GATE_RANK:2 * GATE_RANK, D_K:2 * D_K].set(w_gk_bwd.astype(BF16))
    pad_cols = jnp.zeros((D_MODEL, PIECE_COLS), BF16)
    pos = jnp.arange(TILE)
    same_chunk = (pos[:, None] // CHUNK_B) == (pos[None, :] // CHUNK_B)
    ltri = (same_chunk & (pos[None, :] <= pos[:, None])).astype(BF16)
    utri = (same_chunk & (pos[None, :] >= pos[:, None])).astype(BF16)
    return {
        "npre": norm_pre.reshape(1, D_MODEL),
        "win": jnp.concatenate([w_in_bf[:, OFF_U:OFF_K], w_in_bf[:, OFF_ZB:OFF_LRF]], axis=1),
        "wkv": jnp.concatenate([w_in_bf[:, OFF_K:OFF_ZB], wlr, pad_cols], axis=1),
        "wsp": jnp.concatenate([w_sp[0::2], w_sp[1::2]], axis=-1).astype(BF16),
        "bsp": jnp.broadcast_to(b_sp[:, :, None], (N_HEADS_A, CHUNK_A, HEAD_A)),
        "gva": g_v_a.reshape(1, D_A),
        "wgk_f": wgk[:, 0:D_K],
        "wgk_b": wgk[:, D_K:2 * D_K],
        "bgk_f": b_gk_fwd.reshape(1, D_K),
        "bgk_b": b_gk_bwd.reshape(1, D_K),
        "ltri": ltri,
        "utri": utri,
        "gnb": g_norm_b.reshape(1, HEAD_V),
        "wout": jnp.concatenate([w_out.astype(BF16), pad_cols], axis=1),
        "npost": norm_post.reshape(1, D_MODEL),
    }


def kernel(x_prompt, x_sample, norm_pre, w_in, w_sp, b_sp, g_v_a, w_gk_fwd, b_gk_fwd,
           w_gk_bwd, b_gk_bwd, g_norm_b, w_out, norm_post):
    y_prompt, y_sample = x_prompt, x_sample
    for l in range(norm_pre.shape[0]):
        wts = _prepare_weights(norm_pre[l], w_in[l], w_sp[l], b_sp[l], g_v_a[l], w_gk_fwd[l],
                               b_gk_fwd[l], w_gk_bwd[l], b_gk_bwd[l], g_norm_b[l], w_out[l],
                               norm_post[l])
        y_prompt = _hybrid_layer(y_prompt, wts)
        y_sample = _hybrid_layer(y_sample, wts)
    return (y_prompt, y_sample)
```

```python
import functools

import jax
import jax.numpy as jnp
from jax import lax
from jax.experimental import pallas as pl
from jax.experimental.pallas import tpu as pltpu

F32 = jnp.float32
BF16 = jnp.bfloat16

D_MODEL = 1024
D_A = 512
N_HEADS_A = 4
HEAD_A = 128
CHUNK_A = 128
D_B = 512
N_HEADS_B = 4
HEAD_V = 128
D_K = 256
HEAD_K = 64
GATE_RANK = 16
GATE_NORMALIZER = 16.0
CHUNK_B = 64
EPS = 1e-6

OFF_U, OFF_V, OFF_ZA, OFF_Q, OFF_K, OFF_VB, OFF_ZB, OFF_LRF, OFF_LRB, D_IN = (
    0, 512, 1024, 1536, 1792, 2048, 2560, 3072, 3088, 3104)

LANES = 128
TILE = 256
N_CHUNK = TILE // CHUNK_B
N_PAIR = N_HEADS_B // 2
PIECE_COLS = 256
PRE_COLS = D_K + D_B + PIECE_COLS
PRE_SUB = 4
MAIN_SUB = 2
KLB_COLS = D_K + LANES + D_K
M_U, M_V, M_ZA, M_Q, M_ZB, P_COLS = 0, 512, 1024, 1536, 1792, 2304
VMEM_LIMIT = 56 * 1024 * 1024


def _dot(a, b):
    return jnp.dot(a, b, preferred_element_type=F32)


def _dot_nt(a, b):
    return lax.dot_general(a, b, (((1,), (1,)), ((), ())), preferred_element_type=F32)


def _gelu(x):
    return 0.5 * x * (1.0 + jnp.tanh(0.7978845608028654 * (x + 0.044715 * (x * x * x))))


def _silu(x):
    return x * (0.5 * jnp.tanh(0.5 * x) + 0.5)


def _log_sigmoid(x):
    return jnp.minimum(x, 0.0) - jnp.log1p(jnp.exp(-jnp.abs(x)))


def _rms_scale(x):
    return lax.rsqrt(jnp.mean(x * x, axis=-1, keepdims=True) + EPS)


def _chunk_cumsum(tri_ref, g):
    hi = g.astype(BF16)
    lo = (g - hi.astype(F32)).astype(BF16)
    n = g.shape[1]
    both = _dot(tri_ref[...], jnp.concatenate([hi, lo], axis=1))
    return both[:, 0:n] + both[:, n:2 * n]


def _decayed_keys(k, bcum, backward):
    kd, dec = [], []
    for c in range(N_CHUNK):
        r0 = c * CHUNK_B
        last = r0 if backward else r0 + CHUNK_B - 1
        b_last = bcum[last:last + 1, :]
        kd.append(k[r0:r0 + CHUNK_B, :] * jnp.exp(b_last - bcum[r0:r0 + CHUNK_B, :]))
        dec.append(jnp.exp(b_last))
    return jnp.concatenate(kd, axis=0), dec


def _chunk_kv(kt, v_par, c):
    cp, par = divmod(c, 2)
    parts = []
    for p in range(N_PAIR):
        full = _dot(kt[p * LANES:(p + 1) * LANES, cp * LANES:(cp + 1) * LANES],
                    v_par[cp][par][:, p * 2 * HEAD_V:(p + 1) * 2 * HEAD_V])
        parts.append(full[0:HEAD_K, 0:HEAD_V])
        parts.append(full[HEAD_K:2 * HEAD_K, HEAD_V:2 * HEAD_V])
    return jnp.concatenate(parts, axis=0)


def _decay_column(dec_row):
    return jnp.transpose(jnp.broadcast_to(dec_row, (LANES, D_K)))


def _parity_masked_values(vb):
    out = []
    row = lax.broadcasted_iota(jnp.int32, (2 * CHUNK_B, D_B), 0)
    for cp in range(N_CHUNK // 2):
        vp = vb[cp * 2 * CHUNK_B:(cp + 1) * 2 * CHUNK_B, :]
        zero = jnp.zeros_like(vp)
        out.append((jnp.where(row < CHUNK_B, vp, zero), jnp.where(row >= CHUNK_B, vp, zero)))
    return out


def _projection_pieces(x_ref, npre_ref, w_ref, p_ref, x_keep_ref=None):
    normed = {}

    def piece(r0, c0):
        def run():
            if r0 not in normed:
                x = x_ref[0, r0:r0 + TILE, :]
                if x_keep_ref is not None:
                    x_keep_ref[r0:r0 + TILE, :] = x
                normed[r0] = (x * _rms_scale(x) * npre_ref[...]).astype(BF16)
            p_ref[r0:r0 + TILE, c0:c0 + PIECE_COLS] = _dot(normed[r0], w_ref[:, c0:c0 + PIECE_COLS])
        return run

    return [piece(r0, c0) for r0 in range(0, p_ref.shape[0], TILE)
            for c0 in range(0, p_ref.shape[1], PIECE_COLS)]


def _emitter(pieces):
    def emit(n):
        for _ in range(min(n, len(pieces))):
            pieces.pop(0)()
    return emit


def _ping_pong_steps(i, n, step, even_refs, odd_refs):
    first, last = i == 0, i == n
    middle = jnp.logical_not(jnp.logical_or(first, last))

    @pl.when(first)
    def _():
        step(even_refs, None)

    @pl.when(jnp.logical_and(middle, i % 2 == 0))
    def _():
        step(even_refs, odd_refs)

    @pl.when(jnp.logical_and(middle, i % 2 == 1))
    def _():
        step(odd_refs, even_refs)

    @pl.when(last)
    def _():
        step(None, even_refs if n % 2 == 1 else odd_refs)


def _bwd_state_kernel(steps_per_row, n_steps, xn_ref, npre_ref, wkv_ref, wgk_ref, bgk_ref, utri_ref,
                      klr_ref, vs_ref, p0_ref, p1_ref, s_ref):
    i = pl.program_id(0)

    @pl.when(jnp.logical_or(i == 0, (n_steps - i) % steps_per_row == steps_per_row - 1))
    def _():
        s_ref[...] = jnp.zeros_like(s_ref)

    def step(write_refs, read_refs):
        emit = _emitter(_projection_pieces(xn_ref, npre_ref, wkv_ref, *write_refs)
                        if write_refs is not None else [])
        s = s_ref[...]
        for j in reversed(range(PRE_SUB) if read_refs is not None else ()):
            p_ref, r0 = read_refs[0], j * TILE
            emit(1)
            k = p_ref[r0:r0 + TILE, 0:D_K]
            vb = p_ref[r0:r0 + TILE, D_K:D_K + D_B].astype(BF16)
            plr = p_ref[r0:r0 + TILE, D_K + D_B:D_K + D_B + LANES]
            klr_ref[j, :, 0:D_K] = k
            klr_ref[j, :, D_K:D_K + LANES] = plr
            vs_ref[j, 0:TILE, :] = vb
            g = _log_sigmoid(_dot(plr.astype(BF16), wgk_ref[...]) + bgk_ref[...]) * (
                1.0 / GATE_NORMALIZER)
            emit(1)
            bcum = _chunk_cumsum(utri_ref, g)
            klr_ref[j, :, D_K + LANES:KLB_COLS] = bcum
            kdec, dec = _decayed_keys(k, bcum, backward=True)
            kt = jnp.transpose(kdec).astype(BF16)
            emit(1)
            v_par = _parity_masked_values(vb)
            kv = [_chunk_kv(kt, v_par, c) for c in range(N_CHUNK)]
            dec_col = [_decay_column(dec[c]) for c in range(N_CHUNK)]
            emit(1)
            for c in reversed(range(N_CHUNK)):
                vs_ref[j, TILE:TILE + N_HEADS_B * HEAD_K, c * HEAD_V:(c + 1) * HEAD_V] = (
                    s.astype(BF16))
                s = dec_col[c] * s + kv[c]
        emit(PRE_SUB * PRE_COLS // PIECE_COLS)
        s_ref[...] = s

    _ping_pong_steps(i, n_steps, step, (p0_ref,), (p1_ref,))


def _mix(j, p_ref, x_ref, klr_ref, vs_ref, wsp_ref, bsp_ref, gva_ref, wgk_ref, bgk_ref,
         ltri_ref, gnb_ref, wout_ref, npost_ref, out_ref, sf_ref, o_ref, mix_ref, emit):
    t0 = j * TILE

    emit(1)
    zero_a = jnp.zeros((CHUNK_A, HEAD_A), BF16)
    for c in range(TILE // CHUNK_A):
        r0 = c * CHUNK_A
        vg = _gelu(p_ref[t0 + r0:t0 + r0 + CHUNK_A, M_V:M_ZA])
        vc = vg - jnp.mean(vg, axis=-1, keepdims=True)
        vn_c = (vc * _rms_scale(vc) * gva_ref[...]).astype(BF16)
        emit(1 if c == 0 else 0)
        for pp in range(N_HEADS_A // 2):
            ca, cb = 2 * pp * HEAD_A, (2 * pp + 1) * HEAD_A
            sv_pair = _dot(wsp_ref[pp], jnp.concatenate([
                jnp.concatenate([vn_c[:, ca:ca + HEAD_A], zero_a], axis=1),
                jnp.concatenate([zero_a, vn_c[:, cb:cb + HEAD_A]], axis=1)], axis=0))
            for hd, c0 in ((2 * pp, ca), (2 * pp + 1, cb)):
                sv = sv_pair[:, c0 - ca:c0 - ca + HEAD_A] + bsp_ref[hd]
                out_a = _gelu(p_ref[t0 + r0:t0 + r0 + CHUNK_A, M_U + c0:M_U + c0 + HEAD_A]) * sv
                out_a = out_a * _silu(
                    p_ref[t0 + r0:t0 + r0 + CHUNK_A, M_ZA + c0:M_ZA + c0 + HEAD_A])
                mix_ref[j, r0:r0 + CHUNK_A, c0:c0 + HEAD_A] = out_a.astype(BF16)
            emit(1 if pp == 0 else 0)

    q = p_ref[t0:t0 + TILE, M_Q:M_ZB] * (HEAD_K ** -0.5)
    k = klr_ref[j, :, 0:D_K]
    vb = vs_ref[j, 0:TILE, :]
    plr = klr_ref[j, :, D_K:D_K + LANES]
    emit(1)
    g_fwd = _log_sigmoid(_dot(plr.astype(BF16), wgk_ref[...]) + bgk_ref[...]) * (
        1.0 / GATE_NORMALIZER)
    v_par = _parity_masked_values(vb)

    lane_head = lax.broadcasted_iota(jnp.int32, (CHUNK_B, D_K), 1) // HEAD_K
    lane_pos = lax.broadcasted_iota(jnp.int32, (CHUNK_B, D_K), 1) % CHUNK_B
    row_pos = lax.broadcasted_iota(jnp.int32, (CHUNK_B, D_K), 0)
    zero_blk = jnp.zeros((CHUNK_B, HEAD_V), BF16)

    dirs = []
    for d in range(2):
        backward = d == 1
        emit(1)
        bcum = klr_ref[j, :, D_K + LANES:KLB_COLS] if backward else _chunk_cumsum(ltri_ref, g_fwd)
        q_in = (q * jnp.exp(bcum)).astype(BF16)
        k_in = (k * jnp.exp(-bcum)).astype(BF16)
        keep = (row_pos < lane_pos) if backward else (row_pos >= lane_pos)
        dirs.append((q_in, k_in, keep))
        if not backward:
            kdec, dec = _decayed_keys(k, bcum, backward)
            kt = jnp.transpose(kdec).astype(BF16)

    p_blk, kv_blk, dec_col = {}, {}, {}
    for c in range(N_CHUNK):
        scores = []
        for q_in, k_in, _ in dirs:
            k_c = k_in[c * CHUNK_B:(c + 1) * CHUNK_B, :]
            k_bd = jnp.concatenate(
                [jnp.where(lane_head == hd, k_c, jnp.zeros_like(k_c)) for hd in range(N_HEADS_B)],
                axis=0)
            scores.append(_dot_nt(q_in[c * CHUNK_B:(c + 1) * CHUNK_B, :], k_bd))
        p_blk[c] = jnp.where(dirs[0][2], scores[0], scores[1]).astype(BF16)
    for c in range(N_CHUNK):
        kv_blk[c] = _chunk_kv(kt, v_par, c)
        dec_col[c] = _decay_column(dec[c])

    for c in range(N_CHUNK):
        r0 = c * CHUNK_B
        s_fwd = sf_ref[...]
        s_f = s_fwd.astype(BF16)
        s_b = vs_ref[j, TILE:TILE + N_HEADS_B * HEAD_K, c * HEAD_V:(c + 1) * HEAD_V]
        for p in range(N_PAIR):
            a0 = 2 * p * HEAD_V
            rows_a = slice(2 * p * HEAD_K, (2 * p + 1) * HEAD_K)
            rows_b = slice((2 * p + 1) * HEAD_K, (2 * p + 2) * HEAD_K)
            rhs = jnp.concatenate([
                jnp.concatenate([vb[r0:r0 + CHUNK_B, a0:a0 + HEAD_V], zero_blk], axis=1),
                jnp.concatenate([zero_blk, vb[r0:r0 + CHUNK_B, a0 + HEAD_V:a0 + 2 * HEAD_V]], axis=1),
                jnp.concatenate([s_f[rows_a, :], zero_blk], axis=1),
                jnp.concatenate([zero_blk, s_f[rows_b, :]], axis=1),
                jnp.concatenate([s_b[rows_a, :], zero_blk], axis=1),
                jnp.concatenate([zero_blk, s_b[rows_b, :]], axis=1),
            ], axis=0)
            lhs = jnp.concatenate(
                [p_blk[c][:, p * LANES:(p + 1) * LANES]]
                + [q_in[r0:r0 + CHUNK_B, p * LANES:(p + 1) * LANES] for q_in, _, _ in dirs],
                axis=1)
            o_ref[j, r0:r0 + CHUNK_B, a0:a0 + 2 * HEAD_V] = _dot(lhs, rhs)
        sf_ref[...] = dec_col[c] * s_fwd + kv_blk[c]

    emit(1)
    for hd in range(N_HEADS_B):
        c0 = hd * HEAD_V
        oh = o_ref[j, :, c0:c0 + HEAD_V]
        out_b = oh * _rms_scale(oh) * gnb_ref[...] * _silu(
            p_ref[t0:t0 + TILE, M_ZB + c0:M_ZB + c0 + HEAD_V])
        mix_ref[j, :, D_A + c0:D_A + c0 + HEAD_V] = out_b.astype(BF16)

    mixed = jnp.concatenate(
        [_dot(mix_ref[j], wout_ref[:, c0:c0 + PIECE_COLS]) for c0 in range(0, D_MODEL, PIECE_COLS)],
        axis=1)
    emit(1)
    out_ref[0, t0:t0 + TILE, :] = (
        x_ref[t0:t0 + TILE, :] + mixed * _rms_scale(mixed) * npost_ref[...])


def _layer_kernel(steps_per_row, m_steps, xn_ref, klr_ref, vs_ref, npre_ref, win_ref,
                  wsp_ref, bsp_ref, gva_ref, wgk_ref, bgk_ref, ltri_ref, gnb_ref,
                  wout_ref, npost_ref, out_ref, p0_ref, p1_ref, x0_ref, x1_ref, sf_ref, o_ref, mix_ref):
    i = pl.program_id(0)

    @pl.when(jnp.logical_or(i == 0, (i - 1) % steps_per_row == 0))
    def _():
        sf_ref[...] = jnp.zeros_like(sf_ref)

    def step(write_refs, read_refs):
        emit = _emitter(_projection_pieces(xn_ref, npre_ref, win_ref, *write_refs)
                        if write_refs is not None else [])
        for j in range(MAIN_SUB) if read_refs is not None else ():
            _mix(j, *read_refs, klr_ref, vs_ref, wsp_ref, bsp_ref, gva_ref, wgk_ref, bgk_ref,
                 ltri_ref, gnb_ref, wout_ref, npost_ref, out_ref, sf_ref, o_ref, mix_ref,
                 emit)
        emit(MAIN_SUB * P_COLS // PIECE_COLS)

    _ping_pong_steps(i, m_steps, step, (p0_ref, x0_ref), (p1_ref, x1_ref))


def _const_spec(shape):
    return pl.BlockSpec(shape, lambda *_: (0,) * len(shape))


def _hybrid_layer(x, wts):
    bsz, seq, _ = x.shape
    n_tiles = seq // TILE
    n_total = bsz * n_tiles

    params = pltpu.CompilerParams(dimension_semantics=("arbitrary",), vmem_limit_bytes=VMEM_LIMIT)

    n_steps = n_total // PRE_SUB

    def recurred_block(i):
        return (jnp.minimum(n_steps - i, n_steps - 1), 0, 0)

    klr_tiles, vs_tiles = pl.pallas_call(
        functools.partial(_bwd_state_kernel, n_tiles // PRE_SUB, n_steps),
        grid=(n_steps + 1,),
        in_specs=[
            pl.BlockSpec((1, PRE_SUB * TILE, D_MODEL),
                         lambda i: (jnp.maximum(n_steps - 1 - i, 0), 0, 0)),
            _const_spec((1, D_MODEL)),
            _const_spec((D_MODEL, PRE_COLS + PIECE_COLS)),
            _const_spec((LANES, D_K)),
            _const_spec((1, D_K)),
            _const_spec((TILE, TILE)),
        ],
        out_specs=[
            pl.BlockSpec((PRE_SUB, TILE, KLB_COLS), recurred_block),
            pl.BlockSpec((PRE_SUB, TILE + N_HEADS_B * HEAD_K, D_B), recurred_block),
        ],
        out_shape=[
            jax.ShapeDtypeStruct((n_total, TILE, KLB_COLS), F32),
            jax.ShapeDtypeStruct((n_total, TILE + N_HEADS_B * HEAD_K, D_B), BF16),
        ],
        scratch_shapes=[
            pltpu.VMEM((PRE_SUB * TILE, PRE_COLS), F32),
            pltpu.VMEM((PRE_SUB * TILE, PRE_COLS), F32),
            pltpu.VMEM((N_HEADS_B * HEAD_K, HEAD_V), F32),
        ],
        compiler_params=params,
    )(x.reshape(n_steps, PRE_SUB * TILE, D_MODEL), wts["npre"], wts["wkv"], wts["wgk_b"],
      wts["bgk_b"], wts["utri"])

    m_steps = n_total // MAIN_SUB

    def finished_block(i):
        return (jnp.maximum(i - 1, 0), 0, 0)

    y = pl.pallas_call(
        functools.partial(_layer_kernel, n_tiles // MAIN_SUB, m_steps),
        grid=(m_steps + 1,),
        in_specs=[
            pl.BlockSpec((1, MAIN_SUB * TILE, D_MODEL),
                         lambda i: (jnp.minimum(i, m_steps - 1), 0, 0)),
            pl.BlockSpec((MAIN_SUB, TILE, KLB_COLS), finished_block),
            pl.BlockSpec((MAIN_SUB, TILE + N_HEADS_B * HEAD_K, D_B), finished_block),
            _const_spec((1, D_MODEL)),
            _const_spec((D_MODEL, P_COLS)),
            _const_spec((N_HEADS_A // 2, CHUNK_A, 2 * CHUNK_A)),
            _const_spec((N_HEADS_A, CHUNK_A, HEAD_A)),
            _const_spec((1, D_A)),
            _const_spec((LANES, D_K)),
            _const_spec((1, D_K)),
            _const_spec((TILE, TILE)),
            _const_spec((1, HEAD_V)),
            _const_spec((D_A + D_B, D_MODEL + PIECE_COLS)),
            _const_spec((1, D_MODEL)),
        ],
        out_specs=pl.BlockSpec((1, MAIN_SUB * TILE, D_MODEL), finished_block),
        out_shape=jax.ShapeDtypeStruct((m_steps, MAIN_SUB * TILE, D_MODEL), x.dtype),
        scratch_shapes=[
            pltpu.VMEM((MAIN_SUB * TILE, P_COLS), F32),
            pltpu.VMEM((MAIN_SUB * TILE, P_COLS), F32),
            pltpu.VMEM((MAIN_SUB * TILE, D_MODEL), F32),
            pltpu.VMEM((MAIN_SUB * TILE, D_MODEL), F32),
            pltpu.VMEM((N_HEADS_B * HEAD_K, HEAD_V), F32),
            pltpu.VMEM((MAIN_SUB, TILE, D_B), F32),
            pltpu.VMEM((MAIN_SUB, TILE, D_A + D_B), BF16),
        ],
        compiler_params=params,
    )(x.reshape(m_steps, MAIN_SUB * TILE, D_MODEL), klr_tiles, vs_tiles, wts["npre"], wts["win"],
      wts["wsp"], wts["bsp"], wts["gva"], wts["wgk_f"], wts["bgk_f"], wts["ltri"], wts["gnb"],
      wts["wout"], wts["npost"])
    return y.reshape(x.shape)


def _prepare_weights(norm_pre, w_in, w_sp, b_sp, g_v_a, w_gk_fwd, b_gk_fwd, w_gk_bwd, b_gk_bwd,
                     g_norm_b, w_out, norm_post):
    w_in_bf = w_in.astype(BF16)
    wlr = jnp.zeros((D_MODEL, PIECE_COLS), BF16).at[:, 0:2 * GATE_RANK].set(w_in_bf[:, OFF_LRF:D_IN])
    wgk = jnp.zeros((LANES, 2 * D_K), BF16)
    wgk = wgk.at[0:GATE_RANK, 0:D_K].set(w_gk_fwd.astype(BF16))
    wgk = wgk.at[GATE_RANK:2 * GATE_RANK, D_K:2 * D_K].set(w_gk_bwd.astype(BF16))
    pad_cols = jnp.zeros((D_MODEL, PIECE_COLS), BF16)
    pos = jnp.arange(TILE)
    same_chunk = (pos[:, None] // CHUNK_B) == (pos[None, :] // CHUNK_B)
    ltri = (same_chunk & (pos[None, :] <= pos[:, None])).astype(BF16)
    utri = (same_chunk & (pos[None, :] >= pos[:, None])).astype(BF16)
    return {
        "npre": norm_pre.reshape(1, D_MODEL),
        "win": jnp.concatenate([w_in_bf[:, OFF_U:OFF_K], w_in_bf[:, OFF_ZB:OFF_LRF]], axis=1),
        "wkv": jnp.concatenate([w_in_bf[:, OFF_K:OFF_ZB], wlr, pad_cols], axis=1),
        "wsp": jnp.concatenate([w_sp[0::2], w_sp[1::2]], axis=-1).astype(BF16),
        "bsp": jnp.broadcast_to(b_sp[:, :, None], (N_HEADS_A, CHUNK_A, HEAD_A)),
        "gva": g_v_a.reshape(1, D_A),
        "wgk_f": wgk[:, 0:D_K],
        "wgk_b": wgk[:, D_K:2 * D_K],
        "bgk_f": b_gk_fwd.reshape(1, D_K),
        "bgk_b": b_gk_bwd.reshape(1, D_K),
        "ltri": ltri,
        "utri": utri,
        "gnb": g_norm_b.reshape(1, HEAD_V),
        "wout": jnp.concatenate([w_out.astype(BF16), pad_cols], axis=1),
        "npost": norm_post.reshape(1, D_MODEL),
    }


def kernel(x_prompt, x_sample, norm_pre, w_in, w_sp, b_sp, g_v_a, w_gk_fwd, b_gk_fwd,
           w_gk_bwd, b_gk_bwd, g_norm_b, w_out, norm_post):
    y_prompt, y_sample = x_prompt, x_sample
    for l in range(norm_pre.shape[0]):
        wts = _prepare_weights(norm_pre[l], w_in[l], w_sp[l], b_sp[l], g_v_a[l], w_gk_fwd[l],
                               b_gk_fwd[l], w_gk_bwd[l], b_gk_bwd[l], g_norm_b[l], w_out[l],
                               norm_post[l])
        y_prompt = _hybrid_layer(y_prompt, wts)
        y_sample = _hybrid_layer(y_sample, wts)
    return (y_prompt, y_sample)
```

```python
import functools

import jax
import jax.numpy as jnp
from jax import lax
from jax.experimental import pallas as pl
from jax.experimental.pallas import tpu as pltpu

F32 = jnp.float32
BF16 = jnp.bfloat16

D_MODEL = 1024
D_A = 512
N_HEADS_A = 4
HEAD_A = 128
CHUNK_A = 128
D_B = 512
N_HEADS_B = 4
HEAD_V = 128
D_K = 256
HEAD_K = 64
GATE_RANK = 16
GATE_NORMALIZER = 16.0
CHUNK_B = 64
EPS = 1e-6

OFF_U, OFF_V, OFF_ZA, OFF_Q, OFF_K, OFF_VB, OFF_ZB, OFF_LRF, OFF_LRB, D_IN = (
    0, 512, 1024, 1536, 1792, 2048, 2560, 3072, 3088, 3104)

LANES = 128
TILE = 256
N_CHUNK = TILE // CHUNK_B
N_PAIR = N_HEADS_B // 2
PIECE_COLS = 256
PRE_COLS = D_K + D_B + PIECE_COLS
PRE_SUB = 4
MAIN_SUB = 2
KLB_COLS = D_K + LANES + D_K
M_U, M_V, M_ZA, M_Q, M_ZB, P_COLS = 0, 512, 1024, 1536, 1792, 2304
VMEM_LIMIT = 56 * 1024 * 1024


def _dot(a, b):
    return jnp.dot(a, b, preferred_element_type=F32)


def _dot_nt(a, b):
    return lax.dot_general(a, b, (((1,), (1,)), ((), ())), preferred_element_type=F32)


def _gelu(x):
    return 0.5 * x * (1.0 + jnp.tanh(0.7978845608028654 * (x + 0.044715 * (x * x * x))))


def _silu(x):
    return x * (0.5 * jnp.tanh(0.5 * x) + 0.5)


def _log_sigmoid(x):
    return jnp.minimum(x, 0.0) - jnp.log1p(jnp.exp(-jnp.abs(x)))


def _rms_scale(x):
    return lax.rsqrt(jnp.mean(x * x, axis=-1, keepdims=True) + EPS)


def _chunk_cumsum(tri_ref, g):
    hi = g.astype(BF16)
    lo = (g - hi.astype(F32)).astype(BF16)
    n = g.shape[1]
    both = _dot(tri_ref[...], jnp.concatenate([hi, lo], axis=1))
    return both[:, 0:n] + both[:, n:2 * n]


def _decayed_keys(k, bcum, backward):
    kd, dec = [], []
    for c in range(N_CHUNK):
        r0 = c * CHUNK_B
        last = r0 if backward else r0 + CHUNK_B - 1
        b_last = bcum[last:last + 1, :]
        kd.append(k[r0:r0 + CHUNK_B, :] * jnp.exp(b_last - bcum[r0:r0 + CHUNK_B, :]))
        dec.append(jnp.exp(b_last))
    return jnp.concatenate(kd, axis=0), dec


def _chunk_kv(kt, v_par, c):
    cp, par = divmod(c, 2)
    parts = []
    for p in range(N_PAIR):
        full = _dot(kt[p * LANES:(p + 1) * LANES, cp * LANES:(cp + 1) * LANES],
                    v_par[cp][par][:, p * 2 * HEAD_V:(p + 1) * 2 * HEAD_V])
        parts.append(full[0:HEAD_K, 0:HEAD_V])
        parts.append(full[HEAD_K:2 * HEAD_K, HEAD_V:2 * HEAD_V])
    return jnp.concatenate(parts, axis=0)


def _decay_column(dec_row):
    return jnp.transpose(jnp.broadcast_to(dec_row, (LANES, D_K)))


def _parity_masked_values(vb):
    out = []
    row = lax.broadcasted_iota(jnp.int32, (2 * CHUNK_B, D_B), 0)
    for cp in range(N_CHUNK // 2):
        vp = vb[cp * 2 * CHUNK_B:(cp + 1) * 2 * CHUNK_B, :]
        zero = jnp.zeros_like(vp)
        out.append((jnp.where(row < CHUNK_B, vp, zero), jnp.where(row >= CHUNK_B, vp, zero)))
    return out


def _projection_pieces(x_ref, npre_ref, w_ref, p_ref, x_keep_ref=None):
    normed = {}

    def piece(r0, c0):
        def run():
            if r0 not in normed:
                x = x_ref[0, r0:r0 + TILE, :]
                if x_keep_ref is not None:
                    x_keep_ref[r0:r0 + TILE, :] = x
                normed[r0] = (x * _rms_scale(x) * npre_ref[...]).astype(BF16)
            p_ref[r0:r0 + TILE, c0:c0 + PIECE_COLS] = _dot(normed[r0], w_ref[:, c0:c0 + PIECE_COLS])
        return run

    return [piece(r0, c0) for r0 in range(0, p_ref.shape[0], TILE)
            for c0 in range(0, p_ref.shape[1], PIECE_COLS)]


def _emitter(pieces):
    def emit(n):
        for _ in range(min(n, len(pieces))):
            pieces.pop(0)()
    return emit


def _ping_pong_steps(i, n, step, even_refs, odd_refs):
    first, last = i == 0, i == n
    middle = jnp.logical_not(jnp.logical_or(first, last))

    @pl.when(first)
    def _():
        step(even_refs, None)

    @pl.when(jnp.logical_and(middle, i % 2 == 0))
    def _():
        step(even_refs, odd_refs)

    @pl.when(jnp.logical_and(middle, i % 2 == 1))
    def _():
        step(odd_refs, even_refs)

    @pl.when(last)
    def _():
        step(None, even_refs if n % 2 == 1 else odd_refs)


def _bwd_state_kernel(steps_per_row, n_steps, xn_ref, npre_ref, wkv_ref, wgk_ref, bgk_ref, utri_ref,
                      klr_ref, vs_ref, p0_ref, p1_ref, s_ref):
    i = pl.program_id(0)

    @pl.when(jnp.logical_or(i == 0, (n_steps - i) % steps_per_row == steps_per_row - 1))
    def _():
        s_ref[...] = jnp.zeros_like(s_ref)

    def step(write_refs, read_refs):
        emit = _emitter(_projection_pieces(xn_ref, npre_ref, wkv_ref, *write_refs)
                        if write_refs is not None else [])
        s = s_ref[...]
        for j in reversed(range(PRE_SUB) if read_refs is not None else ()):
            p_ref, r0 = read_refs[0], j * TILE
            emit(1)
            k = p_ref[r0:r0 + TILE, 0:D_K]
            vb = p_ref[r0:r0 + TILE, D_K:D_K + D_B].astype(BF16)
            plr = p_ref[r0:r0 + TILE, D_K + D_B:D_K + D_B + LANES]
            klr_ref[j, :, 0:D_K] = k
            klr_ref[j, :, D_K:D_K + LANES] = plr
            vs_ref[j, 0:TILE, :] = vb
            g = _log_sigmoid(_dot(plr.astype(BF16), wgk_ref[...]) + bgk_ref[...]) * (
                1.0 / GATE_NORMALIZER)
            emit(1)
            bcum = _chunk_cumsum(utri_ref, g)
            klr_ref[j, :, D_K + LANES:KLB_COLS] = bcum
            kdec, dec = _decayed_keys(k, bcum, backward=True)
            kt = jnp.transpose(kdec).astype(BF16)
            emit(1)
            v_par = _parity_masked_values(vb)
            kv = [_chunk_kv(kt, v_par, c) for c in range(N_CHUNK)]
            dec_col = [_decay_column(dec[c]) for c in range(N_CHUNK)]
            emit(1)
            for c in reversed(range(N_CHUNK)):
                vs_ref[j, TILE:TILE + N_HEADS_B * HEAD_K, c * HEAD_V:(c + 1) * HEAD_V] = (
                    s.astype(BF16))
                s = dec_col[c] * s + kv[c]
        emit(PRE_SUB * PRE_COLS // PIECE_COLS)
        s_ref[...] = s

    _ping_pong_steps(i, n_steps, step, (p0_ref,), (p1_ref,))


def _mix(j, p_ref, x_ref, klr_ref, vs_ref, wsp_ref, bsp_ref, gva_ref, wgk_ref, bgk_ref,
         ltri_ref, gnb_ref, wout_ref, npost_ref, out_ref, sf_ref, o_ref, mix_ref, emit):
    t0 = j * TILE

    emit(2)
    vg = _gelu(p_ref[t0:t0 + TILE, M_V:M_ZA])
    vc = vg - jnp.mean(vg, axis=-1, keepdims=True)
    vn = (vc * _rms_scale(vc) * gva_ref[...]).astype(BF16)
    zero_a = jnp.zeros((CHUNK_A, HEAD_A), BF16)
    for c in range(TILE // CHUNK_A):
        r0 = c * CHUNK_A
        for pp in range(N_HEADS_A // 2):
            ca, cb = 2 * pp * HEAD_A, (2 * pp + 1) * HEAD_A
            sv_pair = _dot(wsp_ref[pp], jnp.concatenate([
                jnp.concatenate([vn[r0:r0 + CHUNK_A, ca:ca + HEAD_A], zero_a], axis=1),
                jnp.concatenate([zero_a, vn[r0:r0 + CHUNK_A, cb:cb + HEAD_A]], axis=1)], axis=0))
            for hd, c0 in ((2 * pp, ca), (2 * pp + 1, cb)):
                sv = sv_pair[:, c0 - ca:c0 - ca + HEAD_A] + bsp_ref[hd]
                out_a = _gelu(p_ref[t0 + r0:t0 + r0 + CHUNK_A, M_U + c0:M_U + c0 + HEAD_A]) * sv
                out_a = out_a * _silu(
                    p_ref[t0 + r0:t0 + r0 + CHUNK_A, M_ZA + c0:M_ZA + c0 + HEAD_A])
                mix_ref[j, r0:r0 + CHUNK_A, c0:c0 + HEAD_A] = out_a.astype(BF16)
            emit(1 if pp == 0 else 0)

    q = p_ref[t0:t0 + TILE, M_Q:M_ZB] * (HEAD_K ** -0.5)
    k = klr_ref[j, :, 0:D_K]
    vb = vs_ref[j, 0:TILE, :]
    plr = klr_ref[j, :, D_K:D_K + LANES]
    emit(1)
    g_fwd = _log_sigmoid(_dot(plr.astype(BF16), wgk_ref[...]) + bgk_ref[...]) * (
        1.0 / GATE_NORMALIZER)
    v_par = _parity_masked_values(vb)

    lane_head = lax.broadcasted_iota(jnp.int32, (CHUNK_B, D_K), 1) // HEAD_K
    lane_pos = lax.broadcasted_iota(jnp.int32, (CHUNK_B, D_K), 1) % CHUNK_B
    row_pos = lax.broadcasted_iota(jnp.int32, (CHUNK_B, D_K), 0)
    zero_blk = jnp.zeros((CHUNK_B, HEAD_V), BF16)

    dirs = []
    for d in range(2):
        backward = d == 1
        emit(1)
        bcum = klr_ref[j, :, D_K + LANES:KLB_COLS] if backward else _chunk_cumsum(ltri_ref, g_fwd)
        q_in = (q * jnp.exp(bcum)).astype(BF16)
        k_in = (k * jnp.exp(-bcum)).astype(BF16)
        keep = (row_pos < lane_pos) if backward else (row_pos >= lane_pos)
        dirs.append((q_in, k_in, keep))
        if not backward:
            kdec, dec = _decayed_keys(k, bcum, backward)
            kt = jnp.transpose(kdec).astype(BF16)

    p_blk, kv_blk, dec_col = {}, {}, {}
    for c in range(N_CHUNK):
        scores = []
        for q_in, k_in, _ in dirs:
            k_c = k_in[c * CHUNK_B:(c + 1) * CHUNK_B, :]
            k_bd = jnp.concatenate(
                [jnp.where(lane_head == hd, k_c, jnp.zeros_like(k_c)) for hd in range(N_HEADS_B)],
                axis=0)
            scores.append(_dot_nt(q_in[c * CHUNK_B:(c + 1) * CHUNK_B, :], k_bd))
        p_blk[c] = jnp.where(dirs[0][2], scores[0], scores[1]).astype(BF16)
    for c in range(N_CHUNK):
        kv_blk[c] = _chunk_kv(kt, v_par, c)
        dec_col[c] = _decay_column(dec[c])

    for c in range(N_CHUNK):
        r0 = c * CHUNK_B
        s_fwd = sf_ref[...]
        s_f = s_fwd.astype(BF16)
        s_b = vs_ref[j, TILE:TILE + N_HEADS_B * HEAD_K, c * HEAD_V:(c + 1) * HEAD_V]
        for p in range(N_PAIR):
            a0 = 2 * p * HEAD_V
            rows_a = slice(2 * p * HEAD_K, (2 * p + 1) * HEAD_K)
            rows_b = slice((2 * p + 1) * HEAD_K, (2 * p + 2) * HEAD_K)
            rhs = jnp.concatenate([
                jnp.concatenate([vb[r0:r0 + CHUNK_B, a0:a0 + HEAD_V], zero_blk], axis=1),
                jnp.concatenate([zero_blk, vb[r0:r0 + CHUNK_B, a0 + HEAD_V:a0 + 2 * HEAD_V]], axis=1),
                jnp.concatenate([s_f[rows_a, :], zero_blk], axis=1),
                jnp.concatenate([zero_blk, s_f[rows_b, :]], axis=1),
                jnp.concatenate([s_b[rows_a, :], zero_blk], axis=1),
                jnp.concatenate([zero_blk, s_b[rows_b, :]], axis=1),
            ], axis=0)
            lhs = jnp.concatenate(
                [p_blk[c][:, p * LANES:(p + 1) * LANES]]
                + [q_in[r0:r0 + CHUNK_B, p * LANES:(p + 1) * LANES] for q_in, _, _ in dirs],
                axis=1)
            o_ref[j, r0:r0 + CHUNK_B, a0:a0 + 2 * HEAD_V] = _dot(lhs, rhs)
        sf_ref[...] = dec_col[c] * s_fwd + kv_blk[c]

    emit(1)
    for hd in range(N_HEADS_B):
        c0 = hd * HEAD_V
        oh = o_ref[j, :, c0:c0 + HEAD_V]
        out_b = oh * _rms_scale(oh) * gnb_ref[...] * _silu(
            p_ref[t0:t0 + TILE, M_ZB + c0:M_ZB + c0 + HEAD_V])
        mix_ref[j, :, D_A + c0:D_A + c0 + HEAD_V] = out_b.astype(BF16)

    mixed = jnp.concatenate(
        [_dot(mix_ref[j], wout_ref[:, c0:c0 + PIECE_COLS]) for c0 in range(0, D_MODEL, PIECE_COLS)],
        axis=1)
    emit(1)
    out_ref[0, t0:t0 + TILE, :] = (
        x_ref[t0:t0 + TILE, :] + mixed * _rms_scale(mixed) * npost_ref[...])


def _layer_kernel(steps_per_row, m_steps, xn_ref, klr_ref, vs_ref, npre_ref, win_ref,
                  wsp_ref, bsp_ref, gva_ref, wgk_ref, bgk_ref, ltri_ref, gnb_ref,
                  wout_ref, npost_ref, out_ref, p0_ref, p1_ref, x0_ref, x1_ref, sf_ref, o_ref, mix_ref):
    i = pl.program_id(0)

    @pl.when(jnp.logical_or(i == 0, (i - 1) % steps_per_row == 0))
    def _():
        sf_ref[...] = jnp.zeros_like(sf_ref)

    def step(write_refs, read_refs):
        emit = _emitter(_projection_pieces(xn_ref, npre_ref, win_ref, *write_refs)
                        if write_refs is not None else [])
        for j in range(MAIN_SUB) if read_refs is not None else ():
            _mix(j, *read_refs, klr_ref, vs_ref, wsp_ref, bsp_ref, gva_ref, wgk_ref, bgk_ref,
                 ltri_ref, gnb_ref, wout_ref, npost_ref, out_ref, sf_ref, o_ref, mix_ref,
                 emit)
        emit(MAIN_SUB * P_COLS // PIECE_COLS)

    _ping_pong_steps(i, m_steps, step, (p0_ref, x0_ref), (p1_ref, x1_ref))


def _const_spec(shape):
    return pl.BlockSpec(shape, lambda *_: (0,) * len(shape), pipeline_mode=pl.Buffered(1))


def _hybrid_layer(x, wts):
    bsz, seq, _ = x.shape
    n_tiles = seq // TILE
    n_total = bsz * n_tiles

    params = pltpu.CompilerParams(dimension_semantics=("arbitrary",), vmem_limit_bytes=VMEM_LIMIT)

    n_steps = n_total // PRE_SUB

    def recurred_block(i):
        return (jnp.minimum(n_steps - i, n_steps - 1), 0, 0)

    klr_tiles, vs_tiles = pl.pallas_call(
        functools.partial(_bwd_state_kernel, n_tiles // PRE_SUB, n_steps),
        grid=(n_steps + 1,),
        in_specs=[
            pl.BlockSpec((1, PRE_SUB * TILE, D_MODEL),
                         lambda i: (jnp.maximum(n_steps - 1 - i, 0), 0, 0)),
            _const_spec((1, D_MODEL)),
            _const_spec((D_MODEL, PRE_COLS + PIECE_COLS)),
            _const_spec((LANES, D_K)),
            _const_spec((1, D_K)),
            _const_spec((TILE, TILE)),
        ],
        out_specs=[
            pl.BlockSpec((PRE_SUB, TILE, KLB_COLS), recurred_block),
            pl.BlockSpec((PRE_SUB, TILE + N_HEADS_B * HEAD_K, D_B), recurred_block),
        ],
        out_shape=[
            jax.ShapeDtypeStruct((n_total, TILE, KLB_COLS), F32),
            jax.ShapeDtypeStruct((n_total, TILE + N_HEADS_B * HEAD_K, D_B), BF16),
        ],
        scratch_shapes=[
            pltpu.VMEM((PRE_SUB * TILE, PRE_COLS), F32),
            pltpu.VMEM((PRE_SUB * TILE, PRE_COLS), F32),
            pltpu.VMEM((N_HEADS_B * HEAD_K, HEAD_V), F32),
        ],
        compiler_params=params,
    )(x.reshape(n_steps, PRE_SUB * TILE, D_MODEL), wts["npre"], wts["wkv"], wts["wgk_b"],
      wts["bgk_b"], wts["utri"])

    m_steps = n_total // MAIN_SUB

    def finished_block(i):
        return (jnp.maximum(i - 1, 0), 0, 0)

    y = pl.pallas_call(
        functools.partial(_layer_kernel, n_tiles // MAIN_SUB, m_steps),
        grid=(m_steps + 1,),
        in_specs=[
            pl.BlockSpec((1, MAIN_SUB * TILE, D_MODEL),
                         lambda i: (jnp.minimum(i, m_steps - 1), 0, 0)),
            pl.BlockSpec((MAIN_SUB, TILE, KLB_COLS), finished_block),
            pl.BlockSpec((MAIN_SUB, TILE + N_HEADS_B * HEAD_K, D_B), finished_block),
            _const_spec((1, D_MODEL)),
            _const_spec((D_MODEL, P_COLS)),
            _const_spec((N_HEADS_A // 2, CHUNK_A, 2 * CHUNK_A)),
            _const_spec((N_HEADS_A, CHUNK_A, HEAD_A)),
            _const_spec((1, D_A)),
            _const_spec((LANES, D_K)),
            _const_spec((1, D_K)),
            _const_spec((TILE, TILE)),
            _const_spec((1, HEAD_V)),
            _const_spec((D_A + D_B, D_MODEL + PIECE_COLS)),
            _const_spec((1, D_MODEL)),
        ],
        out_specs=pl.BlockSpec((1, MAIN_SUB * TILE, D_MODEL), finished_block),
        out_shape=jax.ShapeDtypeStruct((m_steps, MAIN_SUB * TILE, D_MODEL), x.dtype),
        scratch_shapes=[
            pltpu.VMEM((MAIN_SUB * TILE, P_COLS), F32),
            pltpu.VMEM((MAIN_SUB * TILE, P_COLS), F32),
            pltpu.VMEM((MAIN_SUB * TILE, D_MODEL), F32),
            pltpu.VMEM((MAIN_SUB * TILE, D_MODEL), F32),
            pltpu.VMEM((N_HEADS_B * HEAD_K, HEAD_V), F32),
            pltpu.VMEM((MAIN_SUB, TILE, D_B), F32),
            pltpu.VMEM((MAIN_SUB, TILE, D_A + D_B), BF16),
        ],
        compiler_params=params,
    )(x.reshape(m_steps, MAIN_SUB * TILE, D_MODEL), klr_tiles, vs_tiles, wts["npre"], wts["win"],
      wts["wsp"], wts["bsp"], wts["gva"], wts["wgk_f"], wts["bgk_f"], wts["ltri"], wts["gnb"],
      wts["wout"], wts["npost"])
    return y.reshape(x.shape)


def _prepare_weights(norm_pre, w_in, w_sp, b_sp, g_v_a, w_gk_fwd, b_gk_fwd, w_gk_bwd, b_gk_bwd,
                     g_norm_b, w_out, norm_post):
    w_in_bf = w_in.astype(BF16)
    wlr = jnp.zeros((D_MODEL, PIECE_COLS), BF16).at[:, 0:2 * GATE_RANK].set(w_in_bf[:, OFF_LRF:D_IN])
    wgk = jnp.zeros((LANES, 2 * D_K), BF16)
    wgk = wgk.at[0:GATE_RANK, 0:D_K].set(w_gk_fwd.astype(BF16))
    wgk = wgk.at[GATE_RANK:2 * GATE_RANK, D_K:2 * D_K].set(w_gk_bwd.astype(BF16))
    pad_cols = jnp.zeros((D_MODEL, PIECE_COLS), BF16)
    pos = jnp.arange(TILE)
    same_chunk = (pos[:, None] // CHUNK_B) == (pos[None, :] // CHUNK_B)
    ltri = (same_chunk & (pos[None, :] <= pos[:, None])).astype(BF16)
    utri = (same_chunk & (pos[None, :] >= pos[:, None])).astype(BF16)
    return {
        "npre": norm_pre.reshape(1, D_MODEL),
        "win": jnp.concatenate([w_in_bf[:, OFF_U:OFF_K], w_in_bf[:, OFF_ZB:OFF_LRF]], axis=1),
        "wkv": jnp.concatenate([w_in_bf[:, OFF_K:OFF_ZB], wlr, pad_cols], axis=1),
        "wsp": jnp.concatenate([w_sp[0::2], w_sp[1::2]], axis=-1).astype(BF16),
        "bsp": jnp.broadcast_to(b_sp[:, :, None], (N_HEADS_A, CHUNK_A, HEAD_A)),
        "gva": g_v_a.reshape(1, D_A),
        "wgk_f": wgk[:, 0:D_K],
        "wgk_b": wgk[:, D_K:2 * D_K],
        "bgk_f": b_gk_fwd.reshape(1, D_K),
        "bgk_b": b_gk_bwd.reshape(1, D_K),
        "ltri": ltri,
        "utri": utri,
        "gnb": g_norm_b.reshape(1, HEAD_V),
        "wout": jnp.concatenate([w_out.astype(BF16), pad_cols], axis=1),
        "npost": norm_post.reshape(1, D_MODEL),
    }


def kernel(x_prompt, x_sample, norm_pre, w_in, w_sp, b_sp, g_v_a, w_gk_fwd, b_gk_fwd,
           w_gk_bwd, b_gk_bwd, g_norm_b, w_out, norm_post):
    y_prompt, y_sample = x_prompt, x_sample
    for l in range(norm_pre.shape[0]):
        wts = _prepare_weights(norm_pre[l], w_in[l], w_sp[l], b_sp[l], g_v_a[l], w_gk_fwd[l],
                               b_gk_fwd[l], w_gk_bwd[l], b_gk_bwd[l], g_norm_b[l], w_out[l],
                               norm_post[l])
        y_prompt = _hybrid_layer(y_prompt, wts)
        y_sample = _hybrid_layer(y_sample, wts)
    return (y_prompt, y_sample)
```
